```python
import jax, jax.numpy as jnp
from jax import lax
import numpy as np

D_MODEL = 1024
BATCH = 16
SEQ = 2048
DEPTH = 1

CHUNK = 64
Q_BLOCK = 128
MEM_LEN = 256
D_MIX = D_MODEL
MLA_HEADS = 8
MLA_NOPE = 64
MLA_ROPE = 32
MLA_V = 64
MLA_WIDTH = MLA_HEADS * MLA_V
Q_LORA = 256
KV_LORA = 128
ROPE_THETA = 10000.0
CONV_CH = D_MIX - MLA_WIDTH
CONV_WIDTH = 31
IN_COLS = Q_LORA + KV_LORA + MLA_ROPE + 2 * CONV_CH
X_HEADS = 4
X_HEAD_DIM = D_MODEL // X_HEADS
N_EXPERTS = 32
TOP_K = 4
D_FF_EXPERT = D_MODEL
SWIGLU_LIMIT = 7.0
SWIGLU_ALPHA = 1.702
MOE_BLOCK = 256
NORM_EPS = 1e-5
MAX_OFFSET = 4096

kernel_name = "hybrid_mla_conformer_memxattn_moe"


def rmsnorm(x, g):
    xf = x.astype(jnp.float32)
    y = xf * lax.rsqrt(jnp.mean(xf * xf, axis=-1, keepdims=True) + NORM_EPS)
    return (y * g.astype(jnp.float32)).astype(x.dtype)


def layernorm(x, g, b):
    xf = x.astype(jnp.float32)
    mu = jnp.mean(xf, axis=-1, keepdims=True)
    var = jnp.mean(jnp.square(xf - mu), axis=-1, keepdims=True)
    y = (xf - mu) * lax.rsqrt(var + NORM_EPS)
    return (y * g.astype(jnp.float32) + b.astype(jnp.float32)).astype(x.dtype)


def rope_angles(pos):
    inv = ROPE_THETA ** (-jnp.arange(0, MLA_ROPE, 2, dtype=jnp.float32) / MLA_ROPE)
    ang = pos.astype(jnp.float32)[..., None] * inv
    return jnp.cos(ang), jnp.sin(ang)


def apply_rope(x, cos, sin):
    half = x.shape[-1] // 2
    x1 = x[..., :half].astype(jnp.float32)
    x2 = x[..., half:].astype(jnp.float32)
    return jnp.concatenate([x1 * cos - x2 * sin, x1 * sin + x2 * cos], axis=-1).astype(x.dtype)


def mla_attention(c_q, c_kv, k_rope_raw, pos, g_q, w_uq, g_kv, w_ukv):
    B, S, _ = c_q.shape
    q = (rmsnorm(c_q, g_q) @ w_uq).reshape(B, S, MLA_HEADS, MLA_NOPE + MLA_ROPE)
    q_nope, q_rope = q[..., :MLA_NOPE], q[..., MLA_NOPE:]
    kv = (rmsnorm(c_kv, g_kv) @ w_ukv).reshape(B, S, MLA_HEADS, MLA_NOPE + MLA_V)
    k_nope, v = kv[..., :MLA_NOPE], kv[..., MLA_NOPE:]
    cos, sin = rope_angles(pos)
    q_rope = apply_rope(q_rope, cos[:, :, None, :], sin[:, :, None, :])
    k_rope = apply_rope(k_rope_raw, cos, sin)
    scale = (MLA_NOPE + MLA_ROPE) ** -0.5
    chunk_id = jnp.arange(S) // CHUNK
    outs = []
    for blk in range(S // Q_BLOCK):
        q0, q1 = blk * Q_BLOCK, (blk + 1) * Q_BLOCK
        s = (jnp.einsum('bqhd,bkhd->bhqk', q_nope[:, q0:q1], k_nope[:, :q1])
             + jnp.einsum('bqhr,bkr->bhqk', q_rope[:, q0:q1], k_rope[:, :q1]))
        s = s.astype(jnp.float32) * scale
        mask = chunk_id[None, :q1] <= chunk_id[q0:q1, None]
        s = jnp.where(mask[None, None], s, -jnp.inf)
        p = jax.nn.softmax(s, axis=-1).astype(v.dtype)
        outs.append(jnp.einsum('bhqk,bkhd->bqhd', p, v[:, :q1]))
    return jnp.concatenate(outs, axis=1).reshape(B, S, MLA_WIDTH)


def conformer_conv(u, w_dw, b_dw, g_ln, b_ln):
    a, gate = u[..., :CONV_CH], u[..., CONV_CH:]
    z = a * jax.nn.sigmoid(gate)
    z = lax.conv_general_dilated(
        z, w_dw[:, None, :], window_strides=(1,), padding=[(CONV_WIDTH - 1, 0)],
        dimension_numbers=('NWC', 'WIO', 'NWC'), feature_group_count=CONV_CH) + b_dw
    return jax.nn.silu(layernorm(z, g_ln, b_ln))


def memory_cross_attention(h, mem_n, w_xq, w_xkv, w_xo):
    B, S, _ = h.shape
    M = mem_n.shape[1]
    q = (h @ w_xq).reshape(B, S, X_HEADS, X_HEAD_DIM)
    kv = (mem_n @ w_xkv).reshape(B, M, 2, X_HEADS, X_HEAD_DIM)
    k, v = kv[:, :, 0], kv[:, :, 1]
    s = jnp.einsum('bqhd,bmhd->bhqm', q, k).astype(jnp.float32) * (X_HEAD_DIM ** -0.5)
    p = jax.nn.softmax(s, axis=-1).astype(v.dtype)
    o = jnp.einsum('bhqm,bmhd->bqhd', p, v).reshape(B, S, D_MODEL)
    return o @ w_xo


def moe_ffn(h, w_router, b_router, w_up, b_up, w_down, b_down):
    B, S, D = h.shape
    N = B * S
    t = h.reshape(N, D)
    logits = (t @ w_router + b_router).astype(jnp.float32)
    top_val, top_idx = lax.top_k(logits, TOP_K)
    gates = jax.nn.softmax(top_val, axis=-1)
    A = N * TOP_K
    flat_e = top_idx.reshape(A)
    flat_tok = jnp.repeat(jnp.arange(N, dtype=jnp.int32), TOP_K)
    flat_g = gates.reshape(A)
    order = jnp.argsort(flat_e)
    se = flat_e[order]
    counts = jnp.bincount(flat_e, length=N_EXPERTS)
    start = jnp.cumsum(counts) - counts
    padded = (counts + MOE_BLOCK - 1) // MOE_BLOCK * MOE_BLOCK
    pend = jnp.cumsum(padded)
    pstart = pend - padded
    dest = pstart[se] + (jnp.arange(A) - start[se])
    P = A + N_EXPERTS * MOE_BLOCK
    nb = P // MOE_BLOCK
    slot_tok = jnp.full((P,), N, dtype=jnp.int32).at[dest].set(flat_tok[order])
    slot_gate = jnp.zeros((P,), jnp.float32).at[dest].set(flat_g[order])
    blk_e = jnp.minimum(jnp.searchsorted(pend, jnp.arange(nb) * MOE_BLOCK, side='right'),
                        N_EXPERTS - 1)
    t_pad = jnp.concatenate([t, jnp.zeros((1, D), t.dtype)], axis=0)

    def expert_block(args):
        tok, e = args
        xb = t_pad[tok]
        up = xb @ w_up[e] + b_up[e]
        glu, lin = up[:, :D_FF_EXPERT], up[:, D_FF_EXPERT:]
        glu = jnp.minimum(glu, SWIGLU_LIMIT)
        lin = jnp.clip(lin, -SWIGLU_LIMIT, SWIGLU_LIMIT)
        act = glu * jax.nn.sigmoid(SWIGLU_ALPHA * glu) * (lin + 1)
        return act @ w_down[e] + b_down[e]

    y = lax.map(expert_block, (slot_tok.reshape(nb, MOE_BLOCK), blk_e))
    y = y.reshape(P, D) * slot_gate[:, None].astype(y.dtype)
    out = jax.ops.segment_sum(y, slot_tok, num_segments=N + 1)[:N]
    return out.reshape(B, S, D)


def setup_inputs(seed: int = 0) -> dict:
    key = jax.random.key(seed)
    ks = jax.random.split(key, 32)
    f32 = jnp.float32
    L = DEPTH

    def w(k, shape, fan_in):
        return jax.random.normal(k, shape, f32) * (fan_in ** -0.5)

    def gain(k, shape):
        return 1.0 + 0.02 * jax.random.normal(k, shape, f32)

    def bias(k, shape):
        return 0.01 * jax.random.normal(k, shape, f32)

    x = jax.random.normal(ks[0], (BATCH, SEQ, D_MODEL), f32)
    mem = jax.random.normal(ks[1], (BATCH, MEM_LEN, D_MODEL), f32)
    offs = jax.random.randint(ks[2], (BATCH, 1), 0, MAX_OFFSET, dtype=jnp.int32)
    positions = offs + jnp.arange(SEQ, dtype=jnp.int32)[None, :]
    return {
        "x": x,
        "mem": mem,
        "positions": positions,
        "g_mix": gain(ks[3], (L, D_MODEL)),
        "w_in": w(ks[4], (L, D_MODEL, IN_COLS), D_MODEL),
        "g_q": gain(ks[5], (L, Q_LORA)),
        "w_uq": w(ks[6], (L, Q_LORA, MLA_HEADS * (MLA_NOPE + MLA_ROPE)), Q_LORA),
        "g_kv": gain(ks[7], (L, KV_LORA)),
        "w_ukv": w(ks[8], (L, KV_LORA, MLA_HEADS * (MLA_NOPE + MLA_V)), KV_LORA),
        "w_dw": w(ks[9], (L, CONV_WIDTH, CONV_CH), CONV_WIDTH),
        "b_dw": bias(ks[10], (L, CONV_CH)),
        "g_conv_ln": gain(ks[11], (L, CONV_CH)),
        "b_conv_ln": bias(ks[12], (L, CONV_CH)),
        "w_out": w(ks[13], (L, D_MIX, D_MODEL), D_MIX),
        "g_xattn": gain(ks[14], (L, D_MODEL)),
        "g_mem": gain(ks[15], (L, D_MODEL)),
        "w_xq": w(ks[16], (L, D_MODEL, D_MODEL), D_MODEL),
        "w_xkv": w(ks[17], (L, D_MODEL, 2 * D_MODEL), D_MODEL),
        "w_xo": w(ks[18], (L, D_MODEL, D_MODEL), D_MODEL),
        "g_ffn": gain(ks[19], (L, D_MODEL)),
        "w_router": w(ks[20], (L, D_MODEL, N_EXPERTS), D_MODEL),
        "b_router": bias(ks[21], (L, N_EXPERTS)),
        "w_up": w(ks[22], (L, N_EXPERTS, D_MODEL, 2 * D_FF_EXPERT), D_MODEL),
        "b_up": bias(ks[23], (L, N_EXPERTS, 2 * D_FF_EXPERT)),
        "w_down": w(ks[24], (L, N_EXPERTS, D_FF_EXPERT, D_MODEL), D_FF_EXPERT),
        "b_down": bias(ks[25], (L, N_EXPERTS, D_MODEL)),
        "g_final": gain(ks[26], (D_MODEL,)),
    }


def reference(x, mem, positions, g_mix, w_in, g_q, w_uq, g_kv, w_ukv, w_dw, b_dw,
              g_conv_ln, b_conv_ln, w_out, g_xattn, g_mem, w_xq, w_xkv, w_xo,
              g_ffn, w_router, b_router, w_up, b_up, w_down, b_down, g_final):
    o1 = Q_LORA
    o2 = o1 + KV_LORA
    o3 = o2 + MLA_ROPE
    for l in range(DEPTH):
        h = rmsnorm(x, g_mix[l])
        u = h @ w_in[l]
        y_mla = mla_attention(u[..., :o1], u[..., o1:o2], u[..., o2:o3], positions,
                              g_q[l], w_uq[l], g_kv[l], w_ukv[l])
        y_conv = conformer_conv(u[..., o3:], w_dw[l], b_dw[l], g_conv_ln[l], b_conv_ln[l])
        x = x + jnp.concatenate([y_mla, y_conv], axis=-1) @ w_out[l]
        x = x + memory_cross_attention(rmsnorm(x, g_xattn[l]), rmsnorm(mem, g_mem[l]),
                                       w_xq[l], w_xkv[l], w_xo[l])
        x = x + moe_ffn(rmsnorm(x, g_ffn[l]), w_router[l], b_router[l],
                        w_up[l], b_up[l], w_down[l], b_down[l])
    return rmsnorm(x, g_final)
```

```python
import functools

import jax
import jax.numpy as jnp
from jax import lax
from jax.experimental import pallas as pl
from jax.experimental.pallas import tpu as pltpu

F32 = jnp.float32
BF16 = jnp.bfloat16
I32 = jnp.int32

D_MODEL = 1024
MLA_HEADS = 8
MLA_NOPE = 64
MLA_ROPE = 32
MLA_V = 64
MLA_WIDTH = MLA_HEADS * MLA_V
Q_LORA = 256
KV_LORA = 128
ROPE_THETA = 10000.0
CHUNK = 64
CONV_CH = 512
CONV_WIDTH = 31
X_HEADS = 4
X_HEAD_DIM = D_MODEL // X_HEADS
N_EXPERTS = 32
TOP_K = 4
D_FF = D_MODEL
SWIGLU_LIMIT = 7.0
SWIGLU_ALPHA = 1.702
NORM_EPS = 1e-5

LANES = 128
SUBLANES = 8
ROW_TILES = D_MODEL // LANES
VMEM_LIMIT_BYTES = 56 * 1024 * 1024

HEAD_PAD = LANES
CONV_HALO = 32

TM_MIX = 512
TQ_ATTN = 256
TM_MID = 512
TM_DISP = 512
TB_EXPERT = 256
TM_COMB = 256
WAIT_BATCH = 64


def _rms(x, g):
    return x * lax.rsqrt(jnp.mean(x * x, axis=-1, keepdims=True) + NORM_EPS) * g


def _cparams(sem):
    return pltpu.CompilerParams(dimension_semantics=sem, vmem_limit_bytes=VMEM_LIMIT_BYTES)


def _mix_in_kernel(x_ref, pos_ref, inv_ref, sgn_ref, gmix_ref, win_ref, gq_ref, wuq_ref, wuqs_ref,
                   gkv_ref, wukk_ref, wukv_ref, wdw_ref, bdw_ref, gln_ref, bln_ref,
                   q_ref, k_ref, v_ref, yc_ref, zbuf, *, tiles_per_batch, tm):
    i = pl.program_id(0)
    x = x_ref[...]
    h = _rms(x, gmix_ref[...]).astype(BF16)
    u = jnp.dot(h, win_ref[...], preferred_element_type=F32)
    cq = u[:, 0:256]
    ckv = u[:, 256:384]
    kr = u[:, 384:512]
    krs = u[:, 512:640]
    a = u[:, 640:1152]
    gate = u[:, 1152:1664]

    ang = pos_ref[...].astype(F32) * inv_ref[...]
    cosb = jnp.cos(ang)
    sinb = jnp.sin(ang) * sgn_ref[...]

    cqn = _rms(cq, gq_ref[...]).astype(BF16)
    qm = jnp.dot(cqn, wuq_ref[...], preferred_element_type=F32)
    qs = jnp.dot(cqn, wuqs_ref[...], preferred_element_type=F32)
    for hd in range(MLA_HEADS):
        sl = slice(hd * HEAD_PAD, (hd + 1) * HEAD_PAD)
        q_ref[:, sl] = (qm[:, sl] * cosb + qs[:, sl] * sinb).astype(BF16)

    ckvn = _rms(ckv, gkv_ref[...]).astype(BF16)
    kk = jnp.dot(ckvn, wukk_ref[...], preferred_element_type=F32)
    v_ref[...] = jnp.dot(ckvn, wukv_ref[...], preferred_element_type=F32).astype(BF16)
    krot = kr * cosb + krs * sinb
    for hd in range(MLA_HEADS):
        sl = slice(hd * HEAD_PAD, (hd + 1) * HEAD_PAD)
        k_ref[:, sl] = (kk[:, sl] + krot).astype(BF16)

    z = a * jax.nn.sigmoid(gate)

    @pl.when(i % tiles_per_batch == 0)
    def _():
        zbuf[0:CONV_HALO, :] = jnp.zeros((CONV_HALO, CONV_CH), F32)

    zbuf[CONV_HALO:CONV_HALO + tm, :] = z
    off = CONV_HALO - (CONV_WIDTH - 1)
    rows = 32
    for r0 in range(0, tm, rows):
        acc = jnp.zeros((rows, CONV_CH), F32) + bdw_ref[...]
        for j in range(CONV_WIDTH):
            acc = acc + wdw_ref[j:j + 1, :] * zbuf[r0 + off + j:r0 + off + j + rows, :]
        mu = jnp.mean(acc, axis=-1, keepdims=True)
        cen = acc - mu
        var = jnp.mean(cen * cen, axis=-1, keepdims=True)
        y = cen * lax.rsqrt(var + NORM_EPS) * gln_ref[...] + bln_ref[...]
        yc_ref[r0:r0 + rows, :] = (y * jax.nn.sigmoid(y)).astype(BF16)
    zbuf[0:CONV_HALO, :] = zbuf[tm:tm + CONV_HALO, :]


def _mix_in(x2d, pos2d, inv_l, sgn_l, g_mix, w_in_p, g_q, w_uq_p, w_uq_s, g_kv, w_uk_k, w_uk_v,
            w_dw, b_dw, g_ln, b_ln, *, seq):
    n = x2d.shape[0]
    tm = min(TM_MIX, seq)
    full = lambda a: pl.BlockSpec(a.shape, lambda i: (0,) * a.ndim)
    consts = [inv_l, sgn_l, g_mix, w_in_p, g_q, w_uq_p, w_uq_s, g_kv, w_uk_k, w_uk_v, w_dw, b_dw, g_ln, b_ln]
    return pl.pallas_call(
        functools.partial(_mix_in_kernel, tiles_per_batch=seq // tm, tm=tm),
        grid=(n // tm,),
        in_specs=[pl.BlockSpec((tm, D_MODEL), lambda i: (i, 0)),
                  pl.BlockSpec((tm, 1), lambda i: (i, 0))] + [full(a) for a in consts],
        out_specs=[pl.BlockSpec((tm, MLA_HEADS * HEAD_PAD), lambda i: (i, 0)),
                   pl.BlockSpec((tm, MLA_HEADS * HEAD_PAD), lambda i: (i, 0)),
                   pl.BlockSpec((tm, MLA_WIDTH), lambda i: (i, 0)),
                   pl.BlockSpec((tm, CONV_CH), lambda i: (i, 0))],
        out_shape=[jax.ShapeDtypeStruct((n, MLA_HEADS * HEAD_PAD), BF16),
                   jax.ShapeDtypeStruct((n, MLA_HEADS * HEAD_PAD), BF16),
                   jax.ShapeDtypeStruct((n, MLA_WIDTH), BF16),
                   jax.ShapeDtypeStruct((n, CONV_CH), BF16)],
        scratch_shapes=[pltpu.VMEM((CONV_HALO + tm + SUBLANES, CONV_CH), F32)],
        compiler_params=_cparams(("arbitrary",)),
        name="mix_in",
    )(x2d, pos2d, *consts)


def _attn_kernel(q_ref, k_ref, v_ref, o_ref, *, tq):
    i = pl.program_id(2)
    scale = (MLA_NOPE + MLA_ROPE) ** -0.5
    nt = (((1,), (1,)), ((), ()))
    row_c = lax.broadcasted_iota(I32, (tq, tq), 0) // CHUNK
    col_c = lax.broadcasted_iota(I32, (tq, tq), 1) // CHUNK
    diag_mask = col_c <= row_c
    outs = []
    for hh in range(2):
        lsl = slice(hh * HEAD_PAD, (hh + 1) * HEAD_PAD)
        q = q_ref[:, lsl]

        def step(j, carry, masked):
            m, l, acc = carry
            start = pl.multiple_of(j * tq, tq)
            kj = k_ref[pl.ds(start, tq), lsl]
            vj = v_ref[pl.ds(start, tq), :]
            s = lax.dot_general(q, kj, nt, preferred_element_type=F32) * scale
            if masked:
                s = jnp.where(diag_mask, s, -jnp.inf)
            m_new = jnp.maximum(m, jnp.max(s, axis=-1, keepdims=True))
            alpha = jnp.exp(m - m_new)
            p = jnp.exp(s - m_new)
            l = alpha * l + jnp.sum(p, axis=-1, keepdims=True)
            acc = alpha * acc + jnp.dot(p.astype(BF16), vj, preferred_element_type=F32)
            return m_new, l, acc

        init = (jnp.full((tq, 1), -jnp.inf, F32), jnp.zeros((tq, 1), F32), jnp.zeros((tq, LANES), F32))
        carry = lax.fori_loop(0, i, functools.partial(step, masked=False), init)
        m, l, acc = step(i, carry, True)
        outs.append(acc / l)
    lane = lax.broadcasted_iota(I32, (tq, LANES), 1)
    o_ref[...] = jnp.where(lane < MLA_V, outs[0], outs[1]).astype(BF16)


def _attn(q, k, v, *, batch, seq):
    tq = min(TQ_ATTN, seq)
    nq = seq // tq
    return pl.pallas_call(
        functools.partial(_attn_kernel, tq=tq),
        grid=(batch, MLA_HEADS // 2, nq),
        in_specs=[pl.BlockSpec((tq, 2 * HEAD_PAD), lambda b, hp, i: (b * nq + i, hp)),
                  pl.BlockSpec((seq, 2 * HEAD_PAD), lambda b, hp, i: (b, hp)),
                  pl.BlockSpec((seq, 2 * MLA_V), lambda b, hp, i: (b, hp))],
        out_specs=pl.BlockSpec((tq, 2 * MLA_V), lambda b, hp, i: (b * nq + i, hp)),
        out_shape=jax.ShapeDtypeStruct((batch * seq, MLA_WIDTH), BF16),
        compiler_params=_cparams(("arbitrary", "arbitrary", "arbitrary")),
        name="mla_attn",
    )(q, k, v)


def _mem_kv_kernel(mem_ref, g_ref, w_ref, kv_ref):
    mn = _rms(mem_ref[...], g_ref[...]).astype(BF16)
    kv_ref[...] = jnp.dot(mn, w_ref[...], preferred_element_type=F32).astype(BF16)


def _mem_kv(mem2d, g_mem, w_xkv, *, batch, mem_len):
    return pl.pallas_call(
        _mem_kv_kernel,
        grid=(batch,),
        in_specs=[pl.BlockSpec((mem_len, D_MODEL), lambda b: (b, 0)),
                  pl.BlockSpec((1, D_MODEL), lambda b: (0, 0)),
                  pl.BlockSpec((D_MODEL, 2 * D_MODEL), lambda b: (0, 0))],
        out_specs=pl.BlockSpec((mem_len, 2 * D_MODEL), lambda b: (b, 0)),
        out_shape=jax.ShapeDtypeStruct((batch * mem_len, 2 * D_MODEL), BF16),
        compiler_params=_cparams(("arbitrary",)),
        name="mem_kv",
    )(mem2d, g_mem, w_xkv)


def _mid_kernel(x_ref, ya_ref, yc_ref, kv_ref, woa_ref, wob_ref, gx_ref, wxq_ref, wxo_ref, gf_ref,
                wrh_ref, wrl_ref, br_ref, tri_ref,
                x2_ref, h3_ref, idx_ref, gate_ref, rank_ref, cnt_ref, run_ref, *, tm):
    i = pl.program_id(0)

    @pl.when(i == 0)
    def _():
        run_ref[...] = jnp.zeros_like(run_ref)

    x1 = (x_ref[...]
          + jnp.dot(ya_ref[...], woa_ref[...], preferred_element_type=F32)
          + jnp.dot(yc_ref[...], wob_ref[...], preferred_element_type=F32))
    h = _rms(x1, gx_ref[...]).astype(BF16)
    q = jnp.dot(h, wxq_ref[...], preferred_element_type=F32).astype(BF16)
    nt = (((1,), (1,)), ((), ()))
    os = []
    for hd in range(X_HEADS):
        sl = slice(hd * X_HEAD_DIM, (hd + 1) * X_HEAD_DIM)
        vsl = slice(D_MODEL + hd * X_HEAD_DIM, D_MODEL + (hd + 1) * X_HEAD_DIM)
        s = lax.dot_general(q[:, sl], kv_ref[:, sl], nt, preferred_element_type=F32) * (X_HEAD_DIM ** -0.5)
        e = jnp.exp(s - jnp.max(s, axis=-1, keepdims=True))
        p = (e / jnp.sum(e, axis=-1, keepdims=True)).astype(BF16)
        os.append(jnp.dot(p, kv_ref[:, vsl], preferred_element_type=F32).astype(BF16))
    o = jnp.concatenate(os, axis=-1)
    x2 = x1 + jnp.dot(o, wxo_ref[...], preferred_element_type=F32)
    x2_ref[...] = x2
    h3 = _rms(x2, gf_ref[...])
    for t in range(ROW_TILES):
        h3_ref[:, t, :] = h3[:, t * LANES:(t + 1) * LANES]

    hh = h3.astype(BF16)
    hl = (h3 - hh.astype(F32)).astype(BF16)
    lg = (jnp.dot(hh, wrh_ref[...], preferred_element_type=F32)
          + (jnp.dot(hh, wrl_ref[...], preferred_element_type=F32)
             + jnp.dot(hl, wrh_ref[...], preferred_element_type=F32)))
    lgt = jnp.transpose(lg)[0:N_EXPERTS, :] + br_ref[...]
    eid = lax.broadcasted_iota(I32, (N_EXPERTS, tm), 0)
    vals, hots = [], []
    cur = lgt
    for k in range(TOP_K):
        mx = jnp.max(cur, axis=0, keepdims=True)
        ik = jnp.min(jnp.where(cur == mx, eid, N_EXPERTS), axis=0, keepdims=True)
        hot = eid == ik
        cur = jnp.where(hot, -jnp.inf, cur)
        idx_ref[k:k + 1, :] = ik
        vals.append(mx)
        hots.append(hot)
    es = [jnp.exp(v - vals[0]) for v in vals]
    den = es[0] + es[1] + es[2] + es[3]
    for k in range(TOP_K):
        gate_ref[k:k + 1, :] = es[k] / den
    cnt = jnp.zeros((N_EXPERTS, tm), F32)
    for k in range(TOP_K):
        cnt = cnt + hots[k].astype(F32)
    run = run_ref[:, 0:1]
    tot = jnp.dot(cnt.astype(BF16), tri_ref[...], preferred_element_type=F32) + run
    for k in range(TOP_K):
        rank_ref[k:k + 1, :] = jnp.sum(jnp.where(hots[k], tot, 0.0), axis=0, keepdims=True).astype(I32)
    new_run = run + jnp.sum(cnt, axis=1, keepdims=True)
    run_ref[...] = jnp.broadcast_to(new_run, run_ref.shape)
    cnt_ref[...] = jnp.broadcast_to(new_run, cnt_ref.shape).astype(I32)


def _mid(x2d, ya, yc, kv, w_oa, w_ob, g_x, w_xq, w_xo, g_f, wr_h, wr_l, b_r, tri, *, seq, mem_len, tm):
    n = x2d.shape[0]
    tpb = seq // tm
    full = lambda a: pl.BlockSpec(a.shape, lambda i: (0,) * a.ndim)
    consts_a = [w_oa, w_ob, g_x, w_xq, w_xo, g_f, wr_h, wr_l, b_r, tri]
    return pl.pallas_call(
        functools.partial(_mid_kernel, tm=tm),
        grid=(n // tm,),
        in_specs=[pl.BlockSpec((tm, D_MODEL), lambda i: (i, 0)),
                  pl.BlockSpec((tm, MLA_WIDTH), lambda i: (i, 0)),
                  pl.BlockSpec((tm, CONV_CH), lambda i: (i, 0)),
                  pl.BlockSpec((mem_len, 2 * D_MODEL), lambda i: (i // tpb, 0))] + [full(a) for a in consts_a],
        out_specs=[pl.BlockSpec((tm, D_MODEL), lambda i: (i, 0)),
                   pl.BlockSpec((tm, ROW_TILES, LANES), lambda i: (i, 0, 0)),
                   pl.BlockSpec((TOP_K, tm), lambda i: (0, i)),
                   pl.BlockSpec((TOP_K, tm), lambda i: (0, i)),
                   pl.BlockSpec((TOP_K, tm), lambda i: (0, i)),
                   pl.BlockSpec((N_EXPERTS, LANES), lambda i: (0, 0))],
        out_shape=[jax.ShapeDtypeStruct((n, D_MODEL), F32),
                   jax.ShapeDtypeStruct((n, ROW_TILES, LANES), F32),
                   jax.ShapeDtypeStruct((TOP_K, n), I32),
                   jax.ShapeDtypeStruct((TOP_K, n), F32),
                   jax.ShapeDtypeStruct((TOP_K, n), I32),
                   jax.ShapeDtypeStruct((N_EXPERTS, LANES), I32)],
        scratch_shapes=[pltpu.VMEM((N_EXPERTS, LANES), F32)],
        compiler_params=_cparams(("arbitrary",)),
        name="mid",
    )(x2d, ya, yc, kv, *consts_a)


def _pos_kernel(idx_ref, rank_ref, pstart_ref, pos_ref, *, tm):
    eid = lax.broadcasted_iota(I32, (N_EXPERTS, tm), 0)
    ps = pstart_ref[:, 0:1]
    for k in range(TOP_K):
        base = jnp.sum(jnp.where(eid == idx_ref[k:k + 1, :], ps, 0), axis=0, keepdims=True)
        pos_ref[k:k + 1, :] = base + rank_ref[k:k + 1, :]


def _pos(idx, rank, pstart_l, *, tm):
    n = idx.shape[1]
    return pl.pallas_call(
        functools.partial(_pos_kernel, tm=tm),
        grid=(n // tm,),
        in_specs=[pl.BlockSpec((TOP_K, tm), lambda i: (0, i)),
                  pl.BlockSpec((TOP_K, tm), lambda i: (0, i)),
                  pl.BlockSpec((N_EXPERTS, LANES), lambda i: (0, 0))],
        out_specs=pl.BlockSpec((TOP_K, tm), lambda i: (0, i)),
        out_shape=jax.ShapeDtypeStruct((TOP_K, n), I32),
        compiler_params=_cparams(("arbitrary",)),
        name="slot_pos",
    )(idx, rank, pstart_l)


def _row_copy(src_hbm, src_row, dst_hbm, dst_row, sem):
    return pltpu.make_async_copy(src_hbm.at[src_row], dst_hbm.at[dst_row], sem)


def _dispatch_kernel(pad_start_ref, pad_cnt_ref, pos_ref, h3_hbm, xs_hbm, zrow, sem, zsem, *, tm):
    i = pl.program_id(0)

    @pl.when(i == 0)
    def _():
        zrow[...] = jnp.zeros_like(zrow)

        def per_expert(e, c):
            s0 = pad_start_ref[e]

            def start_one(r, c2):
                pltpu.make_async_copy(zrow.at[0], xs_hbm.at[s0 + r], zsem).start()
                return c2

            lax.fori_loop(0, pad_cnt_ref[e], start_one, 0)

            def wait_one(r, c2):
                pltpu.make_async_copy(zrow.at[0], xs_hbm.at[s0 + r], zsem).wait()
                return c2

            lax.fori_loop(0, pad_cnt_ref[e], wait_one, 0)
            return c

        lax.fori_loop(0, N_EXPERTS, per_expert, 0)

    base = i * tm

    def start_tok(t, c):
        for k in range(TOP_K):
            _row_copy(h3_hbm, base + t, xs_hbm, pos_ref[k, t], sem).start()
        return c

    lax.fori_loop(0, tm, start_tok, 0)

    def wait_batch(w, c):
        for _ in range(WAIT_BATCH):
            _row_copy(h3_hbm, 0, xs_hbm, 0, sem).wait()
        return c

    lax.fori_loop(0, tm * TOP_K // WAIT_BATCH, wait_batch, 0)


def _dispatch(pad_start, pad_cnt, pos, h3r, *, n_slots, tm):
    n = h3r.shape[0]
    gs = pltpu.PrefetchScalarGridSpec(
        num_scalar_prefetch=2,
        grid=(n // tm,),
        in_specs=[pl.BlockSpec((TOP_K, tm), lambda i, a, b: (0, i), memory_space=pltpu.SMEM),
                  pl.BlockSpec(memory_space=pl.ANY)],
        out_specs=pl.BlockSpec(memory_space=pl.ANY),
        scratch_shapes=[pltpu.VMEM((1, ROW_TILES, LANES), F32),
                        pltpu.SemaphoreType.DMA(()),
                        pltpu.SemaphoreType.DMA(())],
    )
    return pl.pallas_call(
        functools.partial(_dispatch_kernel, tm=tm),
        grid_spec=gs,
        out_shape=jax.ShapeDtypeStruct((n_slots, ROW_TILES, LANES), F32),
        compiler_params=_cparams(("arbitrary",)),
        name="dispatch",
    )(pad_start, pad_cnt, pos, h3r)


def _expert_kernel(blk_e_ref, nused_ref, x_ref, wu_ref, bu_ref, wd_ref, bd_ref, y_ref, wu_bf, wd_bf, *, tb):
    b = pl.program_id(0)

    @pl.when(b < nused_ref[0])
    def _():
        prev = blk_e_ref[jnp.maximum(b - 1, 0)]

        @pl.when(jnp.logical_or(b == 0, blk_e_ref[b] != prev))
        def _():
            wu_bf[...] = wu_ref[...].astype(BF16)
            wd_bf[...] = wd_ref[...].astype(BF16)

        x = jnp.concatenate([x_ref[:, t, :] for t in range(ROW_TILES)], axis=-1).astype(BF16)
        up = jnp.dot(x, wu_bf[...], preferred_element_type=F32) + bu_ref[...]
        glu = jnp.minimum(up[:, :D_FF], SWIGLU_LIMIT)
        lin = jnp.clip(up[:, D_FF:], -SWIGLU_LIMIT, SWIGLU_LIMIT)
        act = (glu * jax.nn.sigmoid(SWIGLU_ALPHA * glu) * (lin + 1.0)).astype(BF16)
        y = jnp.dot(act, wd_bf[...], preferred_element_type=F32) + bd_ref[...]
        for t in range(ROW_TILES):
            y_ref[:, t, :] = y[:, t * LANES:(t + 1) * LANES]


def _expert(blk_e, nused, xs, w_up, b_up, w_down, b_down, *, tb):
    n_slots = xs.shape[0]
    nb = n_slots // tb
    blk = lambda b, be, nu: jnp.minimum(b, nu[0] - 1)
    gs = pltpu.PrefetchScalarGridSpec(
        num_scalar_prefetch=2,
        grid=(nb,),
        in_specs=[pl.BlockSpec((tb, ROW_TILES, LANES), lambda b, be, nu: (blk(b, be, nu), 0, 0)),
                  pl.BlockSpec((None, D_MODEL, 2 * D_FF), lambda b, be, nu: (be[blk(b, be, nu)], 0, 0)),
                  pl.BlockSpec((None, 1, 2 * D_FF), lambda b, be, nu: (be[blk(b, be, nu)], 0, 0)),
                  pl.BlockSpec((None, D_FF, D_MODEL), lambda b, be, nu: (be[blk(b, be, nu)], 0, 0)),
                  pl.BlockSpec((None, 1, D_MODEL), lambda b, be, nu: (be[blk(b, be, nu)], 0, 0))],
        out_specs=pl.BlockSpec((tb, ROW_TILES, LANES), lambda b, be, nu: (blk(b, be, nu), 0, 0)),
        scratch_shapes=[pltpu.VMEM((D_MODEL, 2 * D_FF), BF16),
                        pltpu.VMEM((D_FF, D_MODEL), BF16)],
    )
    return pl.pallas_call(
        functools.partial(_expert_kernel, tb=tb),
        grid_spec=gs,
        out_shape=jax.ShapeDtypeStruct((n_slots, ROW_TILES, LANES), F32),
        compiler_params=_cparams(("arbitrary",)),
        name="expert_ffn",
    )(blk_e, nused, xs, w_up, b_up, w_down, b_down)


def _combine_kernel(pos_ref, x2_ref, gate_ref, gfin_ref, y_hbm, out_ref, buf, sem, *, tm):
    def start_tok(t, c):
        for k in range(TOP_K):
            pltpu.make_async_copy(y_hbm.at[pos_ref[k, t]], buf.at[k, t], sem).start()
        return c

    lax.fori_loop(0, tm, start_tok, 0)

    def wait_batch(w, c):
        for _ in range(WAIT_BATCH):
            pltpu.make_async_copy(y_hbm.at[0], buf.at[0, 0], sem).wait()
        return c

    lax.fori_loop(0, tm * TOP_K // WAIT_BATCH, wait_batch, 0)

    g = gate_ref[...]
    pieces = []
    for t in range(ROW_TILES):
        acc = x2_ref[:, t * LANES:(t + 1) * LANES]
        for k in range(TOP_K):
            acc = acc + g[:, k:k + 1] * buf[k, :, t, :]
        pieces.append(acc)
    x3 = jnp.concatenate(pieces, axis=-1)
    out_ref[...] = _rms(x3, gfin_ref[...])


def _combine(pos, x2, gate_nt, g_final, y, *, tm):
    n = x2.shape[0]
    return pl.pallas_call(
        functools.partial(_combine_kernel, tm=tm),
        grid=(n // tm,),
        in_specs=[pl.BlockSpec((TOP_K, tm), lambda i: (0, i), memory_space=pltpu.SMEM),
                  pl.BlockSpec((tm, D_MODEL), lambda i: (i, 0)),
                  pl.BlockSpec((tm, TOP_K), lambda i: (i, 0)),
                  pl.BlockSpec((1, D_MODEL), lambda i: (0, 0)),
                  pl.BlockSpec(memory_space=pl.ANY)],
        out_specs=pl.BlockSpec((tm, D_MODEL), lambda i: (i, 0)),
        out_shape=jax.ShapeDtypeStruct((n, D_MODEL), F32),
        scratch_shapes=[pltpu.VMEM((TOP_K, tm, ROW_TILES, LANES), F32),
                        pltpu.SemaphoreType.DMA(())],
        compiler_params=_cparams(("arbitrary",)),
        name="combine",
    )(pos, x2, gate_nt, g_final, y)


def _head_pad_cols(w, per_head_in, take, place):
    kdim = w.shape[0]
    w3 = w.reshape(kdim, MLA_HEADS, per_head_in)
    out = jnp.zeros((kdim, MLA_HEADS, HEAD_PAD), w.dtype)
    for (t0, t1), p0 in zip(take, place):
        out = out.at[:, :, p0:p0 + (t1 - t0)].set(w3[:, :, t0:t1])
    return out.reshape(kdim, MLA_HEADS * HEAD_PAD)


def kernel(x, mem, positions, g_mix, w_in, g_q, w_uq, g_kv, w_ukv, w_dw, b_dw, g_conv_ln, b_conv_ln, w_out,
           g_xattn, g_mem, w_xq, w_xkv, w_xo, g_ffn, w_router, b_router, w_up, b_up, w_down, b_down, g_final):
    batch, seq, _ = x.shape
    mem_len = mem.shape[1]
    n = batch * seq
    half = MLA_ROPE // 2
    r0 = MLA_NOPE
    assert w_in.shape[0] == 1, "one trunk layer"

    row = lambda v: v.reshape(1, -1).astype(F32)

    wi = w_in[0]
    o1, o2, o3 = Q_LORA, Q_LORA + KV_LORA, Q_LORA + KV_LORA + MLA_ROPE
    kr_blk = jnp.zeros((D_MODEL, HEAD_PAD), F32).at[:, r0:r0 + MLA_ROPE].set(wi[:, o2:o3])
    kr_swp = (jnp.zeros((D_MODEL, HEAD_PAD), F32)
              .at[:, r0:r0 + half].set(wi[:, o2 + half:o3])
              .at[:, r0 + half:r0 + MLA_ROPE].set(wi[:, o2:o2 + half]))
    w_in_p = jnp.concatenate([wi[:, :o2], kr_blk, kr_swp, wi[:, o3:]], axis=1).astype(BF16)
    per_q = MLA_NOPE + MLA_ROPE
    w_uq_p = _head_pad_cols(w_uq[0], per_q, [(0, per_q)], [0]).astype(BF16)
    w_uq_s = _head_pad_cols(w_uq[0], per_q, [(MLA_NOPE + half, per_q), (MLA_NOPE, MLA_NOPE + half)],
                            [r0, r0 + half]).astype(BF16)
    per_kv = MLA_NOPE + MLA_V
    w_uk_k = _head_pad_cols(w_ukv[0], per_kv, [(0, MLA_NOPE)], [0]).astype(BF16)
    w_uk_v = (w_ukv[0].reshape(KV_LORA, MLA_HEADS, per_kv)[:, :, MLA_NOPE:]
              .reshape(KV_LORA, MLA_WIDTH).astype(BF16))
    inv = ROPE_THETA ** (-jnp.arange(0, MLA_ROPE, 2, dtype=F32) / MLA_ROPE)
    inv_l = (jnp.zeros((1, LANES), F32).at[0, r0:r0 + half].set(inv).at[0, r0 + half:r0 + MLA_ROPE].set(inv))
    sgn_l = (jnp.zeros((1, LANES), F32).at[0, r0:r0 + half].set(-1.0).at[0, r0 + half:r0 + MLA_ROPE].set(1.0))

    x2d = x.reshape(n, D_MODEL)
    pos2d = positions.reshape(n, 1).astype(I32)

    q, k, v, y_conv = _mix_in(x2d, pos2d, inv_l, sgn_l, row(g_mix[0]), w_in_p, row(g_q[0]), w_uq_p, w_uq_s,
                              row(g_kv[0]), w_uk_k, w_uk_v, w_dw[0].astype(F32), row(b_dw[0]),
                              row(g_conv_ln[0]), row(b_conv_ln[0]), seq=seq)
    y_mla = _attn(q, k, v, batch=batch, seq=seq)
    kv = _mem_kv(mem.reshape(batch * mem_len, D_MODEL), row(g_mem[0]), w_xkv[0].astype(BF16),
                 batch=batch, mem_len=mem_len)

    tm_mid = min(TM_MID, seq)
    wr = jnp.zeros((D_MODEL, LANES), F32).at[:, :N_EXPERTS].set(w_router[0])
    wr_h = wr.astype(BF16)
    wr_l = (wr - wr_h.astype(F32)).astype(BF16)
    b_r = jnp.broadcast_to(b_router[0].reshape(N_EXPERTS, 1), (N_EXPERTS, LANES)).astype(F32)[:, 0:1]
    tri = (lax.broadcasted_iota(I32, (tm_mid, tm_mid), 0)
           < lax.broadcasted_iota(I32, (tm_mid, tm_mid), 1)).astype(BF16)
    wo = w_out[0].astype(BF16)
    x2, h3r, idx, gate, rank, cnt = _mid(
        x2d, y_mla, y_conv, kv, wo[:MLA_WIDTH], wo[MLA_WIDTH:], row(g_xattn[0]), w_xq[0].astype(BF16),
        w_xo[0].astype(BF16), row(g_ffn[0]), wr_h, wr_l, b_r, tri, seq=seq, mem_len=mem_len, tm=tm_mid)

    tb = TB_EXPERT
    counts = cnt[:, 0]
    padded = (counts + tb - 1) // tb * tb
    pend = jnp.cumsum(padded)
    pstart = pend - padded
    n_slots = n * TOP_K + N_EXPERTS * tb
    nb = n_slots // tb
    blk_first = jnp.arange(nb, dtype=I32) * tb
    blk_e = jnp.minimum(jnp.sum((pend[None, :] <= blk_first[:, None]).astype(I32), axis=1),
                        N_EXPERTS - 1).astype(I32)
    nused = (pend[-1:] // tb).astype(I32)
    pstart_l = jnp.broadcast_to(pstart.reshape(N_EXPERTS, 1), (N_EXPERTS, LANES)).astype(I32)

    pos = _pos(idx, rank, pstart_l, tm=tm_mid)
    xs = _dispatch((pstart + counts).astype(I32), (padded - counts).astype(I32), pos, h3r,
                   n_slots=n_slots, tm=min(TM_DISP, n))
    y = _expert(blk_e, nused, xs, w_up[0], b_up[0].reshape(N_EXPERTS, 1, 2 * D_FF),
                w_down[0], b_down[0].reshape(N_EXPERTS, 1, D_MODEL), tb=tb)
    out = _combine(pos, x2, jnp.transpose(gate), row(g_final), y, tm=min(TM_COMB, n))
    return out.reshape(batch, seq, D_MODEL)
```

```python
import functools

import jax
import jax.numpy as jnp
from jax import lax
from jax.experimental import pallas as pl
from jax.experimental.pallas import tpu as pltpu

F32 = jnp.float32
BF16 = jnp.bfloat16
I32 = jnp.int32

D_MODEL = 1024
MLA_HEADS = 8
MLA_NOPE = 64
MLA_ROPE = 32
MLA_V = 64
MLA_WIDTH = MLA_HEADS * MLA_V
Q_LORA = 256
KV_LORA = 128
ROPE_THETA = 10000.0
CHUNK = 64
CONV_CH = 512
CONV_WIDTH = 31
X_HEADS = 4
X_HEAD_DIM = D_MODEL // X_HEADS
N_EXPERTS = 32
TOP_K = 4
D_FF = D_MODEL
SWIGLU_LIMIT = 7.0
SWIGLU_ALPHA = 1.702
NORM_EPS = 1e-5

LANES = 128
SUBLANES = 8
ROW_TILES = D_MODEL // LANES
VMEM_LIMIT_BYTES = 56 * 1024 * 1024

HEAD_PAD = LANES
CONV_HALO = 32

TM_MIX = 512
TQ_ATTN = 256
TM_MID = 512
TM_DISP = 512
TB_EXPERT = 256
TM_COMB = 256
WAIT_BATCH = 64


def _rms(x, g):
    return x * lax.rsqrt(jnp.mean(x * x, axis=-1, keepdims=True) + NORM_EPS) * g


def _cparams(sem):
    return pltpu.CompilerParams(dimension_semantics=sem, vmem_limit_bytes=VMEM_LIMIT_BYTES)


def _mix_in_kernel(x_ref, pos_ref, inv_ref, sgn_ref, gmix_ref, win_ref, gq_ref, wuq_ref, wuqs_ref,
                   gkv_ref, wukk_ref, wukv_ref, wdw_ref, bdw_ref, gln_ref, bln_ref,
                   q_ref, k_ref, v_ref, yc_ref, zbuf, *, tiles_per_batch, tm):
    i = pl.program_id(0)
    x = x_ref[...]
    h = _rms(x, gmix_ref[...]).astype(BF16)
    u = jnp.dot(h, win_ref[...], preferred_element_type=F32)
    cq = u[:, 0:256]
    ckv = u[:, 256:384]
    kr = u[:, 384:512]
    krs = u[:, 512:640]
    a = u[:, 640:1152]
    gate = u[:, 1152:1664]

    ang = pos_ref[...].astype(F32) * inv_ref[...]
    cosb = jnp.cos(ang)
    sinb = jnp.sin(ang) * sgn_ref[...]

    cqn = _rms(cq, gq_ref[...]).astype(BF16)
    qm = jnp.dot(cqn, wuq_ref[...], preferred_element_type=F32)
    qs = jnp.dot(cqn, wuqs_ref[...], preferred_element_type=F32)
    for hd in range(MLA_HEADS):
        sl = slice(hd * HEAD_PAD, (hd + 1) * HEAD_PAD)
        q_ref[:, sl] = (qm[:, sl] * cosb + qs[:, sl] * sinb).astype(BF16)

    ckvn = _rms(ckv, gkv_ref[...]).astype(BF16)
    kk = jnp.dot(ckvn, wukk_ref[...], preferred_element_type=F32)
    v_ref[...] = jnp.dot(ckvn, wukv_ref[...], preferred_element_type=F32).astype(BF16)
    krot = kr * cosb + krs * sinb
    for hd in range(MLA_HEADS):
        sl = slice(hd * HEAD_PAD, (hd + 1) * HEAD_PAD)
        k_ref[:, sl] = (kk[:, sl] + krot).astype(BF16)

    z = a * jax.nn.sigmoid(gate)

    @pl.when(i % tiles_per_batch == 0)
    def _():
        zbuf[0:CONV_HALO, :] = jnp.zeros((CONV_HALO, CONV_CH), F32)

    zbuf[CONV_HALO:CONV_HALO + tm, :] = z
    off = CONV_HALO - (CONV_WIDTH - 1)
    rows = 32
    for r0 in range(0, tm, rows):
        acc = jnp.zeros((rows, CONV_CH), F32) + bdw_ref[...]
        for j in range(CONV_WIDTH):
            acc = acc + wdw_ref[j:j + 1, :] * zbuf[r0 + off + j:r0 + off + j + rows, :]
        mu = jnp.mean(acc, axis=-1, keepdims=True)
        cen = acc - mu
        var = jnp.mean(cen * cen, axis=-1, keepdims=True)
        y = cen * lax.rsqrt(var + NORM_EPS) * gln_ref[...] + bln_ref[...]
        yc_ref[r0:r0 + rows, :] = (y * jax.nn.sigmoid(y)).astype(BF16)
    zbuf[0:CONV_HALO, :] = zbuf[tm:tm + CONV_HALO, :]


def _mix_in(x2d, pos2d, inv_l, sgn_l, g_mix, w_in_p, g_q, w_uq_p, w_uq_s, g_kv, w_uk_k, w_uk_v,
            w_dw, b_dw, g_ln, b_ln, *, seq):
    n = x2d.shape[0]
    tm = min(TM_MIX, seq)
    full = lambda a: pl.BlockSpec(a.shape, lambda i: (0,) * a.ndim)
    consts = [inv_l, sgn_l, g_mix, w_in_p, g_q, w_uq_p, w_uq_s, g_kv, w_uk_k, w_uk_v, w_dw, b_dw, g_ln, b_ln]
    return pl.pallas_call(
        functools.partial(_mix_in_kernel, tiles_per_batch=seq // tm, tm=tm),
        grid=(n // tm,),
        in_specs=[pl.BlockSpec((tm, D_MODEL), lambda i: (i, 0)),
                  pl.BlockSpec((tm, 1), lambda i: (i, 0))] + [full(a) for a in consts],
        out_specs=[pl.BlockSpec((tm, MLA_HEADS * HEAD_PAD), lambda i: (i, 0)),
                   pl.BlockSpec((tm, MLA_HEADS * HEAD_PAD), lambda i: (i, 0)),
                   pl.BlockSpec((tm, MLA_WIDTH), lambda i: (i, 0)),
                   pl.BlockSpec((tm, CONV_CH), lambda i: (i, 0))],
        out_shape=[jax.ShapeDtypeStruct((n, MLA_HEADS * HEAD_PAD), BF16),
                   jax.ShapeDtypeStruct((n, MLA_HEADS * HEAD_PAD), BF16),
                   jax.ShapeDtypeStruct((n, MLA_WIDTH), BF16),
                   jax.ShapeDtypeStruct((n, CONV_CH), BF16)],
        scratch_shapes=[pltpu.VMEM((CONV_HALO + tm + SUBLANES, CONV_CH), F32)],
        compiler_params=_cparams(("arbitrary",)),
        name="mix_in",
    )(x2d, pos2d, *consts)


ATTN_GROUP = 4


def _attn_kernel(q_ref, k_ref, v_ref, o_ref, *, tq):
    i = pl.program_id(1)
    c2 = (MLA_NOPE + MLA_ROPE) ** -0.5 * 1.4426950408889634
    nt = (((1,), (1,)), ((), ()))
    row_c = lax.broadcasted_iota(I32, (tq, tq), 0) // CHUNK
    col_c = lax.broadcasted_iota(I32, (tq, tq), 1) // CHUNK
    diag_mask = col_c <= row_c
    lane = lax.broadcasted_iota(I32, (tq, LANES), 1)

    for g in range(MLA_HEADS // ATTN_GROUP):
        heads = list(range(g * ATTN_GROUP, (g + 1) * ATTN_GROUP))
        qs = [q_ref[:, hd * HEAD_PAD:(hd + 1) * HEAD_PAD] for hd in heads]

        def step(j, carry, masked):
            start = pl.multiple_of(j * tq, tq)
            new = []
            for n_, hd in enumerate(heads):
                m, l, acc = carry[n_]
                kj = k_ref[pl.ds(start, tq), hd * HEAD_PAD:(hd + 1) * HEAD_PAD]
                vj = v_ref[pl.ds(start, tq), (hd // 2) * LANES:(hd // 2 + 1) * LANES]
                s = lax.dot_general(qs[n_], kj, nt, preferred_element_type=F32)
                if masked:
                    s = jnp.where(diag_mask, s, -jnp.inf)
                m_new = jnp.maximum(m, jnp.max(s, axis=-1, keepdims=True))
                alpha = jnp.exp2((m - m_new) * c2)
                p = jnp.exp2((s - m_new) * c2)
                l = alpha * l + jnp.sum(p, axis=-1, keepdims=True)
                acc = alpha * acc + jnp.dot(p.astype(BF16), vj, preferred_element_type=F32)
                new.append((m_new, l, acc))
            return tuple(new)

        init = tuple((jnp.full((tq, 1), -jnp.inf, F32), jnp.zeros((tq, 1), F32), jnp.zeros((tq, LANES), F32))
                     for _ in heads)
        carry = lax.fori_loop(0, i, functools.partial(step, masked=False), init)
        fin = step(i, carry, True)
        outs = [acc / l for (_, l, acc) in fin]
        for pr in range(ATTN_GROUP // 2):
            col = (heads[0] // 2 + pr) * LANES
            o_ref[:, col:col + LANES] = jnp.where(lane < MLA_V, outs[2 * pr], outs[2 * pr + 1]).astype(BF16)


def _attn(q, k, v, *, batch, seq):
    tq = min(TQ_ATTN, seq)
    nq = seq // tq
    return pl.pallas_call(
        functools.partial(_attn_kernel, tq=tq),
        grid=(batch, nq),
        in_specs=[pl.BlockSpec((tq, MLA_HEADS * HEAD_PAD), lambda b, i: (b * nq + i, 0)),
                  pl.BlockSpec((seq, MLA_HEADS * HEAD_PAD), lambda b, i: (b, 0)),
                  pl.BlockSpec((seq, MLA_WIDTH), lambda b, i: (b, 0))],
        out_specs=pl.BlockSpec((tq, MLA_WIDTH), lambda b, i: (b * nq + i, 0)),
        out_shape=jax.ShapeDtypeStruct((batch * seq, MLA_WIDTH), BF16),
        compiler_params=_cparams(("arbitrary", "arbitrary")),
        name="mla_attn",
    )(q, k, v)


def _mem_kv_kernel(mem_ref, g_ref, w_ref, kv_ref):
    mn = _rms(mem_ref[...], g_ref[...]).astype(BF16)
    kv_ref[...] = jnp.dot(mn, w_ref[...], preferred_element_type=F32).astype(BF16)


def _mem_kv(mem2d, g_mem, w_xkv, *, batch, mem_len):
    return pl.pallas_call(
        _mem_kv_kernel,
        grid=(batch,),
        in_specs=[pl.BlockSpec((mem_len, D_MODEL), lambda b: (b, 0)),
                  pl.BlockSpec((1, D_MODEL), lambda b: (0, 0)),
                  pl.BlockSpec((D_MODEL, 2 * D_MODEL), lambda b: (0, 0))],
        out_specs=pl.BlockSpec((mem_len, 2 * D_MODEL), lambda b: (b, 0)),
        out_shape=jax.ShapeDtypeStruct((batch * mem_len, 2 * D_MODEL), BF16),
        compiler_params=_cparams(("arbitrary",)),
        name="mem_kv",
    )(mem2d, g_mem, w_xkv)


def _mid_kernel(x_ref, ya_ref, yc_ref, kv_ref, woa_ref, wob_ref, gx_ref, wxq_ref, wxo_ref, gf_ref,
                wrh_ref, wrl_ref, br_ref, tri_ref,
                x2_ref, h3_ref, idx_ref, gate_ref, rank_ref, cnt_ref, run_ref, *, tm):
    i = pl.program_id(0)

    @pl.when(i == 0)
    def _():
        run_ref[...] = jnp.zeros_like(run_ref)

    x1 = (x_ref[...]
          + jnp.dot(ya_ref[...], woa_ref[...], preferred_element_type=F32)
          + jnp.dot(yc_ref[...], wob_ref[...], preferred_element_type=F32))
    h = _rms(x1, gx_ref[...]).astype(BF16)
    q = jnp.dot(h, wxq_ref[...], preferred_element_type=F32).astype(BF16)
    nt = (((1,), (1,)), ((), ()))
    os = []
    for hd in range(X_HEADS):
        sl = slice(hd * X_HEAD_DIM, (hd + 1) * X_HEAD_DIM)
        vsl = slice(D_MODEL + hd * X_HEAD_DIM, D_MODEL + (hd + 1) * X_HEAD_DIM)
        s = lax.dot_general(q[:, sl], kv_ref[:, sl], nt, preferred_element_type=F32) * (X_HEAD_DIM ** -0.5)
        e = jnp.exp(s - jnp.max(s, axis=-1, keepdims=True))
        p = (e / jnp.sum(e, axis=-1, keepdims=True)).astype(BF16)
        os.append(jnp.dot(p, kv_ref[:, vsl], preferred_element_type=F32).astype(BF16))
    o = jnp.concatenate(os, axis=-1)
    x2 = x1 + jnp.dot(o, wxo_ref[...], preferred_element_type=F32)
    x2_ref[...] = x2
    h3 = _rms(x2, gf_ref[...])
    for t in range(ROW_TILES):
        h3_ref[:, t, :] = h3[:, t * LANES:(t + 1) * LANES]

    hh = h3.astype(BF16)
    hl = (h3 - hh.astype(F32)).astype(BF16)
    lg = (jnp.dot(hh, wrh_ref[...], preferred_element_type=F32)
          + (jnp.dot(hh, wrl_ref[...], preferred_element_type=F32)
             + jnp.dot(hl, wrh_ref[...], preferred_element_type=F32)))
    lgt = jnp.transpose(lg)[0:N_EXPERTS, :] + br_ref[...]
    eid = lax.broadcasted_iota(I32, (N_EXPERTS, tm), 0)
    vals, hots = [], []
    cur = lgt
    for k in range(TOP_K):
        mx = jnp.max(cur, axis=0, keepdims=True)
        ik = jnp.min(jnp.where(cur == mx, eid, N_EXPERTS), axis=0, keepdims=True)
        hot = eid == ik
        cur = jnp.where(hot, -jnp.inf, cur)
        idx_ref[k:k + 1, :] = ik
        vals.append(mx)
        hots.append(hot)
    es = [jnp.exp(v - vals[0]) for v in vals]
    den = es[0] + es[1] + es[2] + es[3]
    for k in range(TOP_K):
        gate_ref[k:k + 1, :] = es[k] / den
    cnt = jnp.zeros((N_EXPERTS, tm), F32)
    for k in range(TOP_K):
        cnt = cnt + hots[k].astype(F32)
    run = run_ref[:, 0:1]
    tot = jnp.dot(cnt.astype(BF16), tri_ref[...], preferred_element_type=F32) + run
    for k in range(TOP_K):
        rank_ref[k:k + 1, :] = jnp.sum(jnp.where(hots[k], tot, 0.0), axis=0, keepdims=True).astype(I32)
    new_run = run + jnp.sum(cnt, axis=1, keepdims=True)
    run_ref[...] = jnp.broadcast_to(new_run, run_ref.shape)
    cnt_ref[...] = jnp.broadcast_to(new_run, cnt_ref.shape).astype(I32)


def _mid(x2d, ya, yc, kv, w_oa, w_ob, g_x, w_xq, w_xo, g_f, wr_h, wr_l, b_r, tri, *, seq, mem_len, tm):
    n = x2d.shape[0]
    tpb = seq // tm
    full = lambda a: pl.BlockSpec(a.shape, lambda i: (0,) * a.ndim)
    consts_a = [w_oa, w_ob, g_x, w_xq, w_xo, g_f, wr_h, wr_l, b_r, tri]
    return pl.pallas_call(
        functools.partial(_mid_kernel, tm=tm),
        grid=(n // tm,),
        in_specs=[pl.BlockSpec((tm, D_MODEL), lambda i: (i, 0)),
                  pl.BlockSpec((tm, MLA_WIDTH), lambda i: (i, 0)),
                  pl.BlockSpec((tm, CONV_CH), lambda i: (i, 0)),
                  pl.BlockSpec((mem_len, 2 * D_MODEL), lambda i: (i // tpb, 0))] + [full(a) for a in consts_a],
        out_specs=[pl.BlockSpec((tm, D_MODEL), lambda i: (i, 0)),
                   pl.BlockSpec((tm, ROW_TILES, LANES), lambda i: (i, 0, 0)),
                   pl.BlockSpec((TOP_K, tm), lambda i: (0, i)),
                   pl.BlockSpec((TOP_K, tm), lambda i: (0, i)),
                   pl.BlockSpec((TOP_K, tm), lambda i: (0, i)),
                   pl.BlockSpec((N_EXPERTS, LANES), lambda i: (0, 0))],
        out_shape=[jax.ShapeDtypeStruct((n, D_MODEL), F32),
                   jax.ShapeDtypeStruct((n, ROW_TILES, LANES), F32),
                   jax.ShapeDtypeStruct((TOP_K, n), I32),
                   jax.ShapeDtypeStruct((TOP_K, n), F32),
                   jax.ShapeDtypeStruct((TOP_K, n), I32),
                   jax.ShapeDtypeStruct((N_EXPERTS, LANES), I32)],
        scratch_shapes=[pltpu.VMEM((N_EXPERTS, LANES), F32)],
        compiler_params=_cparams(("arbitrary",)),
        name="mid",
    )(x2d, ya, yc, kv, *consts_a)


def _pos_kernel(idx_ref, rank_ref, pstart_ref, pos_ref, *, tm):
    eid = lax.broadcasted_iota(I32, (N_EXPERTS, tm), 0)
    ps = pstart_ref[:, 0:1]
    for k in range(TOP_K):
        base = jnp.sum(jnp.where(eid == idx_ref[k:k + 1, :], ps, 0), axis=0, keepdims=True)
        pos_ref[k:k + 1, :] = base + rank_ref[k:k + 1, :]


def _pos(idx, rank, pstart_l, *, tm):
    n = idx.shape[1]
    return pl.pallas_call(
        functools.partial(_pos_kernel, tm=tm),
        grid=(n // tm,),
        in_specs=[pl.BlockSpec((TOP_K, tm), lambda i: (0, i)),
                  pl.BlockSpec((TOP_K, tm), lambda i: (0, i)),
                  pl.BlockSpec((N_EXPERTS, LANES), lambda i: (0, 0))],
        out_specs=pl.BlockSpec((TOP_K, tm), lambda i: (0, i)),
        out_shape=jax.ShapeDtypeStruct((TOP_K, n), I32),
        compiler_params=_cparams(("arbitrary",)),
        name="slot_pos",
    )(idx, rank, pstart_l)


def _row_copy(src, src_row, dst, dst_row, sem):
    return pltpu.make_async_copy(src.at[src_row], dst.at[dst_row], sem)


def _dispatch_kernel(pad_start_ref, pad_cnt_ref, pos_ref, h3_ref, xs_hbm, zrow, sem, zsem, *, tm):
    i = pl.program_id(0)

    @pl.when(i == 0)
    def _():
        zrow[...] = jnp.zeros_like(zrow)

        def per_expert(e, c):
            s0 = pad_start_ref[e]

            def start_one(r, c2):
                pltpu.make_async_copy(zrow.at[0], xs_hbm.at[s0 + r], zsem).start()
                return c2

            lax.fori_loop(0, pad_cnt_ref[e], start_one, 0)

            def wait_one(r, c2):
                pltpu.make_async_copy(zrow.at[0], xs_hbm.at[s0 + r], zsem).wait()
                return c2

            lax.fori_loop(0, pad_cnt_ref[e], wait_one, 0)
            return c

        lax.fori_loop(0, N_EXPERTS, per_expert, 0)

    def start_tok(t, c):
        for k in range(TOP_K):
            _row_copy(h3_ref, t, xs_hbm, pos_ref[k, t], sem).start()
        return c

    lax.fori_loop(0, tm, start_tok, 0)

    def wait_batch(w, c):
        for _ in range(WAIT_BATCH):
            _row_copy(h3_ref, 0, xs_hbm, 0, sem).wait()
        return c

    lax.fori_loop(0, tm * TOP_K // WAIT_BATCH, wait_batch, 0)


def _dispatch(pad_start, pad_cnt, pos, h3r, *, n_slots, tm):
    n = h3r.shape[0]
    gs = pltpu.PrefetchScalarGridSpec(
        num_scalar_prefetch=2,
        grid=(n // tm,),
        in_specs=[pl.BlockSpec((TOP_K, tm), lambda i, a, b: (0, i), memory_space=pltpu.SMEM),
                  pl.BlockSpec((tm, ROW_TILES, LANES), lambda i, a, b: (i, 0, 0))],
        out_specs=pl.BlockSpec(memory_space=pl.ANY),
        scratch_shapes=[pltpu.VMEM((1, ROW_TILES, LANES), F32),
                        pltpu.SemaphoreType.DMA(()),
                        pltpu.SemaphoreType.DMA(())],
    )
    return pl.pallas_call(
        functools.partial(_dispatch_kernel, tm=tm),
        grid_spec=gs,
        out_shape=jax.ShapeDtypeStruct((n_slots, ROW_TILES, LANES), F32),
        compiler_params=_cparams(("arbitrary",)),
        name="dispatch",
    )(pad_start, pad_cnt, pos, h3r)


def _expert_kernel(blk_e_ref, nused_ref, x_ref, wu_ref, bu_ref, wd_ref, bd_ref, y_ref, wu_bf, wd_bf, *, tb):
    b = pl.program_id(0)

    @pl.when(b < nused_ref[0])
    def _():
        prev = blk_e_ref[jnp.maximum(b - 1, 0)]

        @pl.when(jnp.logical_or(b == 0, blk_e_ref[b] != prev))
        def _():
            wu_bf[...] = wu_ref[...].astype(BF16)
            wd_bf[...] = wd_ref[...].astype(BF16)

        x = jnp.concatenate([x_ref[:, t, :] for t in range(ROW_TILES)], axis=-1).astype(BF16)
        up = jnp.dot(x, wu_bf[...], preferred_element_type=F32) + bu_ref[...]
        glu = jnp.minimum(up[:, :D_FF], SWIGLU_LIMIT)
        lin = jnp.clip(up[:, D_FF:], -SWIGLU_LIMIT, SWIGLU_LIMIT)
        act = (glu * jax.nn.sigmoid(SWIGLU_ALPHA * glu) * (lin + 1.0)).astype(BF16)
        y = jnp.dot(act, wd_bf[...], preferred_element_type=F32) + bd_ref[...]
        for t in range(ROW_TILES):
            y_ref[:, t, :] = y[:, t * LANES:(t + 1) * LANES]


def _expert(blk_e, nused, xs, w_up, b_up, w_down, b_down, *, tb):
    n_slots = xs.shape[0]
    nb = n_slots // tb
    blk = lambda b, be, nu: jnp.minimum(b, nu[0] - 1)
    gs = pltpu.PrefetchScalarGridSpec(
        num_scalar_prefetch=2,
        grid=(nb,),
        in_specs=[pl.BlockSpec((tb, ROW_TILES, LANES), lambda b, be, nu: (blk(b, be, nu), 0, 0)),
                  pl.BlockSpec((None, D_MODEL, 2 * D_FF), lambda b, be, nu: (be[blk(b, be, nu)], 0, 0)),
                  pl.BlockSpec((None, 1, 2 * D_FF), lambda b, be, nu: (be[blk(b, be, nu)], 0, 0)),
                  pl.BlockSpec((None, D_FF, D_MODEL), lambda b, be, nu: (be[blk(b, be, nu)], 0, 0)),
                  pl.BlockSpec((None, 1, D_MODEL), lambda b, be, nu: (be[blk(b, be, nu)], 0, 0))],
        out_specs=pl.BlockSpec((tb, ROW_TILES, LANES), lambda b, be, nu: (blk(b, be, nu), 0, 0)),
        scratch_shapes=[pltpu.VMEM((D_MODEL, 2 * D_FF), BF16),
                        pltpu.VMEM((D_FF, D_MODEL), BF16)],
    )
    return pl.pallas_call(
        functools.partial(_expert_kernel, tb=tb),
        grid_spec=gs,
        out_shape=jax.ShapeDtypeStruct((n_slots, ROW_TILES, LANES), F32),
        compiler_params=_cparams(("arbitrary",)),
        name="expert_ffn",
    )(blk_e, nused, xs, w_up, b_up, w_down, b_down)


def _combine_kernel(pos_ref, x2_ref, gate_ref, gfin_ref, y_hbm, out_ref, buf, sem, *, tm):
    def start_tok(t, c):
        for k in range(TOP_K):
            pltpu.make_async_copy(y_hbm.at[pos_ref[k, t]], buf.at[k, t], sem).start()
        return c

    lax.fori_loop(0, tm, start_tok, 0)

    def wait_batch(w, c):
        for _ in range(WAIT_BATCH):
            pltpu.make_async_copy(y_hbm.at[0], buf.at[0, 0], sem).wait()
        return c

    lax.fori_loop(0, tm * TOP_K // WAIT_BATCH, wait_batch, 0)

    g = gate_ref[...]
    pieces = []
    for t in range(ROW_TILES):
        acc = x2_ref[:, t * LANES:(t + 1) * LANES]
        for k in range(TOP_K):
            acc = acc + g[:, k:k + 1] * buf[k, :, t, :]
        pieces.append(acc)
    x3 = jnp.concatenate(pieces, axis=-1)
    out_ref[...] = _rms(x3, gfin_ref[...])


def _combine(pos, x2, gate_nt, g_final, y, *, tm):
    n = x2.shape[0]
    return pl.pallas_call(
        functools.partial(_combine_kernel, tm=tm),
        grid=(n // tm,),
        in_specs=[pl.BlockSpec((TOP_K, tm), lambda i: (0, i), memory_space=pltpu.SMEM),
                  pl.BlockSpec((tm, D_MODEL), lambda i: (i, 0)),
                  pl.BlockSpec((tm, TOP_K), lambda i: (i, 0)),
                  pl.BlockSpec((1, D_MODEL), lambda i: (0, 0)),
                  pl.BlockSpec(memory_space=pl.ANY)],
        out_specs=pl.BlockSpec((tm, D_MODEL), lambda i: (i, 0)),
        out_shape=jax.ShapeDtypeStruct((n, D_MODEL), F32),
        scratch_shapes=[pltpu.VMEM((TOP_K, tm, ROW_TILES, LANES), F32),
                        pltpu.SemaphoreType.DMA(())],
        compiler_params=_cparams(("arbitrary",)),
        name="combine",
    )(pos, x2, gate_nt, g_final, y)


def _head_pad_cols(w, per_head_in, take, place):
    kdim = w.shape[0]
    w3 = w.reshape(kdim, MLA_HEADS, per_head_in)
    out = jnp.zeros((kdim, MLA_HEADS, HEAD_PAD), w.dtype)
    for (t0, t1), p0 in zip(take, place):
        out = out.at[:, :, p0:p0 + (t1 - t0)].set(w3[:, :, t0:t1])
    return out.reshape(kdim, MLA_HEADS * HEAD_PAD)


def kernel(x, mem, positions, g_mix, w_in, g_q, w_uq, g_kv, w_ukv, w_dw, b_dw, g_conv_ln, b_conv_ln, w_out,
           g_xattn, g_mem, w_xq, w_xkv, w_xo, g_ffn, w_router, b_router, w_up, b_up, w_down, b_down, g_final):
    batch, seq, _ = x.shape
    mem_len = mem.shape[1]
    n = batch * seq
    half = MLA_ROPE // 2
    r0 = MLA_NOPE
    assert w_in.shape[0] == 1, "one trunk layer"

    row = lambda v: v.reshape(1, -1).astype(F32)

    wi = w_in[0]
    o1, o2, o3 = Q_LORA, Q_LORA + KV_LORA, Q_LORA + KV_LORA + MLA_ROPE
    kr_blk = jnp.zeros((D_MODEL, HEAD_PAD), F32).at[:, r0:r0 + MLA_ROPE].set(wi[:, o2:o3])
    kr_swp = (jnp.zeros((D_MODEL, HEAD_PAD), F32)
              .at[:, r0:r0 + half].set(wi[:, o2 + half:o3])
              .at[:, r0 + half:r0 + MLA_ROPE].set(wi[:, o2:o2 + half]))
    w_in_p = jnp.concatenate([wi[:, :o2], kr_blk, kr_swp, wi[:, o3:]], axis=1).astype(BF16)
    per_q = MLA_NOPE + MLA_ROPE
    w_uq_p = _head_pad_cols(w_uq[0], per_q, [(0, per_q)], [0]).astype(BF16)
    w_uq_s = _head_pad_cols(w_uq[0], per_q, [(MLA_NOPE + half, per_q), (MLA_NOPE, MLA_NOPE + half)],
                            [r0, r0 + half]).astype(BF16)
    per_kv = MLA_NOPE + MLA_V
    w_uk_k = _head_pad_cols(w_ukv[0], per_kv, [(0, MLA_NOPE)], [0]).astype(BF16)
    w_uk_v = (w_ukv[0].reshape(KV_LORA, MLA_HEADS, per_kv)[:, :, MLA_NOPE:]
              .reshape(KV_LORA, MLA_WIDTH).astype(BF16))
    inv = ROPE_THETA ** (-jnp.arange(0, MLA_ROPE, 2, dtype=F32) / MLA_ROPE)
    inv_l = (jnp.zeros((1, LANES), F32).at[0, r0:r0 + half].set(inv).at[0, r0 + half:r0 + MLA_ROPE].set(inv))
    sgn_l = (jnp.zeros((1, LANES), F32).at[0, r0:r0 + half].set(-1.0).at[0, r0 + half:r0 + MLA_ROPE].set(1.0))

    x2d = x.reshape(n, D_MODEL)
    pos2d = positions.reshape(n, 1).astype(I32)

    q, k, v, y_conv = _mix_in(x2d, pos2d, inv_l, sgn_l, row(g_mix[0]), w_in_p, row(g_q[0]), w_uq_p, w_uq_s,
                              row(g_kv[0]), w_uk_k, w_uk_v, w_dw[0].astype(F32), row(b_dw[0]),
                              row(g_conv_ln[0]), row(b_conv_ln[0]), seq=seq)
    y_mla = _attn(q, k, v, batch=batch, seq=seq)
    kv = _mem_kv(mem.reshape(batch * mem_len, D_MODEL), row(g_mem[0]), w_xkv[0].astype(BF16),
                 batch=batch, mem_len=mem_len)

    tm_mid = min(TM_MID, seq)
    wr = jnp.zeros((D_MODEL, LANES), F32).at[:, :N_EXPERTS].set(w_router[0])
    wr_h = wr.astype(BF16)
    wr_l = (wr - wr_h.astype(F32)).astype(BF16)
    b_r = jnp.broadcast_to(b_router[0].reshape(N_EXPERTS, 1), (N_EXPERTS, LANES)).astype(F32)[:, 0:1]
    tri = (lax.broadcasted_iota(I32, (tm_mid, tm_mid), 0)
           < lax.broadcasted_iota(I32, (tm_mid, tm_mid), 1)).astype(BF16)
    wo = w_out[0].astype(BF16)
    x2, h3r, idx, gate, rank, cnt = _mid(
        x2d, y_mla, y_conv, kv, wo[:MLA_WIDTH], wo[MLA_WIDTH:], row(g_xattn[0]), w_xq[0].astype(BF16),
        w_xo[0].astype(BF16), row(g_ffn[0]), wr_h, wr_l, b_r, tri, seq=seq, mem_len=mem_len, tm=tm_mid)

    tb = TB_EXPERT
    counts = cnt[:, 0]
    padded = (counts + tb - 1) // tb * tb
    pend = jnp.cumsum(padded)
    pstart = pend - padded
    n_slots = n * TOP_K + N_EXPERTS * tb
    nb = n_slots // tb
    blk_first = jnp.arange(nb, dtype=I32) * tb
    blk_e = jnp.minimum(jnp.sum((pend[None, :] <= blk_first[:, None]).astype(I32), axis=1),
                        N_EXPERTS - 1).astype(I32)
    nused = (pend[-1:] // tb).astype(I32)
    pstart_l = jnp.broadcast_to(pstart.reshape(N_EXPERTS, 1), (N_EXPERTS, LANES)).astype(I32)

    pos = _pos(idx, rank, pstart_l, tm=tm_mid)
    xs = _dispatch((pstart + counts).astype(I32), (padded - counts).astype(I32), pos, h3r,
                   n_slots=n_slots, tm=min(TM_DISP, n))
    y = _expert(blk_e, nused, xs, w_up[0], b_up[0].reshape(N_EXPERTS, 1, 2 * D_FF),
                w_down[0], b_down[0].reshape(N_EXPERTS, 1, D_MODEL), tb=tb)
    out = _combine(pos, x2, jnp.transpose(gate), row(g_final), y, tm=min(TM_COMB, n))
    return out.reshape(batch, seq, D_MODEL)
```

```python
import functools

import jax
import jax.numpy as jnp
from jax import lax
from jax.experimental import pallas as pl
from jax.experimental.pallas import tpu as pltpu

F32 = jnp.float32
BF16 = jnp.bfloat16
I32 = jnp.int32

D_MODEL = 1024
MLA_HEADS = 8
MLA_NOPE = 64
MLA_ROPE = 32
MLA_V = 64
MLA_WIDTH = MLA_HEADS * MLA_V
Q_LORA = 256
KV_LORA = 128
ROPE_THETA = 10000.0
CHUNK = 64
CONV_CH = 512
CONV_WIDTH = 31
X_HEADS = 4
X_HEAD_DIM = D_MODEL // X_HEADS
N_EXPERTS = 32
TOP_K = 4
D_FF = D_MODEL
SWIGLU_LIMIT = 7.0
SWIGLU_ALPHA = 1.702
NORM_EPS = 1e-5

LANES = 128
SUBLANES = 8
ROW_TILES = D_MODEL // LANES
PACK_TILES = ROW_TILES // 2
HI_MASK = 0xFFFF0000
VMEM_LIMIT_BYTES = 56 * 1024 * 1024

HEAD_PAD = LANES
CONV_HALO = 32

TM_MIX = 512
TQ_ATTN = 512
TM_MID = 512
TM_DISP = 512
TB_EXPERT = 512
TM_COMB = 256
WAIT_BATCH = 64


def _rms(x, g):
    return x * lax.rsqrt(jnp.mean(x * x, axis=-1, keepdims=True) + NORM_EPS) * g


def _cparams(sem):
    return pltpu.CompilerParams(dimension_semantics=sem, vmem_limit_bytes=VMEM_LIMIT_BYTES)


def _mix_in_kernel(x_ref, pos_ref, inv_ref, sgn_ref, gmix_ref, win_ref, gq_ref, wuq_ref, wuqs_ref,
                   gkv_ref, wukk_ref, wukv_ref, wdw_ref, bdw_ref, gln_ref, bln_ref,
                   q_ref, k_ref, v_ref, yc_ref, zbuf, *, tiles_per_batch, tm):
    i = pl.program_id(0)
    x = x_ref[...]
    h = _rms(x, gmix_ref[...]).astype(BF16)
    u = jnp.dot(h, win_ref[...], preferred_element_type=F32)
    cq = u[:, 0:256]
    ckv = u[:, 256:384]
    kr = u[:, 384:512]
    krs = u[:, 512:640]
    a = u[:, 640:1152]
    gate = u[:, 1152:1664]

    ang_t = inv_ref[...] * pos_ref[...].astype(F32)
    cos_t = jnp.cos(ang_t)
    sin_t = jnp.sin(ang_t) * sgn_ref[...]
    cosb = jnp.transpose(cos_t)
    sinb = jnp.transpose(sin_t)

    nt = (((1,), (1,)), ((), ()))
    cqn = _rms(cq, gq_ref[...]).astype(BF16)
    qm = lax.dot_general(wuq_ref[...], cqn, nt, preferred_element_type=F32)
    qs = lax.dot_general(wuqs_ref[...], cqn, nt, preferred_element_type=F32)
    for hd in range(MLA_HEADS):
        sl = slice(hd * HEAD_PAD, (hd + 1) * HEAD_PAD)
        q_ref[sl, :] = (qm[sl, :] * cos_t + qs[sl, :] * sin_t).astype(BF16)

    ckvn = _rms(ckv, gkv_ref[...]).astype(BF16)
    kk = jnp.dot(ckvn, wukk_ref[...], preferred_element_type=F32)
    v_ref[...] = lax.dot_general(wukv_ref[...], ckvn, nt, preferred_element_type=F32).astype(BF16)
    krot = kr * cosb + krs * sinb
    for hd in range(MLA_HEADS):
        sl = slice(hd * HEAD_PAD, (hd + 1) * HEAD_PAD)
        k_ref[:, sl] = (kk[:, sl] + krot).astype(BF16)

    z = a * jax.nn.sigmoid(gate)

    @pl.when(i % tiles_per_batch == 0)
    def _():
        zbuf[0:CONV_HALO, :] = jnp.zeros((CONV_HALO, CONV_CH), F32)

    zbuf[CONV_HALO:CONV_HALO + tm, :] = z
    off = CONV_HALO - (CONV_WIDTH - 1)
    rows = 32
    for r0 in range(0, tm, rows):
        acc = jnp.zeros((rows, CONV_CH), F32) + bdw_ref[...]
        for j in range(CONV_WIDTH):
            acc = acc + wdw_ref[j:j + 1, :] * zbuf[r0 + off + j:r0 + off + j + rows, :]
        mu = jnp.mean(acc, axis=-1, keepdims=True)
        cen = acc - mu
        var = jnp.mean(cen * cen, axis=-1, keepdims=True)
        y = cen * lax.rsqrt(var + NORM_EPS) * gln_ref[...] + bln_ref[...]
        yc_ref[r0:r0 + rows, :] = (y * jax.nn.sigmoid(y)).astype(BF16)
    zbuf[0:CONV_HALO, :] = zbuf[tm:tm + CONV_HALO, :]


def _mix_in(x2d, pos_row, inv_c, sgn_c, g_mix, w_in_p, g_q, w_uq_t, w_uq_st, g_kv, w_uk_k, w_uk_vt,
            w_dw, b_dw, g_ln, b_ln, *, seq):
    n = x2d.shape[0]
    tm = min(TM_MIX, seq)
    full = lambda a: pl.BlockSpec(a.shape, lambda i: (0,) * a.ndim)
    consts = [inv_c, sgn_c, g_mix, w_in_p, g_q, w_uq_t, w_uq_st, g_kv, w_uk_k, w_uk_vt, w_dw, b_dw, g_ln, b_ln]
    return pl.pallas_call(
        functools.partial(_mix_in_kernel, tiles_per_batch=seq // tm, tm=tm),
        grid=(n // tm,),
        in_specs=[pl.BlockSpec((tm, D_MODEL), lambda i: (i, 0)),
                  pl.BlockSpec((1, tm), lambda i: (0, i))] + [full(a) for a in consts],
        out_specs=[pl.BlockSpec((MLA_HEADS * HEAD_PAD, tm), lambda i: (0, i)),
                   pl.BlockSpec((tm, MLA_HEADS * HEAD_PAD), lambda i: (i, 0)),
                   pl.BlockSpec((MLA_WIDTH, tm), lambda i: (0, i)),
                   pl.BlockSpec((tm, CONV_CH), lambda i: (i, 0))],
        out_shape=[jax.ShapeDtypeStruct((MLA_HEADS * HEAD_PAD, n), BF16),
                   jax.ShapeDtypeStruct((n, MLA_HEADS * HEAD_PAD), BF16),
                   jax.ShapeDtypeStruct((MLA_WIDTH, n), BF16),
                   jax.ShapeDtypeStruct((n, CONV_CH), BF16)],
        scratch_shapes=[pltpu.VMEM((CONV_HALO + tm + SUBLANES, CONV_CH), F32)],
        compiler_params=_cparams(("arbitrary",)),
        name="mix_in",
    )(x2d, pos_row, *consts)


ATTN_GROUP = 4


TK_ATTN = 512


def _attn_kernel(q_ref, k_ref, v_ref, o_ref, *, tq):
    i = pl.program_id(1)
    tk = min(TK_ATTN, tq)
    per_q = tq // tk
    c2 = (MLA_NOPE + MLA_ROPE) ** -0.5 * 1.4426950408889634
    key_c = lax.broadcasted_iota(I32, (tk, tq), 0) // CHUNK
    qry_c = lax.broadcasted_iota(I32, (tk, tq), 1) // CHUNK

    for g in range(MLA_HEADS // ATTN_GROUP):
        heads = list(range(g * ATTN_GROUP, (g + 1) * ATTN_GROUP))
        qts = [q_ref[hd * HEAD_PAD:(hd + 1) * HEAD_PAD, :] for hd in heads]

        def step(j, carry, diag):
            start = pl.multiple_of(j * tk, tk)
            new = []
            for n_, hd in enumerate(heads):
                m, l, acc = carry[n_]
                kj = k_ref[pl.ds(start, tk), hd * HEAD_PAD:(hd + 1) * HEAD_PAD]
                vj = v_ref[hd * MLA_V:(hd + 1) * MLA_V, pl.ds(start, tk)]
                s = jnp.dot(kj, qts[n_], preferred_element_type=F32)
                if diag is not None:
                    s = jnp.where(key_c + diag * (tk // CHUNK) <= qry_c, s, -jnp.inf)
                m_new = jnp.maximum(m, jnp.max(s, axis=0, keepdims=True))
                alpha = jnp.exp2((m - m_new) * c2)
                p = jnp.exp2((s - m_new) * c2)
                l = alpha * l + jnp.sum(p, axis=0, keepdims=True)
                acc = alpha * acc + jnp.dot(vj, p.astype(BF16), preferred_element_type=F32)
                new.append((m_new, l, acc))
            return tuple(new)

        init = tuple((jnp.full((1, tq), -jnp.inf, F32), jnp.zeros((1, tq), F32), jnp.zeros((MLA_V, tq), F32))
                     for _ in heads)
        carry = lax.fori_loop(0, i * per_q, functools.partial(step, diag=None), init)
        for d in range(per_q):
            carry = step(i * per_q + d, carry, d)
        outs = [acc / l for (_, l, acc) in carry]
        for pr in range(ATTN_GROUP // 2):
            col = (heads[0] // 2 + pr) * LANES
            pair = jnp.concatenate([outs[2 * pr], outs[2 * pr + 1]], axis=0)
            o_ref[:, col:col + LANES] = jnp.transpose(pair).astype(BF16)


def _attn(q_t, k, v_t, *, batch, seq):
    tq = min(TQ_ATTN, seq)
    nq = seq // tq
    return pl.pallas_call(
        functools.partial(_attn_kernel, tq=tq),
        grid=(batch, nq),
        in_specs=[pl.BlockSpec((MLA_HEADS * HEAD_PAD, tq), lambda b, i: (0, b * nq + i)),
                  pl.BlockSpec((seq, MLA_HEADS * HEAD_PAD), lambda b, i: (b, 0)),
                  pl.BlockSpec((MLA_WIDTH, seq), lambda b, i: (0, b))],
        out_specs=pl.BlockSpec((tq, MLA_WIDTH), lambda b, i: (b * nq + i, 0)),
        out_shape=jax.ShapeDtypeStruct((batch * seq, MLA_WIDTH), BF16),
        compiler_params=_cparams(("arbitrary", "arbitrary")),
        name="mla_attn",
    )(q_t, k, v_t)


def _mem_kv_kernel(mem_ref, g_ref, w_ref, kv_ref):
    mn = _rms(mem_ref[...], g_ref[...]).astype(BF16)
    kv_ref[...] = jnp.dot(mn, w_ref[...], preferred_element_type=F32).astype(BF16)


def _mem_kv(mem2d, g_mem, w_xkv, *, batch, mem_len):
    return pl.pallas_call(
        _mem_kv_kernel,
        grid=(batch,),
        in_specs=[pl.BlockSpec((mem_len, D_MODEL), lambda b: (b, 0)),
                  pl.BlockSpec((1, D_MODEL), lambda b: (0, 0)),
                  pl.BlockSpec((D_MODEL, 2 * D_MODEL), lambda b: (0, 0))],
        out_specs=pl.BlockSpec((mem_len, 2 * D_MODEL), lambda b: (b, 0)),
        out_shape=jax.ShapeDtypeStruct((batch * mem_len, 2 * D_MODEL), BF16),
        compiler_params=_cparams(("arbitrary",)),
        name="mem_kv",
    )(mem2d, g_mem, w_xkv)


def _mid_kernel(x_ref, ya_ref, yc_ref, kv_ref, woa_ref, wob_ref, gx_ref, wxq_ref, wxo_ref, gf_ref,
                wrh_ref, wrl_ref, br_ref, tri_ref,
                x2_ref, h3_ref, idx_ref, gate_ref, rank_ref, cnt_ref, run_ref, *, tm):
    i = pl.program_id(0)

    @pl.when(i == 0)
    def _():
        run_ref[...] = jnp.zeros_like(run_ref)

    x1 = (x_ref[...]
          + jnp.dot(ya_ref[...], woa_ref[...], preferred_element_type=F32)
          + jnp.dot(yc_ref[...], wob_ref[...], preferred_element_type=F32))
    h = _rms(x1, gx_ref[...]).astype(BF16)
    q = jnp.dot(h, wxq_ref[...], preferred_element_type=F32).astype(BF16)
    nt = (((1,), (1,)), ((), ()))
    os = []
    for hd in range(X_HEADS):
        sl = slice(hd * X_HEAD_DIM, (hd + 1) * X_HEAD_DIM)
        vsl = slice(D_MODEL + hd * X_HEAD_DIM, D_MODEL + (hd + 1) * X_HEAD_DIM)
        s = lax.dot_general(q[:, sl], kv_ref[:, sl], nt, preferred_element_type=F32) * (X_HEAD_DIM ** -0.5)
        e = jnp.exp(s - jnp.max(s, axis=-1, keepdims=True))
        p = (e / jnp.sum(e, axis=-1, keepdims=True)).astype(BF16)
        os.append(jnp.dot(p, kv_ref[:, vsl], preferred_element_type=F32).astype(BF16))
    o = jnp.concatenate(os, axis=-1)
    x2 = x1 + jnp.dot(o, wxo_ref[...], preferred_element_type=F32)
    x2_ref[...] = x2
    h3 = _rms(x2, gf_ref[...])
    hh = h3.astype(BF16)
    hf = hh.astype(F32)
    half_d = D_MODEL // 2
    word = (lax.shift_right_logical(pltpu.bitcast(hf[:, :half_d], jnp.uint32), jnp.uint32(16))
            | (pltpu.bitcast(hf[:, half_d:], jnp.uint32) & jnp.uint32(HI_MASK)))
    for t in range(PACK_TILES):
        h3_ref[pl.ds(t, tm, stride=PACK_TILES), :] = word[:, t * LANES:(t + 1) * LANES]

    hl = (h3 - hf).astype(BF16)
    lg = (jnp.dot(hh, wrh_ref[...], preferred_element_type=F32)
          + (jnp.dot(hh, wrl_ref[...], preferred_element_type=F32)
             + jnp.dot(hl, wrh_ref[...], preferred_element_type=F32)))
    lgt = jnp.transpose(lg)[0:N_EXPERTS, :] + br_ref[...]
    eid = lax.broadcasted_iota(I32, (N_EXPERTS, tm), 0)
    vals, hots = [], []
    cur = lgt
    for k in range(TOP_K):
        mx = jnp.max(cur, axis=0, keepdims=True)
        ik = jnp.min(jnp.where(cur == mx, eid, N_EXPERTS), axis=0, keepdims=True)
        hot = eid == ik
        cur = jnp.where(hot, -jnp.inf, cur)
        idx_ref[k:k + 1, :] = ik
        vals.append(mx)
        hots.append(hot)
    es = [jnp.exp(v - vals[0]) for v in vals]
    den = es[0] + es[1] + es[2] + es[3]
    for k in range(TOP_K):
        gate_ref[k:k + 1, :] = es[k] / den
    cnt = jnp.zeros((N_EXPERTS, tm), F32)
    for k in range(TOP_K):
        cnt = cnt + hots[k].astype(F32)
    run = run_ref[:, 0:1]
    tot = jnp.dot(cnt.astype(BF16), tri_ref[...], preferred_element_type=F32) + run
    for k in range(TOP_K):
        rank_ref[k:k + 1, :] = jnp.sum(jnp.where(hots[k], tot, 0.0), axis=0, keepdims=True).astype(I32)
    new_run = run + jnp.sum(cnt, axis=1, keepdims=True)
    run_ref[...] = jnp.broadcast_to(new_run, run_ref.shape)
    cnt_ref[...] = jnp.broadcast_to(new_run, cnt_ref.shape).astype(I32)


def _mid(x2d, ya, yc, kv, w_oa, w_ob, g_x, w_xq, w_xo, g_f, wr_h, wr_l, b_r, tri, *, seq, mem_len, tm):
    n = x2d.shape[0]
    tpb = seq // tm
    full = lambda a: pl.BlockSpec(a.shape, lambda i: (0,) * a.ndim)
    consts_a = [w_oa, w_ob, g_x, w_xq, w_xo, g_f, wr_h, wr_l, b_r, tri]
    return pl.pallas_call(
        functools.partial(_mid_kernel, tm=tm),
        grid=(n // tm,),
        in_specs=[pl.BlockSpec((tm, D_MODEL), lambda i: (i, 0)),
                  pl.BlockSpec((tm, MLA_WIDTH), lambda i: (i, 0)),
                  pl.BlockSpec((tm, CONV_CH), lambda i: (i, 0)),
                  pl.BlockSpec((mem_len, 2 * D_MODEL), lambda i: (i // tpb, 0))] + [full(a) for a in consts_a],
        out_specs=[pl.BlockSpec((tm, D_MODEL), lambda i: (i, 0)),
                   pl.BlockSpec((tm * PACK_TILES, LANES), lambda i: (i, 0)),
                   pl.BlockSpec((TOP_K, tm), lambda i: (0, i)),
                   pl.BlockSpec((TOP_K, tm), lambda i: (0, i)),
                   pl.BlockSpec((TOP_K, tm), lambda i: (0, i)),
                   pl.BlockSpec((N_EXPERTS, LANES), lambda i: (0, 0))],
        out_shape=[jax.ShapeDtypeStruct((n, D_MODEL), F32),
                   jax.ShapeDtypeStruct((n * PACK_TILES, LANES), jnp.uint32),
                   jax.ShapeDtypeStruct((TOP_K, n), I32),
                   jax.ShapeDtypeStruct((TOP_K, n), F32),
                   jax.ShapeDtypeStruct((TOP_K, n), I32),
                   jax.ShapeDtypeStruct((N_EXPERTS, LANES), I32)],
        scratch_shapes=[pltpu.VMEM((N_EXPERTS, LANES), F32)],
        compiler_params=_cparams(("arbitrary",)),
        name="mid",
    )(x2d, ya, yc, kv, *consts_a)


def _pos_kernel(idx_ref, rank_ref, pstart_ref, pos_ref, *, tm):
    eid = lax.broadcasted_iota(I32, (N_EXPERTS, tm), 0)
    ps = pstart_ref[:, 0:1]
    for k in range(TOP_K):
        base = jnp.sum(jnp.where(eid == idx_ref[k:k + 1, :], ps, 0), axis=0, keepdims=True)
        pos_ref[k:k + 1, :] = base + rank_ref[k:k + 1, :]


def _pos(idx, rank, pstart_l, *, tm):
    n = idx.shape[1]
    return pl.pallas_call(
        functools.partial(_pos_kernel, tm=tm),
        grid=(n // tm,),
        in_specs=[pl.BlockSpec((TOP_K, tm), lambda i: (0, i)),
                  pl.BlockSpec((TOP_K, tm), lambda i: (0, i)),
                  pl.BlockSpec((N_EXPERTS, LANES), lambda i: (0, 0))],
        out_specs=pl.BlockSpec((TOP_K, tm), lambda i: (0, i)),
        out_shape=jax.ShapeDtypeStruct((TOP_K, n), I32),
        compiler_params=_cparams(("arbitrary",)),
        name="slot_pos",
    )(idx, rank, pstart_l)


def _row_copy(src, src_row, dst, dst_row, sem):
    return pltpu.make_async_copy(src.at[src_row], dst.at[dst_row], sem)


def _dispatch_kernel(pad_start_ref, pad_cnt_ref, pos_ref, h3_ref, xs_hbm, zrow, sem, zsem, *, tm):
    i = pl.program_id(0)

    @pl.when(i == 0)
    def _():
        zrow[...] = jnp.zeros_like(zrow)

        def per_expert(e, c):
            s0 = pad_start_ref[e]

            def start_one(r, c2):
                pltpu.make_async_copy(zrow.at[0], xs_hbm.at[s0 + r], zsem).start()
                return c2

            lax.fori_loop(0, pad_cnt_ref[e], start_one, 0)

            def wait_one(r, c2):
                pltpu.make_async_copy(zrow.at[0], xs_hbm.at[s0 + r], zsem).wait()
                return c2

            lax.fori_loop(0, pad_cnt_ref[e], wait_one, 0)
            return c

        lax.fori_loop(0, N_EXPERTS, per_expert, 0)

    def start_tok(t, c):
        for k in range(TOP_K):
            _row_copy(h3_ref, t, xs_hbm, pos_ref[k, t], sem).start(priority=k % 2)
        return c

    lax.fori_loop(0, tm, start_tok, 0)

    def wait_batch(w, c):
        for _ in range(WAIT_BATCH):
            _row_copy(h3_ref, 0, xs_hbm, 0, sem).wait()
        return c

    lax.fori_loop(0, tm * TOP_K // WAIT_BATCH, wait_batch, 0)


def _dispatch(pad_start, pad_cnt, pos, h3r, *, n_slots, tm):
    n = h3r.shape[0]
    gs = pltpu.PrefetchScalarGridSpec(
        num_scalar_prefetch=2,
        grid=(n // tm,),
        in_specs=[pl.BlockSpec((TOP_K, tm), lambda i, a, b: (0, i), memory_space=pltpu.SMEM),
                  pl.BlockSpec((tm, PACK_TILES, LANES), lambda i, a, b: (i, 0, 0))],
        out_specs=pl.BlockSpec(memory_space=pl.ANY),
        scratch_shapes=[pltpu.VMEM((1, PACK_TILES, LANES), jnp.uint32),
                        pltpu.SemaphoreType.DMA(()),
                        pltpu.SemaphoreType.DMA(())],
    )
    return pl.pallas_call(
        functools.partial(_dispatch_kernel, tm=tm),
        grid_spec=gs,
        out_shape=jax.ShapeDtypeStruct((n_slots, PACK_TILES, LANES), jnp.uint32),
        compiler_params=_cparams(("arbitrary",)),
        name="dispatch",
    )(pad_start, pad_cnt, pos, h3r)


def _expert_kernel(blk_e_ref, nused_ref, x_ref, wu_ref, bu_ref, wd_ref, bd_ref, y_ref, wu_bf, wd_bf, *, tb):
    b = pl.program_id(0)

    @pl.when(b < nused_ref[0])
    def _():
        prev = blk_e_ref[jnp.maximum(b - 1, 0)]

        @pl.when(jnp.logical_or(b == 0, blk_e_ref[b] != prev))
        def _():
            wu_bf[...] = wu_ref[...].astype(BF16)
            wd_bf[...] = wd_ref[...].astype(BF16)

        words = [x_ref[pl.ds(t, tb, stride=PACK_TILES), :] for t in range(PACK_TILES)]
        lo = [pltpu.bitcast(lax.shift_left(w, jnp.uint32(16)), F32) for w in words]
        hi = [pltpu.bitcast(w & jnp.uint32(HI_MASK), F32) for w in words]
        x = jnp.concatenate(lo + hi, axis=-1).astype(BF16)
        up = jnp.dot(x, wu_bf[...], preferred_element_type=F32) + bu_ref[...]
        glu = jnp.minimum(up[:, :D_FF], SWIGLU_LIMIT)
        lin = jnp.clip(up[:, D_FF:], -SWIGLU_LIMIT, SWIGLU_LIMIT)
        act = (glu * jax.nn.sigmoid(SWIGLU_ALPHA * glu) * (lin + 1.0)).astype(BF16)
        y = jnp.dot(act, wd_bf[...], preferred_element_type=F32) + bd_ref[...]
        for t in range(ROW_TILES):
            y_ref[pl.ds(t, tb, stride=ROW_TILES), :] = y[:, t * LANES:(t + 1) * LANES]


def _expert(blk_e, nused, xs, w_up, b_up, w_down, b_down, *, tb):
    n_slots = xs.shape[0] // PACK_TILES
    nb = n_slots // tb
    blk = lambda b, be, nu: jnp.minimum(b, nu[0] - 1)
    gs = pltpu.PrefetchScalarGridSpec(
        num_scalar_prefetch=2,
        grid=(nb,),
        in_specs=[pl.BlockSpec((tb * PACK_TILES, LANES), lambda b, be, nu: (blk(b, be, nu), 0)),
                  pl.BlockSpec((None, D_MODEL, 2 * D_FF), lambda b, be, nu: (be[blk(b, be, nu)], 0, 0)),
                  pl.BlockSpec((None, 1, 2 * D_FF), lambda b, be, nu: (be[blk(b, be, nu)], 0, 0)),
                  pl.BlockSpec((None, D_FF, D_MODEL), lambda b, be, nu: (be[blk(b, be, nu)], 0, 0)),
                  pl.BlockSpec((None, 1, D_MODEL), lambda b, be, nu: (be[blk(b, be, nu)], 0, 0))],
        out_specs=pl.BlockSpec((tb * ROW_TILES, LANES), lambda b, be, nu: (blk(b, be, nu), 0)),
        scratch_shapes=[pltpu.VMEM((D_MODEL, 2 * D_FF), BF16),
                        pltpu.VMEM((D_FF, D_MODEL), BF16)],
    )
    return pl.pallas_call(
        functools.partial(_expert_kernel, tb=tb),
        grid_spec=gs,
        out_shape=jax.ShapeDtypeStruct((n_slots * ROW_TILES, LANES), F32),
        compiler_params=_cparams(("arbitrary",)),
        name="expert_ffn",
    )(blk_e, nused, xs, w_up, b_up, w_down, b_down)


def _combine_kernel(pos_ref, x2_ref, gate_ref, gfin_ref, y_hbm, out_ref, buf, sem, *, tm):
    def buf_row(k, t):
        return buf.at[pl.ds(pl.multiple_of((k * tm + t) * ROW_TILES, ROW_TILES), ROW_TILES), :]

    def start_tok(t, c):
        for k in range(TOP_K):
            pltpu.make_async_copy(y_hbm.at[pos_ref[k, t]], buf_row(k, t), sem).start(priority=k % 2)
        return c

    lax.fori_loop(0, tm, start_tok, 0)

    def wait_batch(w, c):
        for _ in range(WAIT_BATCH):
            pltpu.make_async_copy(y_hbm.at[0], buf_row(0, 0), sem).wait()
        return c

    lax.fori_loop(0, tm * TOP_K // WAIT_BATCH, wait_batch, 0)

    g = gate_ref[...]
    pieces = []
    for t in range(ROW_TILES):
        acc = x2_ref[:, t * LANES:(t + 1) * LANES]
        for k in range(TOP_K):
            acc = acc + g[:, k:k + 1] * buf[pl.ds(k * tm * ROW_TILES + t, tm, stride=ROW_TILES), :]
        pieces.append(acc)
    x3 = jnp.concatenate(pieces, axis=-1)
    out_ref[...] = _rms(x3, gfin_ref[...])


def _combine(pos, x2, gate_nt, g_final, y, *, tm):
    n = x2.shape[0]
    return pl.pallas_call(
        functools.partial(_combine_kernel, tm=tm),
        grid=(n // tm,),
        in_specs=[pl.BlockSpec((TOP_K, tm), lambda i: (0, i), memory_space=pltpu.SMEM),
                  pl.BlockSpec((tm, D_MODEL), lambda i: (i, 0)),
                  pl.BlockSpec((tm, TOP_K), lambda i: (i, 0)),
                  pl.BlockSpec((1, D_MODEL), lambda i: (0, 0)),
                  pl.BlockSpec(memory_space=pl.ANY)],
        out_specs=pl.BlockSpec((tm, D_MODEL), lambda i: (i, 0)),
        out_shape=jax.ShapeDtypeStruct((n, D_MODEL), F32),
        scratch_shapes=[pltpu.VMEM((TOP_K * tm * ROW_TILES, LANES), F32),
                        pltpu.SemaphoreType.DMA(())],
        compiler_params=_cparams(("arbitrary",)),
        name="combine",
    )(pos, x2, gate_nt, g_final, y)


def _head_pad_cols(w, per_head_in, take, place):
    kdim = w.shape[0]
    w3 = w.reshape(kdim, MLA_HEADS, per_head_in)
    out = jnp.zeros((kdim, MLA_HEADS, HEAD_PAD), w.dtype)
    for (t0, t1), p0 in zip(take, place):
        out = out.at[:, :, p0:p0 + (t1 - t0)].set(w3[:, :, t0:t1])
    return out.reshape(kdim, MLA_HEADS * HEAD_PAD)


def kernel(x, mem, positions, g_mix, w_in, g_q, w_uq, g_kv, w_ukv, w_dw, b_dw, g_conv_ln, b_conv_ln, w_out,
           g_xattn, g_mem, w_xq, w_xkv, w_xo, g_ffn, w_router, b_router, w_up, b_up, w_down, b_down, g_final):
    batch, seq, _ = x.shape
    mem_len = mem.shape[1]
    n = batch * seq
    half = MLA_ROPE // 2
    r0 = MLA_NOPE
    assert w_in.shape[0] == 1, "one trunk layer"

    row = lambda v: v.reshape(1, -1).astype(F32)

    wi = w_in[0]
    o1, o2, o3 = Q_LORA, Q_LORA + KV_LORA, Q_LORA + KV_LORA + MLA_ROPE
    kr_blk = jnp.zeros((D_MODEL, HEAD_PAD), F32).at[:, r0:r0 + MLA_ROPE].set(wi[:, o2:o3])
    kr_swp = (jnp.zeros((D_MODEL, HEAD_PAD), F32)
              .at[:, r0:r0 + half].set(wi[:, o2 + half:o3])
              .at[:, r0 + half:r0 + MLA_ROPE].set(wi[:, o2:o2 + half]))
    w_in_p = jnp.concatenate([wi[:, :o2], kr_blk, kr_swp, wi[:, o3:]], axis=1).astype(BF16)
    per_q = MLA_NOPE + MLA_ROPE
    w_uq_p = _head_pad_cols(w_uq[0], per_q, [(0, per_q)], [0]).astype(BF16)
    w_uq_s = _head_pad_cols(w_uq[0], per_q, [(MLA_NOPE + half, per_q), (MLA_NOPE, MLA_NOPE + half)],
                            [r0, r0 + half]).astype(BF16)
    per_kv = MLA_NOPE + MLA_V
    w_uk_k = _head_pad_cols(w_ukv[0], per_kv, [(0, MLA_NOPE)], [0]).astype(BF16)
    w_uk_v = (w_ukv[0].reshape(KV_LORA, MLA_HEADS, per_kv)[:, :, MLA_NOPE:]
              .reshape(KV_LORA, MLA_WIDTH).astype(BF16))
    inv = ROPE_THETA ** (-jnp.arange(0, MLA_ROPE, 2, dtype=F32) / MLA_ROPE)
    inv_c = (jnp.zeros((LANES, 1), F32).at[r0:r0 + half, 0].set(inv).at[r0 + half:r0 + MLA_ROPE, 0].set(inv))
    sgn_c = (jnp.zeros((LANES, 1), F32).at[r0:r0 + half, 0].set(-1.0).at[r0 + half:r0 + MLA_ROPE, 0].set(1.0))

    x2d = x.reshape(n, D_MODEL)
    pos_row = positions.reshape(1, n).astype(I32)

    q_t, k, v_t, y_conv = _mix_in(x2d, pos_row, inv_c, sgn_c, row(g_mix[0]), w_in_p, row(g_q[0]),
                                  jnp.transpose(w_uq_p), jnp.transpose(w_uq_s), row(g_kv[0]), w_uk_k,
                                  jnp.transpose(w_uk_v), w_dw[0].astype(F32), row(b_dw[0]),
                                  row(g_conv_ln[0]), row(b_conv_ln[0]), seq=seq)
    y_mla = _attn(q_t, k, v_t, batch=batch, seq=seq)
    kv = _mem_kv(mem.reshape(batch * mem_len, D_MODEL), row(g_mem[0]), w_xkv[0].astype(BF16),
                 batch=batch, mem_len=mem_len)

    tm_mid = min(TM_MID, seq)
    wr = jnp.zeros((D_MODEL, LANES), F32).at[:, :N_EXPERTS].set(w_router[0])
    wr_h = wr.astype(BF16)
    wr_l = (wr - wr_h.astype(F32)).astype(BF16)
    b_r = jnp.broadcast_to(b_router[0].reshape(N_EXPERTS, 1), (N_EXPERTS, LANES)).astype(F32)[:, 0:1]
    tri = (lax.broadcasted_iota(I32, (tm_mid, tm_mid), 0)
           < lax.broadcasted_iota(I32, (tm_mid, tm_mid), 1)).astype(BF16)
    wo = w_out[0].astype(BF16)
    x2, h3r, idx, gate, rank, cnt = _mid(
        x2d, y_mla, y_conv, kv, wo[:MLA_WIDTH], wo[MLA_WIDTH:], row(g_xattn[0]), w_xq[0].astype(BF16),
        w_xo[0].astype(BF16), row(g_ffn[0]), wr_h, wr_l, b_r, tri, seq=seq, mem_len=mem_len, tm=tm_mid)

    tb = TB_EXPERT
    counts = cnt[:, 0]
    padded = (counts + tb - 1) // tb * tb
    pend = jnp.cumsum(padded)
    pstart = pend - padded
    n_slots = n * TOP_K + N_EXPERTS * tb
    nb = n_slots // tb
    blk_first = jnp.arange(nb, dtype=I32) * tb
    blk_e = jnp.minimum(jnp.sum((pend[None, :] <= blk_first[:, None]).astype(I32), axis=1),
                        N_EXPERTS - 1).astype(I32)
    nused = (pend[-1:] // tb).astype(I32)
    pstart_l = jnp.broadcast_to(pstart.reshape(N_EXPERTS, 1), (N_EXPERTS, LANES)).astype(I32)

    pos = _pos(idx, rank, pstart_l, tm=tm_mid)
    xs = _dispatch((pstart + counts).astype(I32), (padded - counts).astype(I32), pos,
                   h3r.reshape(n, PACK_TILES, LANES), n_slots=n_slots, tm=min(TM_DISP, n))
    y = _expert(blk_e, nused, xs.reshape(n_slots * PACK_TILES, LANES), w_up[0],
                b_up[0].reshape(N_EXPERTS, 1, 2 * D_FF), w_down[0], b_down[0].reshape(N_EXPERTS, 1, D_MODEL), tb=tb)
    out = _combine(pos, x2, jnp.transpose(gate), row(g_final), y.reshape(n_slots, ROW_TILES, LANES),
                   tm=min(TM_COMB, n))
    return out.reshape(batch, seq, D_MODEL)
```

```python
import functools

import jax
import jax.numpy as jnp
from jax import lax
from jax.experimental import pallas as pl
from jax.experimental.pallas import tpu as pltpu

F32 = jnp.float32
BF16 = jnp.bfloat16
I32 = jnp.int32

D_MODEL = 1024
MLA_HEADS = 8
MLA_NOPE = 64
MLA_ROPE = 32
MLA_V = 64
MLA_WIDTH = MLA_HEADS * MLA_V
Q_LORA = 256
KV_LORA = 128
ROPE_THETA = 10000.0
CHUNK = 64
CONV_CH = 512
CONV_WIDTH = 31
X_HEADS = 4
X_HEAD_DIM = D_MODEL // X_HEADS
N_EXPERTS = 32
TOP_K = 4
D_FF = D_MODEL
SWIGLU_LIMIT = 7.0
SWIGLU_ALPHA = 1.702
NORM_EPS = 1e-5

LANES = 128
SUBLANES = 8
ROW_TILES = D_MODEL // LANES
PACK_TILES = ROW_TILES // 2
HI_MASK = 0xFFFF0000
VMEM_LIMIT_BYTES = 56 * 1024 * 1024

HEAD_PAD = LANES
CONV_HALO = 32

TM_MIX = 512
TQ_ATTN = 512
TM_MID = 512
TM_DISP = 512
TB_EXPERT = 512
TM_COMB = 256
WAIT_BATCH = 64
ISSUE_UNROLL = 4


def _rms(x, g):
    return x * lax.rsqrt(jnp.mean(x * x, axis=-1, keepdims=True) + NORM_EPS) * g


def _cparams(sem):
    return pltpu.CompilerParams(dimension_semantics=sem, vmem_limit_bytes=VMEM_LIMIT_BYTES)


def _mix_in_kernel(x_ref, pos_ref, inv_ref, sgn_ref, gmix_ref, win_ref, gq_ref, wuq_ref, wuqs_ref,
                   gkv_ref, wukk_ref, wukv_ref, wdw_ref, bdw_ref, gln_ref, bln_ref,
                   q_ref, k_ref, v_ref, yc_ref, zbuf, zsh, *, tiles_per_batch, tm):
    i = pl.program_id(0)
    x = x_ref[...]
    h = _rms(x, gmix_ref[...]).astype(BF16)
    u = jnp.dot(h, win_ref[...], preferred_element_type=F32)
    cq = u[:, 0:256]
    ckv = u[:, 256:384]
    kr = u[:, 384:512]
    krs = u[:, 512:640]
    a = u[:, 640:1152]
    gate = u[:, 1152:1664]

    ang_t = inv_ref[...] * pos_ref[...].astype(F32)
    cos_t = jnp.cos(ang_t)
    sin_t = jnp.sin(ang_t) * sgn_ref[...]
    cosb = jnp.transpose(cos_t)
    sinb = jnp.transpose(sin_t)

    nt = (((1,), (1,)), ((), ()))
    cqn = _rms(cq, gq_ref[...]).astype(BF16)
    qm = lax.dot_general(wuq_ref[...], cqn, nt, preferred_element_type=F32)
    qs = lax.dot_general(wuqs_ref[...], cqn, nt, preferred_element_type=F32)
    for hd in range(MLA_HEADS):
        sl = slice(hd * HEAD_PAD, (hd + 1) * HEAD_PAD)
        q_ref[sl, :] = (qm[sl, :] * cos_t + qs[sl, :] * sin_t).astype(BF16)

    ckvn = _rms(ckv, gkv_ref[...]).astype(BF16)
    kk = jnp.dot(ckvn, wukk_ref[...], preferred_element_type=F32)
    v_ref[...] = lax.dot_general(wukv_ref[...], ckvn, nt, preferred_element_type=F32).astype(BF16)
    krot = kr * cosb + krs * sinb
    for hd in range(MLA_HEADS):
        sl = slice(hd * HEAD_PAD, (hd + 1) * HEAD_PAD)
        k_ref[:, sl] = (kk[:, sl] + krot).astype(BF16)

    z = a * jax.nn.sigmoid(gate)

    @pl.when(i % tiles_per_batch == 0)
    def _():
        zbuf[0:CONV_HALO, :] = jnp.zeros((CONV_HALO, CONV_CH), F32)

    zbuf[CONV_HALO:CONV_HALO + tm, :] = z
    off = CONV_HALO - (CONV_WIDTH - 1)
    rows = 32
    span = tm + CONV_HALO - SUBLANES
    step = 64
    for r in range(1, SUBLANES):
        for c0 in range(0, span, step):
            cl = min(step, span - c0)
            zsh[r - 1, c0:c0 + cl, :] = zbuf[c0 + r:c0 + r + cl, :]
    for r0 in range(0, tm, rows):
        acc = jnp.zeros((rows, CONV_CH), F32) + bdw_ref[...]
        for j in range(CONV_WIDTH):
            a8, ph = divmod(j + off, SUBLANES)
            lo_row = r0 + a8 * SUBLANES
            tap = zbuf[lo_row:lo_row + rows, :] if ph == 0 else zsh[ph - 1, lo_row:lo_row + rows, :]
            acc = acc + wdw_ref[j:j + 1, :] * tap
        mu = jnp.mean(acc, axis=-1, keepdims=True)
        cen = acc - mu
        var = jnp.mean(cen * cen, axis=-1, keepdims=True)
        y = cen * lax.rsqrt(var + NORM_EPS) * gln_ref[...] + bln_ref[...]
        yc_ref[r0:r0 + rows, :] = (y * jax.nn.sigmoid(y)).astype(BF16)
    zbuf[0:CONV_HALO, :] = zbuf[tm:tm + CONV_HALO, :]


def _mix_in(x2d, pos_row, inv_c, sgn_c, g_mix, w_in_p, g_q, w_uq_t, w_uq_st, g_kv, w_uk_k, w_uk_vt,
            w_dw, b_dw, g_ln, b_ln, *, seq):
    n = x2d.shape[0]
    tm = min(TM_MIX, seq)
    full = lambda a: pl.BlockSpec(a.shape, lambda i: (0,) * a.ndim)
    consts = [inv_c, sgn_c, g_mix, w_in_p, g_q, w_uq_t, w_uq_st, g_kv, w_uk_k, w_uk_vt, w_dw, b_dw, g_ln, b_ln]
    return pl.pallas_call(
        functools.partial(_mix_in_kernel, tiles_per_batch=seq // tm, tm=tm),
        grid=(n // tm,),
        in_specs=[pl.BlockSpec((tm, D_MODEL), lambda i: (i, 0)),
                  pl.BlockSpec((1, tm), lambda i: (0, i))] + [full(a) for a in consts],
        out_specs=[pl.BlockSpec((MLA_HEADS * HEAD_PAD, tm), lambda i: (0, i)),
                   pl.BlockSpec((tm, MLA_HEADS * HEAD_PAD), lambda i: (i, 0)),
                   pl.BlockSpec((MLA_WIDTH, tm), lambda i: (0, i)),
                   pl.BlockSpec((tm, CONV_CH), lambda i: (i, 0))],
        out_shape=[jax.ShapeDtypeStruct((MLA_HEADS * HEAD_PAD, n), BF16),
                   jax.ShapeDtypeStruct((n, MLA_HEADS * HEAD_PAD), BF16),
                   jax.ShapeDtypeStruct((MLA_WIDTH, n), BF16),
                   jax.ShapeDtypeStruct((n, CONV_CH), BF16)],
        scratch_shapes=[pltpu.VMEM((CONV_HALO + tm + SUBLANES, CONV_CH), F32),
                        pltpu.VMEM((SUBLANES - 1, CONV_HALO + tm, CONV_CH), F32)],
        compiler_params=_cparams(("arbitrary",)),
        name="mix_in",
    )(x2d, pos_row, *consts)


ATTN_GROUP = 4


TK_ATTN = 512


def _attn_kernel(q_ref, k_ref, v_ref, o_ref, *, tq):
    i = pl.program_id(1)
    tk = min(TK_ATTN, tq)
    per_q = tq // tk
    c2 = (MLA_NOPE + MLA_ROPE) ** -0.5 * 1.4426950408889634
    key_c = lax.broadcasted_iota(I32, (tk, tq), 0) // CHUNK
    qry_c = lax.broadcasted_iota(I32, (tk, tq), 1) // CHUNK

    for g in range(MLA_HEADS // ATTN_GROUP):
        heads = list(range(g * ATTN_GROUP, (g + 1) * ATTN_GROUP))
        qts = [q_ref[hd * HEAD_PAD:(hd + 1) * HEAD_PAD, :] for hd in heads]

        def step(j, carry, diag):
            start = pl.multiple_of(j * tk, tk)
            new = []
            for n_, hd in enumerate(heads):
                m, l, acc = carry[n_]
                kj = k_ref[pl.ds(start, tk), hd * HEAD_PAD:(hd + 1) * HEAD_PAD]
                vj = v_ref[hd * MLA_V:(hd + 1) * MLA_V, pl.ds(start, tk)]
                s = jnp.dot(kj, qts[n_], preferred_element_type=F32)
                if diag is not None:
                    s = jnp.where(key_c + diag * (tk // CHUNK) <= qry_c, s, -jnp.inf)
                m_new = jnp.maximum(m, jnp.max(s, axis=0, keepdims=True))
                alpha = jnp.exp2((m - m_new) * c2)
                p = jnp.exp2((s - m_new) * c2)
                l = alpha * l + jnp.sum(p, axis=0, keepdims=True)
                acc = alpha * acc + jnp.dot(vj, p.astype(BF16), preferred_element_type=F32)
                new.append((m_new, l, acc))
            return tuple(new)

        init = tuple((jnp.full((1, tq), -jnp.inf, F32), jnp.zeros((1, tq), F32), jnp.zeros((MLA_V, tq), F32))
                     for _ in heads)
        carry = lax.fori_loop(0, i * per_q, functools.partial(step, diag=None), init)
        for d in range(per_q):
            carry = step(i * per_q + d, carry, d)
        outs = [acc / l for (_, l, acc) in carry]
        for pr in range(ATTN_GROUP // 2):
            col = (heads[0] // 2 + pr) * LANES
            pair = jnp.concatenate([outs[2 * pr], outs[2 * pr + 1]], axis=0)
            o_ref[:, col:col + LANES] = jnp.transpose(pair).astype(BF16)


def _attn(q_t, k, v_t, *, batch, seq):
    tq = min(TQ_ATTN, seq)
    nq = seq // tq
    return pl.pallas_call(
        functools.partial(_attn_kernel, tq=tq),
        grid=(batch, nq),
        in_specs=[pl.BlockSpec((MLA_HEADS * HEAD_PAD, tq), lambda b, i: (0, b * nq + i)),
                  pl.BlockSpec((seq, MLA_HEADS * HEAD_PAD), lambda b, i: (b, 0)),
                  pl.BlockSpec((MLA_WIDTH, seq), lambda b, i: (0, b))],
        out_specs=pl.BlockSpec((tq, MLA_WIDTH), lambda b, i: (b * nq + i, 0)),
        out_shape=jax.ShapeDtypeStruct((batch * seq, MLA_WIDTH), BF16),
        compiler_params=_cparams(("arbitrary", "arbitrary")),
        name="mla_attn",
    )(q_t, k, v_t)


def _mem_kv_kernel(mem_ref, g_ref, w_ref, kv_ref):
    mn = _rms(mem_ref[...], g_ref[...]).astype(BF16)
    kv_ref[...] = jnp.dot(mn, w_ref[...], preferred_element_type=F32).astype(BF16)


def _mem_kv(mem2d, g_mem, w_xkv, *, batch, mem_len):
    return pl.pallas_call(
        _mem_kv_kernel,
        grid=(batch,),
        in_specs=[pl.BlockSpec((mem_len, D_MODEL), lambda b: (b, 0)),
                  pl.BlockSpec((1, D_MODEL), lambda b: (0, 0)),
                  pl.BlockSpec((D_MODEL, 2 * D_MODEL), lambda b: (0, 0))],
        out_specs=pl.BlockSpec((mem_len, 2 * D_MODEL), lambda b: (b, 0)),
        out_shape=jax.ShapeDtypeStruct((batch * mem_len, 2 * D_MODEL), BF16),
        compiler_params=_cparams(("arbitrary",)),
        name="mem_kv",
    )(mem2d, g_mem, w_xkv)


def _mid_kernel(x_ref, ya_ref, yc_ref, kv_ref, woa_ref, wob_ref, gx_ref, wxq_ref, wxo_ref, gf_ref,
                wrh_ref, wrl_ref, br_ref, tri_ref,
                x2_ref, h3_ref, idx_ref, gate_ref, rank_ref, cnt_ref, runs_ref, run_ref, *, tm):
    i = pl.program_id(0)

    @pl.when(i == 0)
    def _():
        run_ref[...] = jnp.zeros_like(run_ref)

    x1 = (x_ref[...]
          + jnp.dot(ya_ref[...], woa_ref[...], preferred_element_type=F32)
          + jnp.dot(yc_ref[...], wob_ref[...], preferred_element_type=F32))
    h = _rms(x1, gx_ref[...]).astype(BF16)
    q = jnp.dot(h, wxq_ref[...], preferred_element_type=F32).astype(BF16)
    nt = (((1,), (1,)), ((), ()))
    os = []
    for hd in range(X_HEADS):
        sl = slice(hd * X_HEAD_DIM, (hd + 1) * X_HEAD_DIM)
        vsl = slice(D_MODEL + hd * X_HEAD_DIM, D_MODEL + (hd + 1) * X_HEAD_DIM)
        s = lax.dot_general(q[:, sl], kv_ref[:, sl], nt, preferred_element_type=F32) * (X_HEAD_DIM ** -0.5)
        e = jnp.exp(s - jnp.max(s, axis=-1, keepdims=True))
        p = (e / jnp.sum(e, axis=-1, keepdims=True)).astype(BF16)
        os.append(jnp.dot(p, kv_ref[:, vsl], preferred_element_type=F32).astype(BF16))
    o = jnp.concatenate(os, axis=-1)
    x2 = x1 + jnp.dot(o, wxo_ref[...], preferred_element_type=F32)
    x2_ref[...] = x2
    h3 = _rms(x2, gf_ref[...])
    hh = h3.astype(BF16)
    hf = hh.astype(F32)
    half_d = D_MODEL // 2
    word = (lax.shift_right_logical(pltpu.bitcast(hf[:, :half_d], jnp.uint32), jnp.uint32(16))
            | (pltpu.bitcast(hf[:, half_d:], jnp.uint32) & jnp.uint32(HI_MASK)))
    for t in range(PACK_TILES):
        h3_ref[pl.ds(t, tm, stride=PACK_TILES), :] = word[:, t * LANES:(t + 1) * LANES]

    hl = (h3 - hf).astype(BF16)
    lg = (jnp.dot(hh, wrh_ref[...], preferred_element_type=F32)
          + (jnp.dot(hh, wrl_ref[...], preferred_element_type=F32)
             + jnp.dot(hl, wrh_ref[...], preferred_element_type=F32)))
    lgt = jnp.transpose(lg)[0:N_EXPERTS, :] + br_ref[...]
    eid = lax.broadcasted_iota(I32, (N_EXPERTS, tm), 0)
    vals, hots = [], []
    cur = lgt
    for k in range(TOP_K):
        mx = jnp.max(cur, axis=0, keepdims=True)
        ik = jnp.min(jnp.where(cur == mx, eid, N_EXPERTS), axis=0, keepdims=True)
        hot = eid == ik
        cur = jnp.where(hot, -jnp.inf, cur)
        idx_ref[k:k + 1, :] = ik
        vals.append(mx)
        hots.append(hot)
    es = [jnp.exp(v - vals[0]) for v in vals]
    den = es[0] + es[1] + es[2] + es[3]
    for k in range(TOP_K):
        gate_ref[k:k + 1, :] = es[k] / den
    cnt = jnp.zeros((N_EXPERTS, tm), F32)
    for k in range(TOP_K):
        cnt = cnt + hots[k].astype(F32)
    run = run_ref[:, 0:1]
    runs_ref[0] = run_ref[...].astype(I32)
    tot = jnp.dot(cnt.astype(BF16), tri_ref[...], preferred_element_type=F32) + run
    for k in range(TOP_K):
        rank_ref[k:k + 1, :] = jnp.sum(jnp.where(hots[k], tot, 0.0), axis=0, keepdims=True).astype(I32)
    new_run = run + jnp.sum(cnt, axis=1, keepdims=True)
    run_ref[...] = jnp.broadcast_to(new_run, run_ref.shape)
    cnt_ref[...] = jnp.broadcast_to(new_run, cnt_ref.shape).astype(I32)


def _mid(x2d, ya, yc, kv, w_oa, w_ob, g_x, w_xq, w_xo, g_f, wr_h, wr_l, b_r, tri, *, seq, mem_len, tm):
    n = x2d.shape[0]
    tpb = seq // tm
    full = lambda a: pl.BlockSpec(a.shape, lambda i: (0,) * a.ndim)
    consts_a = [w_oa, w_ob, g_x, w_xq, w_xo, g_f, wr_h, wr_l, b_r, tri]
    return pl.pallas_call(
        functools.partial(_mid_kernel, tm=tm),
        grid=(n // tm,),
        in_specs=[pl.BlockSpec((tm, D_MODEL), lambda i: (i, 0)),
                  pl.BlockSpec((tm, MLA_WIDTH), lambda i: (i, 0)),
                  pl.BlockSpec((tm, CONV_CH), lambda i: (i, 0)),
                  pl.BlockSpec((mem_len, 2 * D_MODEL), lambda i: (i // tpb, 0))] + [full(a) for a in consts_a],
        out_specs=[pl.BlockSpec((tm, D_MODEL), lambda i: (i, 0)),
                   pl.BlockSpec((tm * PACK_TILES, LANES), lambda i: (i, 0)),
                   pl.BlockSpec((TOP_K, tm), lambda i: (0, i)),
                   pl.BlockSpec((TOP_K, tm), lambda i: (0, i)),
                   pl.BlockSpec((TOP_K, tm), lambda i: (0, i)),
                   pl.BlockSpec((N_EXPERTS, LANES), lambda i: (0, 0)),
                   pl.BlockSpec((1, N_EXPERTS, LANES), lambda i: (i, 0, 0))],
        out_shape=[jax.ShapeDtypeStruct((n, D_MODEL), F32),
                   jax.ShapeDtypeStruct((n * PACK_TILES, LANES), jnp.uint32),
                   jax.ShapeDtypeStruct((TOP_K, n), I32),
                   jax.ShapeDtypeStruct((TOP_K, n), F32),
                   jax.ShapeDtypeStruct((TOP_K, n), I32),
                   jax.ShapeDtypeStruct((N_EXPERTS, LANES), I32),
                   jax.ShapeDtypeStruct((n // tm, N_EXPERTS, LANES), I32)],
        scratch_shapes=[pltpu.VMEM((N_EXPERTS, LANES), F32)],
        compiler_params=_cparams(("arbitrary",)),
        name="mid",
    )(x2d, ya, yc, kv, *consts_a)


def _pos_kernel(idx_ref, rank_ref, pstart_ref, adj_ref, pos_ref, lpos_ref, *, tm):
    eid = lax.broadcasted_iota(I32, (N_EXPERTS, tm), 0)
    ps = pstart_ref[:, 0:1]
    adj = adj_ref[0][:, 0:1]
    for k in range(TOP_K):
        hot = eid == idx_ref[k:k + 1, :]
        rk = rank_ref[k:k + 1, :]
        pos_ref[k:k + 1, :] = jnp.sum(jnp.where(hot, ps, 0), axis=0, keepdims=True) + rk
        lpos_ref[k:k + 1, :] = (jnp.sum(jnp.where(hot, adj, 0), axis=0, keepdims=True) + rk) * ROW_TILES


def _pos(idx, rank, pstart_l, adj_l, *, tm):
    n = idx.shape[1]
    return pl.pallas_call(
        functools.partial(_pos_kernel, tm=tm),
        grid=(n // tm,),
        in_specs=[pl.BlockSpec((TOP_K, tm), lambda i: (0, i)),
                  pl.BlockSpec((TOP_K, tm), lambda i: (0, i)),
                  pl.BlockSpec((N_EXPERTS, LANES), lambda i: (0, 0)),
                  pl.BlockSpec((1, N_EXPERTS, LANES), lambda i: (i, 0, 0))],
        out_specs=[pl.BlockSpec((TOP_K, tm), lambda i: (0, i)),
                   pl.BlockSpec((TOP_K, tm), lambda i: (0, i))],
        out_shape=[jax.ShapeDtypeStruct((TOP_K, n), I32),
                   jax.ShapeDtypeStruct((TOP_K, n), I32)],
        compiler_params=_cparams(("arbitrary",)),
        name="slot_pos",
    )(idx, rank, pstart_l, adj_l)


def _row_copy(src, src_row, dst, dst_row, sem):
    return pltpu.make_async_copy(src.at[src_row], dst.at[dst_row], sem)


def _dispatch_kernel(pad_start_ref, pad_cnt_ref, pos_ref, h3_ref, xs_hbm, zrow, sem, zsem, *, tm):
    i = pl.program_id(0)

    @pl.when(i == 0)
    def _():
        zrow[...] = jnp.zeros_like(zrow)

        def per_expert(e, c):
            s0 = pad_start_ref[e]

            def start_one(r, c2):
                pltpu.make_async_copy(zrow.at[0], xs_hbm.at[s0 + r], zsem).start()
                return c2

            lax.fori_loop(0, pad_cnt_ref[e], start_one, 0)

            def wait_one(r, c2):
                pltpu.make_async_copy(zrow.at[0], xs_hbm.at[s0 + r], zsem).wait()
                return c2

            lax.fori_loop(0, pad_cnt_ref[e], wait_one, 0)
            return c

        lax.fori_loop(0, N_EXPERTS, per_expert, 0)

    def start_tok(tt, c):
        for u in range(ISSUE_UNROLL):
            t = tt * ISSUE_UNROLL + u
            for k in range(TOP_K):
                _row_copy(h3_ref, t, xs_hbm, pos_ref[k, t], sem).start(priority=k % 2)
        return c

    lax.fori_loop(0, tm // ISSUE_UNROLL, start_tok, 0)

    def wait_batch(w, c):
        for _ in range(WAIT_BATCH):
            _row_copy(h3_ref, 0, xs_hbm, 0, sem).wait()
        return c

    lax.fori_loop(0, tm * TOP_K // WAIT_BATCH, wait_batch, 0)


def _dispatch(pad_start, pad_cnt, pos, h3r, *, n_slots, tm):
    n = h3r.shape[0]
    gs = pltpu.PrefetchScalarGridSpec(
        num_scalar_prefetch=2,
        grid=(n // tm,),
        in_specs=[pl.BlockSpec((TOP_K, tm), lambda i, a, b: (0, i), memory_space=pltpu.SMEM),
                  pl.BlockSpec((tm, PACK_TILES, LANES), lambda i, a, b: (i, 0, 0))],
        out_specs=pl.BlockSpec(memory_space=pl.ANY),
        scratch_shapes=[pltpu.VMEM((1, PACK_TILES, LANES), jnp.uint32),
                        pltpu.SemaphoreType.DMA(()),
                        pltpu.SemaphoreType.DMA(())],
    )
    return pl.pallas_call(
        functools.partial(_dispatch_kernel, tm=tm),
        grid_spec=gs,
        out_shape=jax.ShapeDtypeStruct((n_slots, PACK_TILES, LANES), jnp.uint32),
        compiler_params=_cparams(("arbitrary",)),
        name="dispatch",
    )(pad_start, pad_cnt, pos, h3r)


def _expert_kernel(blk_e_ref, nused_ref, x_ref, wu_ref, bu_ref, wd_ref, bd_ref, y_ref, wu_bf, wd_bf, *, tb):
    b = pl.program_id(0)

    @pl.when(b < nused_ref[0])
    def _():
        prev = blk_e_ref[jnp.maximum(b - 1, 0)]

        @pl.when(jnp.logical_or(b == 0, blk_e_ref[b] != prev))
        def _():
            wu_bf[...] = wu_ref[...].astype(BF16)
            wd_bf[...] = wd_ref[...].astype(BF16)

        words = [x_ref[pl.ds(t, tb, stride=PACK_TILES), :] for t in range(PACK_TILES)]
        lo = [pltpu.bitcast(lax.shift_left(w, jnp.uint32(16)), F32) for w in words]
        hi = [pltpu.bitcast(w & jnp.uint32(HI_MASK), F32) for w in words]
        x = jnp.concatenate(lo + hi, axis=-1).astype(BF16)
        up = jnp.dot(x, wu_bf[...], preferred_element_type=F32) + bu_ref[...]
        glu = jnp.minimum(up[:, :D_FF], SWIGLU_LIMIT)
        lin = jnp.clip(up[:, D_FF:], -SWIGLU_LIMIT, SWIGLU_LIMIT)
        act = (glu * jax.nn.sigmoid(SWIGLU_ALPHA * glu) * (lin + 1.0)).astype(BF16)
        y = jnp.dot(act, wd_bf[...], preferred_element_type=F32) + bd_ref[...]
        for t in range(ROW_TILES):
            y_ref[pl.ds(t, tb, stride=ROW_TILES), :] = y[:, t * LANES:(t + 1) * LANES]


def _expert(blk_e, nused, xs, w_up, b_up, w_down, b_down, *, tb):
    n_slots = xs.shape[0] // PACK_TILES
    nb = n_slots // tb
    blk = lambda b, be, nu: jnp.minimum(b, nu[0] - 1)
    gs = pltpu.PrefetchScalarGridSpec(
        num_scalar_prefetch=2,
        grid=(nb,),
        in_specs=[pl.BlockSpec((tb * PACK_TILES, LANES), lambda b, be, nu: (blk(b, be, nu), 0)),
                  pl.BlockSpec((None, D_MODEL, 2 * D_FF), lambda b, be, nu: (be[blk(b, be, nu)], 0, 0)),
                  pl.BlockSpec((None, 1, 2 * D_FF), lambda b, be, nu: (be[blk(b, be, nu)], 0, 0)),
                  pl.BlockSpec((None, D_FF, D_MODEL), lambda b, be, nu: (be[blk(b, be, nu)], 0, 0)),
                  pl.BlockSpec((None, 1, D_MODEL), lambda b, be, nu: (be[blk(b, be, nu)], 0, 0))],
        out_specs=pl.BlockSpec((tb * ROW_TILES, LANES), lambda b, be, nu: (blk(b, be, nu), 0)),
        scratch_shapes=[pltpu.VMEM((D_MODEL, 2 * D_FF), BF16),
                        pltpu.VMEM((D_FF, D_MODEL), BF16)],
    )
    return pl.pallas_call(
        functools.partial(_expert_kernel, tb=tb),
        grid_spec=gs,
        out_shape=jax.ShapeDtypeStruct(((n_slots + tb) * ROW_TILES, LANES), F32),
        compiler_params=_cparams(("arbitrary",)),
        name="expert_ffn",
    )(blk_e, nused, xs, w_up, b_up, w_down, b_down)


COMB_PIECE = 32
COMB_UNROLL = 8


def _comb_cap(tm):
    return TOP_K * tm + N_EXPERTS * (COMB_PIECE - 1)


def _combine_kernel(cstart_ref, cpieces_ref, cloc_ref, lpos_ref, gate_ref, x2_ref, gfin_ref, y_hbm, out_ref,
                    chunks, rt, sems, *, tm, n_tiles):
    i = pl.program_id(0)
    slot = i % 2
    piece_rows = COMB_PIECE * ROW_TILES

    def piece_copy(tile, e, p, to_slot):
        src = (cstart_ref[tile * N_EXPERTS + e] + p * COMB_PIECE) * ROW_TILES
        dst = (cloc_ref[tile * N_EXPERTS + e] + p * COMB_PIECE) * ROW_TILES
        return pltpu.make_async_copy(
            y_hbm.at[pl.ds(pl.multiple_of(src, ROW_TILES), piece_rows), :],
            chunks.at[to_slot, pl.ds(pl.multiple_of(dst, ROW_TILES), piece_rows), :],
            sems.at[to_slot])

    def for_pieces(tile, to_slot, start):
        def per_e(e, c):
            def per_p(p, c2):
                cp = piece_copy(tile, e, p, to_slot)
                if start:
                    cp.start()
                else:
                    cp.wait()
                return c2

            lax.fori_loop(0, cpieces_ref[tile * N_EXPERTS + e], per_p, 0)
            return c

        lax.fori_loop(0, N_EXPERTS, per_e, 0)

    @pl.when(i == 0)
    def _():
        for_pieces(0, 0, True)

    @pl.when(i + 1 < n_tiles)
    def _():
        for_pieces(i + 1, 1 - slot, True)

    for_pieces(i, slot, False)

    def tok_group(tt, c):
        for u in range(COMB_UNROLL):
            t = tt * COMB_UNROLL + u
            acc = None
            for k in range(TOP_K):
                r = pl.multiple_of(lpos_ref[t * TOP_K + k], ROW_TILES)
                term = gate_ref[t * TOP_K + k] * chunks[slot, pl.ds(r, ROW_TILES), :]
                acc = term if acc is None else acc + term
            rt[pl.ds(pl.multiple_of(t * ROW_TILES, ROW_TILES), ROW_TILES), :] = acc
        return c

    lax.fori_loop(0, tm // COMB_UNROLL, tok_group, 0)

    moe = jnp.concatenate([rt[pl.ds(s, tm, stride=ROW_TILES), :] for s in range(ROW_TILES)], axis=-1)
    out_ref[...] = _rms(x2_ref[...] + moe, gfin_ref[...])


def _combine(cstart, cpieces, cloc, lpos, gate, x2, g_final, y, *, tm):
    n = x2.shape[0]
    n_tiles = n // tm
    gs = pltpu.PrefetchScalarGridSpec(
        num_scalar_prefetch=3,
        grid=(n_tiles,),
        in_specs=[pl.BlockSpec((TOP_K * tm,), lambda i, a, b, c: (i,), memory_space=pltpu.SMEM),
                  pl.BlockSpec((TOP_K * tm,), lambda i, a, b, c: (i,), memory_space=pltpu.SMEM),
                  pl.BlockSpec((tm, D_MODEL), lambda i, a, b, c: (i, 0)),
                  pl.BlockSpec((1, D_MODEL), lambda i, a, b, c: (0, 0)),
                  pl.BlockSpec(memory_space=pl.ANY)],
        out_specs=pl.BlockSpec((tm, D_MODEL), lambda i, a, b, c: (i, 0)),
        scratch_shapes=[pltpu.VMEM((2, _comb_cap(tm) * ROW_TILES, LANES), F32),
                        pltpu.VMEM((tm * ROW_TILES, LANES), F32),
                        pltpu.SemaphoreType.DMA((2,))],
    )
    return pl.pallas_call(
        functools.partial(_combine_kernel, tm=tm, n_tiles=n_tiles),
        grid_spec=gs,
        out_shape=jax.ShapeDtypeStruct((n, D_MODEL), F32),
        compiler_params=_cparams(("arbitrary",)),
        name="combine",
    )(cstart, cpieces, cloc, lpos, gate, x2, g_final, y)


def _head_pad_cols(w, per_head_in, take, place):
    kdim = w.shape[0]
    w3 = w.reshape(kdim, MLA_HEADS, per_head_in)
    out = jnp.zeros((kdim, MLA_HEADS, HEAD_PAD), w.dtype)
    for (t0, t1), p0 in zip(take, place):
        out = out.at[:, :, p0:p0 + (t1 - t0)].set(w3[:, :, t0:t1])
    return out.reshape(kdim, MLA_HEADS * HEAD_PAD)


def kernel(x, mem, positions, g_mix, w_in, g_q, w_uq, g_kv, w_ukv, w_dw, b_dw, g_conv_ln, b_conv_ln, w_out,
           g_xattn, g_mem, w_xq, w_xkv, w_xo, g_ffn, w_router, b_router, w_up, b_up, w_down, b_down, g_final):
    batch, seq, _ = x.shape
    mem_len = mem.shape[1]
    n = batch * seq
    half = MLA_ROPE // 2
    r0 = MLA_NOPE
    assert w_in.shape[0] == 1, "one trunk layer"

    row = lambda v: v.reshape(1, -1).astype(F32)

    wi = w_in[0]
    o1, o2, o3 = Q_LORA, Q_LORA + KV_LORA, Q_LORA + KV_LORA + MLA_ROPE
    kr_blk = jnp.zeros((D_MODEL, HEAD_PAD), F32).at[:, r0:r0 + MLA_ROPE].set(wi[:, o2:o3])
    kr_swp = (jnp.zeros((D_MODEL, HEAD_PAD), F32)
              .at[:, r0:r0 + half].set(wi[:, o2 + half:o3])
              .at[:, r0 + half:r0 + MLA_ROPE].set(wi[:, o2:o2 + half]))
    w_in_p = jnp.concatenate([wi[:, :o2], kr_blk, kr_swp, wi[:, o3:]], axis=1).astype(BF16)
    per_q = MLA_NOPE + MLA_ROPE
    w_uq_p = _head_pad_cols(w_uq[0], per_q, [(0, per_q)], [0]).astype(BF16)
    w_uq_s = _head_pad_cols(w_uq[0], per_q, [(MLA_NOPE + half, per_q), (MLA_NOPE, MLA_NOPE + half)],
                            [r0, r0 + half]).astype(BF16)
    per_kv = MLA_NOPE + MLA_V
    w_uk_k = _head_pad_cols(w_ukv[0], per_kv, [(0, MLA_NOPE)], [0]).astype(BF16)
    w_uk_v = (w_ukv[0].reshape(KV_LORA, MLA_HEADS, per_kv)[:, :, MLA_NOPE:]
              .reshape(KV_LORA, MLA_WIDTH).astype(BF16))
    inv = ROPE_THETA ** (-jnp.arange(0, MLA_ROPE, 2, dtype=F32) / MLA_ROPE)
    inv_c = (jnp.zeros((LANES, 1), F32).at[r0:r0 + half, 0].set(inv).at[r0 + half:r0 + MLA_ROPE, 0].set(inv))
    sgn_c = (jnp.zeros((LANES, 1), F32).at[r0:r0 + half, 0].set(-1.0).at[r0 + half:r0 + MLA_ROPE, 0].set(1.0))

    x2d = x.reshape(n, D_MODEL)
    pos_row = positions.reshape(1, n).astype(I32)

    q_t, k, v_t, y_conv = _mix_in(x2d, pos_row, inv_c, sgn_c, row(g_mix[0]), w_in_p, row(g_q[0]),
                                  jnp.transpose(w_uq_p), jnp.transpose(w_uq_s), row(g_kv[0]), w_uk_k,
                                  jnp.transpose(w_uk_v), w_dw[0].astype(F32), row(b_dw[0]),
                                  row(g_conv_ln[0]), row(b_conv_ln[0]), seq=seq)
    y_mla = _attn(q_t, k, v_t, batch=batch, seq=seq)
    kv = _mem_kv(mem.reshape(batch * mem_len, D_MODEL), row(g_mem[0]), w_xkv[0].astype(BF16),
                 batch=batch, mem_len=mem_len)

    tm_mid = min(TM_MID, seq)
    wr = jnp.zeros((D_MODEL, LANES), F32).at[:, :N_EXPERTS].set(w_router[0])
    wr_h = wr.astype(BF16)
    wr_l = (wr - wr_h.astype(F32)).astype(BF16)
    b_r = jnp.broadcast_to(b_router[0].reshape(N_EXPERTS, 1), (N_EXPERTS, LANES)).astype(F32)[:, 0:1]
    tri = (lax.broadcasted_iota(I32, (tm_mid, tm_mid), 0)
           < lax.broadcasted_iota(I32, (tm_mid, tm_mid), 1)).astype(BF16)
    wo = w_out[0].astype(BF16)
    x2, h3r, idx, gate, rank, cnt, runs_l = _mid(
        x2d, y_mla, y_conv, kv, wo[:MLA_WIDTH], wo[MLA_WIDTH:], row(g_xattn[0]), w_xq[0].astype(BF16),
        w_xo[0].astype(BF16), row(g_ffn[0]), wr_h, wr_l, b_r, tri, seq=seq, mem_len=mem_len, tm=tm_mid)

    tb = TB_EXPERT
    counts = cnt[:, 0]
    padded = (counts + tb - 1) // tb * tb
    pend = jnp.cumsum(padded)
    pstart = pend - padded
    n_slots = n * TOP_K + N_EXPERTS * tb
    nb = n_slots // tb
    blk_first = jnp.arange(nb, dtype=I32) * tb
    blk_e = jnp.minimum(jnp.sum((pend[None, :] <= blk_first[:, None]).astype(I32), axis=1),
                        N_EXPERTS - 1).astype(I32)
    nused = (pend[-1:] // tb).astype(I32)
    pstart_l = jnp.broadcast_to(pstart.reshape(N_EXPERTS, 1), (N_EXPERTS, LANES)).astype(I32)

    runs = runs_l[:, :, 0]
    run_len = jnp.concatenate([runs[1:], counts[None, :]], axis=0) - runs
    cstart = pstart[None, :] + runs
    cpieces = (run_len + COMB_PIECE - 1) // COMB_PIECE
    cloc = jnp.cumsum(cpieces * COMB_PIECE, axis=1) - cpieces * COMB_PIECE
    adj_l = jnp.broadcast_to((cloc - runs)[:, :, None], runs_l.shape).astype(I32)

    pos, lpos = _pos(idx, rank, pstart_l, adj_l, tm=tm_mid)
    xs = _dispatch((pstart + counts).astype(I32), (padded - counts).astype(I32), pos,
                   h3r.reshape(n, PACK_TILES, LANES), n_slots=n_slots, tm=min(TM_DISP, n))
    y = _expert(blk_e, nused, xs.reshape(n_slots * PACK_TILES, LANES), w_up[0],
                b_up[0].reshape(N_EXPERTS, 1, 2 * D_FF), w_down[0], b_down[0].reshape(N_EXPERTS, 1, D_MODEL), tb=tb)
    out = _combine(cstart.reshape(-1).astype(I32), cpieces.reshape(-1).astype(I32), cloc.reshape(-1).astype(I32),
                   jnp.transpose(lpos).reshape(-1), jnp.transpose(gate).reshape(-1), x2, row(g_final), y, tm=tm_mid)
    return out.reshape(batch, seq, D_MODEL)
```

```python
import functools

import jax
import jax.numpy as jnp
from jax import lax
from jax.experimental import pallas as pl
from jax.experimental.pallas import tpu as pltpu

F32 = jnp.float32
BF16 = jnp.bfloat16
I32 = jnp.int32

D_MODEL = 1024
MLA_HEADS = 8
MLA_NOPE = 64
MLA_ROPE = 32
MLA_V = 64
MLA_WIDTH = MLA_HEADS * MLA_V
Q_LORA = 256
KV_LORA = 128
ROPE_THETA = 10000.0
CHUNK = 64
CONV_CH = 512
CONV_WIDTH = 31
X_HEADS = 4
X_HEAD_DIM = D_MODEL // X_HEADS
N_EXPERTS = 32
TOP_K = 4
D_FF = D_MODEL
SWIGLU_LIMIT = 7.0
SWIGLU_ALPHA = 1.702
NORM_EPS = 1e-5

LANES = 128
SUBLANES = 8
ROW_TILES = D_MODEL // LANES
PACK_TILES = ROW_TILES // 2
HI_MASK = 0xFFFF0000
VMEM_LIMIT_BYTES = 56 * 1024 * 1024

HEAD_PAD = LANES
CONV_HALO = 32

TM_MIX = 512
TQ_ATTN = 1024
TM_MID = 512
TB_EXPERT = 512
MID_SPLIT = 2
COMB_TM = 512


def _rms(x, g):
    return x * lax.rsqrt(jnp.mean(x * x, axis=-1, keepdims=True) + NORM_EPS) * g


def _cparams(sem):
    return pltpu.CompilerParams(dimension_semantics=sem, vmem_limit_bytes=VMEM_LIMIT_BYTES)


def _mix_in_kernel(x_ref, pos_ref, inv_ref, sgn_ref, gmix_ref, win_ref, gq_ref, wuq_ref, wuqs_ref,
                   gkv_ref, wukk_ref, wukv_ref, wdw_ref, bdw_ref, gln_ref, bln_ref,
                   q_ref, k_ref, v_ref, yc_ref, zbuf, zsh, *, tiles_per_batch, tm):
    i = pl.program_id(0)
    x = x_ref[...]
    h = _rms(x, gmix_ref[...]).astype(BF16)
    u = jnp.dot(h, win_ref[...], preferred_element_type=F32)
    cq = u[:, 0:256]
    ckv = u[:, 256:384]
    kr = u[:, 384:512]
    krs = u[:, 512:640]
    a = u[:, 640:1152]
    gate = u[:, 1152:1664]

    r0, r1 = MLA_NOPE, MLA_NOPE + MLA_ROPE
    ang_r = inv_ref[r0:r1, :] * pos_ref[...].astype(F32)
    cos_t = jnp.concatenate([jnp.ones((r0, tm), F32), jnp.cos(ang_r), jnp.ones((HEAD_PAD - r1, tm), F32)], axis=0)
    sin_t = jnp.concatenate([jnp.zeros((r0, tm), F32), jnp.sin(ang_r) * sgn_ref[r0:r1, :],
                             jnp.zeros((HEAD_PAD - r1, tm), F32)], axis=0)
    cosb = jnp.transpose(cos_t)
    sinb = jnp.transpose(sin_t)

    nt = (((1,), (1,)), ((), ()))
    cqn = _rms(cq, gq_ref[...]).astype(BF16)
    qm = lax.dot_general(wuq_ref[...], cqn, nt, preferred_element_type=F32)
    qs = lax.dot_general(wuqs_ref[...], cqn, nt, preferred_element_type=F32)
    for hd in range(MLA_HEADS):
        sl = slice(hd * HEAD_PAD, (hd + 1) * HEAD_PAD)
        q_ref[sl, :] = (qm[sl, :] * cos_t + qs[sl, :] * sin_t).astype(BF16)

    ckvn = _rms(ckv, gkv_ref[...]).astype(BF16)
    kk = jnp.dot(ckvn, wukk_ref[...], preferred_element_type=F32)
    v_ref[...] = lax.dot_general(wukv_ref[...], ckvn, nt, preferred_element_type=F32).astype(BF16)
    krot = kr * cosb + krs * sinb
    for hd in range(MLA_HEADS):
        sl = slice(hd * HEAD_PAD, (hd + 1) * HEAD_PAD)
        k_ref[:, sl] = (kk[:, sl] + krot).astype(BF16)

    z = a * jax.nn.sigmoid(gate)

    @pl.when(i % tiles_per_batch == 0)
    def _():
        zbuf[0:CONV_HALO, :] = jnp.zeros((CONV_HALO, CONV_CH), F32)

    zbuf[CONV_HALO:CONV_HALO + tm, :] = z
    off = CONV_HALO - (CONV_WIDTH - 1)
    rows = 32
    span = tm + CONV_HALO - SUBLANES
    step = 64
    for r in range(1, SUBLANES):
        for c0 in range(0, span, step):
            cl = min(step, span - c0)
            zsh[r - 1, c0:c0 + cl, :] = zbuf[c0 + r:c0 + r + cl, :]
    for r0 in range(0, tm, rows):
        acc = jnp.zeros((rows, CONV_CH), F32) + bdw_ref[...]
        for j in range(CONV_WIDTH):
            a8, ph = divmod(j + off, SUBLANES)
            lo_row = r0 + a8 * SUBLANES
            tap = zbuf[lo_row:lo_row + rows, :] if ph == 0 else zsh[ph - 1, lo_row:lo_row + rows, :]
            acc = acc + wdw_ref[j:j + 1, :] * tap
        mu = jnp.mean(acc, axis=-1, keepdims=True)
        cen = acc - mu
        var = jnp.mean(cen * cen, axis=-1, keepdims=True)
        y = cen * lax.rsqrt(var + NORM_EPS) * gln_ref[...] + bln_ref[...]
        yc_ref[r0:r0 + rows, :] = (y * jax.nn.sigmoid(y)).astype(BF16)
    zbuf[0:CONV_HALO, :] = zbuf[tm:tm + CONV_HALO, :]


def _mix_in(x2d, pos_row, inv_c, sgn_c, g_mix, w_in_p, g_q, w_uq_t, w_uq_st, g_kv, w_uk_k, w_uk_vt,
            w_dw, b_dw, g_ln, b_ln, *, seq):
    n = x2d.shape[0]
    tm = min(TM_MIX, seq)
    full = lambda a: pl.BlockSpec(a.shape, lambda i: (0,) * a.ndim)
    consts = [inv_c, sgn_c, g_mix, w_in_p, g_q, w_uq_t, w_uq_st, g_kv, w_uk_k, w_uk_vt, w_dw, b_dw, g_ln, b_ln]
    return pl.pallas_call(
        functools.partial(_mix_in_kernel, tiles_per_batch=seq // tm, tm=tm),
        grid=(n // tm,),
        in_specs=[pl.BlockSpec((tm, D_MODEL), lambda i: (i, 0)),
                  pl.BlockSpec((1, tm), lambda i: (0, i))] + [full(a) for a in consts],
        out_specs=[pl.BlockSpec((MLA_HEADS * HEAD_PAD, tm), lambda i: (0, i)),
                   pl.BlockSpec((tm, MLA_HEADS * HEAD_PAD), lambda i: (i, 0)),
                   pl.BlockSpec((MLA_WIDTH, tm), lambda i: (0, i)),
                   pl.BlockSpec((tm, CONV_CH), lambda i: (i, 0))],
        out_shape=[jax.ShapeDtypeStruct((MLA_HEADS * HEAD_PAD, n), BF16),
                   jax.ShapeDtypeStruct((n, MLA_HEADS * HEAD_PAD), BF16),
                   jax.ShapeDtypeStruct((MLA_WIDTH, n), BF16),
                   jax.ShapeDtypeStruct((n, CONV_CH), BF16)],
        scratch_shapes=[pltpu.VMEM((CONV_HALO + tm + SUBLANES, CONV_CH), F32),
                        pltpu.VMEM((SUBLANES - 1, CONV_HALO + tm, CONV_CH), F32)],
        compiler_params=_cparams(("arbitrary",)),
        name="mix_in",
    )(x2d, pos_row, *consts)


ATTN_GROUP = 1


TK_ATTN = 1024


def _attn_kernel(q_ref, k_ref, v_ref, o_ref, *s_scr, tq):
    i = pl.program_id(1)
    tk = min(TK_ATTN, tq)
    per_q = tq // tk
    c2 = (MLA_NOPE + MLA_ROPE) ** -0.5 * 1.4426950408889634
    key_c = lax.broadcasted_iota(I32, (tk, tq), 0) // CHUNK
    qry_c = lax.broadcasted_iota(I32, (tk, tq), 1) // CHUNK

    for g in range(MLA_HEADS // ATTN_GROUP):
        heads = list(range(g * ATTN_GROUP, (g + 1) * ATTN_GROUP))
        qts = [q_ref[hd * HEAD_PAD:(hd + 1) * HEAD_PAD, :] for hd in heads]

        def step(j, carry, diag):
            start = pl.multiple_of(j * tk, tk)
            new = []
            for n_, hd in enumerate(heads):
                m, l, acc = carry[n_]
                kj = k_ref[pl.ds(start, tk), hd * HEAD_PAD:(hd + 1) * HEAD_PAD]
                vj = v_ref[hd * MLA_V:(hd + 1) * MLA_V, pl.ds(start, tk)]
                s = jnp.dot(kj, qts[n_], preferred_element_type=F32)
                if diag is not None:
                    s = jnp.where(key_c + diag * (tk // CHUNK) <= qry_c, s, -jnp.inf)
                m_new = jnp.maximum(m, jnp.max(s, axis=0, keepdims=True))
                s_scr[n_][...] = s
                s = s_scr[n_][...]
                alpha = jnp.exp2((m - m_new) * c2)
                p = jnp.exp2((s - m_new) * c2)
                l = alpha * l + jnp.sum(p, axis=0, keepdims=True)
                acc = alpha * acc + jnp.dot(vj, p.astype(BF16), preferred_element_type=F32)
                new.append((m_new, l, acc))
            return tuple(new)

        init = tuple((jnp.full((1, tq), -jnp.inf, F32), jnp.zeros((1, tq), F32), jnp.zeros((MLA_V, tq), F32))
                     for _ in heads)
        carry = lax.fori_loop(0, i * per_q, functools.partial(step, diag=None), init)
        for d in range(per_q):
            carry = step(i * per_q + d, carry, d)
        for n_, hd in enumerate(heads):
            _, l, acc = carry[n_]
            o_ref[:, hd * MLA_V:(hd + 1) * MLA_V] = jnp.transpose(acc / l).astype(BF16)


def _attn(q_t, k, v_t, *, batch, seq):
    tq = min(TQ_ATTN, seq)
    nq = seq // tq
    return pl.pallas_call(
        functools.partial(_attn_kernel, tq=tq),
        grid=(batch, nq),
        in_specs=[pl.BlockSpec((MLA_HEADS * HEAD_PAD, tq), lambda b, i: (0, b * nq + i)),
                  pl.BlockSpec((seq, MLA_HEADS * HEAD_PAD), lambda b, i: (b, 0)),
                  pl.BlockSpec((MLA_WIDTH, seq), lambda b, i: (0, b))],
        out_specs=pl.BlockSpec((tq, MLA_WIDTH), lambda b, i: (b * nq + i, 0)),
        out_shape=jax.ShapeDtypeStruct((batch * seq, MLA_WIDTH), BF16),
        scratch_shapes=[pltpu.VMEM((min(TK_ATTN, tq), tq), F32) for _ in range(ATTN_GROUP)],
        compiler_params=_cparams(("arbitrary", "arbitrary")),
        name="mla_attn",
    )(q_t, k, v_t)


def _mem_kv_kernel(mem_ref, g_ref, w_ref, kv_ref):
    mn = _rms(mem_ref[...], g_ref[...]).astype(BF16)
    kv_ref[...] = jnp.dot(mn, w_ref[...], preferred_element_type=F32).astype(BF16)


def _mem_kv(mem2d, g_mem, w_xkv, *, batch, mem_len):
    return pl.pallas_call(
        _mem_kv_kernel,
        grid=(batch,),
        in_specs=[pl.BlockSpec((mem_len, D_MODEL), lambda b: (b, 0)),
                  pl.BlockSpec((1, D_MODEL), lambda b: (0, 0)),
                  pl.BlockSpec((D_MODEL, 2 * D_MODEL), lambda b: (0, 0))],
        out_specs=pl.BlockSpec((mem_len, 2 * D_MODEL), lambda b: (b, 0)),
        out_shape=jax.ShapeDtypeStruct((batch * mem_len, 2 * D_MODEL), BF16),
        compiler_params=_cparams(("arbitrary",)),
        name="mem_kv",
    )(mem2d, g_mem, w_xkv)


def _mid_kernel(x_ref, ya_ref, yc_ref, kv_ref, woa_ref, wob_ref, gx_ref, wxq_ref, wxo_ref, gf_ref,
                wrh_ref, wrl_ref, br_ref, tri_ref,
                x2_ref, h3_ref, idx_ref, gate_ref, rank_ref, cnt_ref, runs_ref, run_ref, *, tm):
    i = pl.program_id(0)

    @pl.when(i == 0)
    def _():
        run_ref[...] = jnp.zeros_like(run_ref)

    nt = (((1,), (1,)), ((), ()))
    half_d = D_MODEL // 2
    rows = tm // MID_SPLIT
    lgs = []
    for part in range(MID_SPLIT):
        rs = slice(part * rows, (part + 1) * rows)
        x1 = (x_ref[rs, :]
              + jnp.dot(ya_ref[rs, :], woa_ref[...], preferred_element_type=F32)
              + jnp.dot(yc_ref[rs, :], wob_ref[...], preferred_element_type=F32))
        h = _rms(x1, gx_ref[...]).astype(BF16)
        q = jnp.dot(h, wxq_ref[...], preferred_element_type=F32).astype(BF16)
        os = []
        for hd in range(X_HEADS):
            sl = slice(hd * X_HEAD_DIM, (hd + 1) * X_HEAD_DIM)
            vsl = slice(D_MODEL + hd * X_HEAD_DIM, D_MODEL + (hd + 1) * X_HEAD_DIM)
            s = lax.dot_general(q[:, sl], kv_ref[:, sl], nt, preferred_element_type=F32) * (X_HEAD_DIM ** -0.5)
            e = jnp.exp(s - jnp.max(s, axis=-1, keepdims=True))
            p = (e / jnp.sum(e, axis=-1, keepdims=True)).astype(BF16)
            os.append(jnp.dot(p, kv_ref[:, vsl], preferred_element_type=F32).astype(BF16))
        o = jnp.concatenate(os, axis=-1)
        x2 = x1 + jnp.dot(o, wxo_ref[...], preferred_element_type=F32)
        x2_ref[rs, :] = x2
        h3 = _rms(x2, gf_ref[...])
        hh = h3.astype(BF16)
        hf = hh.astype(F32)
        word = (lax.shift_right_logical(pltpu.bitcast(hf[:, :half_d], jnp.uint32), jnp.uint32(16))
                | (pltpu.bitcast(hf[:, half_d:], jnp.uint32) & jnp.uint32(HI_MASK)))
        for t in range(PACK_TILES):
            h3_ref[pl.ds(part * rows * PACK_TILES + t, rows, stride=PACK_TILES), :] = word[:, t * LANES:(t + 1) * LANES]

        hl = (h3 - hf).astype(BF16)
        lgs.append(jnp.dot(hh, wrh_ref[...], preferred_element_type=F32)
                   + (jnp.dot(hh, wrl_ref[...], preferred_element_type=F32)
                      + jnp.dot(hl, wrh_ref[...], preferred_element_type=F32)))
    lg = jnp.concatenate(lgs, axis=0)
    lgt = jnp.transpose(lg)[0:N_EXPERTS, :] + br_ref[...]
    eid = lax.broadcasted_iota(I32, (N_EXPERTS, tm), 0)
    vals, hots = [], []
    cur = lgt
    for k in range(TOP_K):
        mx = jnp.max(cur, axis=0, keepdims=True)
        ik = jnp.min(jnp.where(cur == mx, eid, N_EXPERTS), axis=0, keepdims=True)
        hot = eid == ik
        cur = jnp.where(hot, -jnp.inf, cur)
        idx_ref[k:k + 1, :] = ik
        vals.append(mx)
        hots.append(hot)
    es = [jnp.exp(v - vals[0]) for v in vals]
    den = es[0] + es[1] + es[2] + es[3]
    for k in range(TOP_K):
        gate_ref[k:k + 1, :] = es[k] / den
    cnt = jnp.zeros((N_EXPERTS, tm), F32)
    for k in range(TOP_K):
        cnt = cnt + hots[k].astype(F32)
    run = run_ref[:, 0:1]
    ctm = min(COMB_TM, tm)
    for j in range(tm // ctm):
        before = run if j == 0 else run + jnp.sum(cnt[:, :j * ctm], axis=1, keepdims=True)
        runs_ref[j] = jnp.broadcast_to(before, (N_EXPERTS, LANES)).astype(I32)
    tot = jnp.dot(cnt.astype(BF16), tri_ref[...], preferred_element_type=F32) + run
    for k in range(TOP_K):
        rank_ref[k:k + 1, :] = jnp.sum(jnp.where(hots[k], tot, 0.0), axis=0, keepdims=True).astype(I32)
    new_run = run + jnp.sum(cnt, axis=1, keepdims=True)
    run_ref[...] = jnp.broadcast_to(new_run, run_ref.shape)
    cnt_ref[...] = jnp.broadcast_to(new_run, cnt_ref.shape).astype(I32)


def _mid(x2d, ya, yc, kv, w_oa, w_ob, g_x, w_xq, w_xo, g_f, wr_h, wr_l, b_r, tri, *, seq, mem_len, tm):
    n = x2d.shape[0]
    tpb = seq // tm
    full = lambda a: pl.BlockSpec(a.shape, lambda i: (0,) * a.ndim)
    consts_a = [w_oa, w_ob, g_x, w_xq, w_xo, g_f, wr_h, wr_l, b_r, tri]
    return pl.pallas_call(
        functools.partial(_mid_kernel, tm=tm),
        grid=(n // tm,),
        in_specs=[pl.BlockSpec((tm, D_MODEL), lambda i: (i, 0)),
                  pl.BlockSpec((tm, MLA_WIDTH), lambda i: (i, 0)),
                  pl.BlockSpec((tm, CONV_CH), lambda i: (i, 0)),
                  pl.BlockSpec((mem_len, 2 * D_MODEL), lambda i: (i // tpb, 0))] + [full(a) for a in consts_a],
        out_specs=[pl.BlockSpec((tm, D_MODEL), lambda i: (i, 0)),
                   pl.BlockSpec((tm * PACK_TILES, LANES), lambda i: (i, 0)),
                   pl.BlockSpec((TOP_K, tm), lambda i: (0, i)),
                   pl.BlockSpec((TOP_K, tm), lambda i: (0, i)),
                   pl.BlockSpec((TOP_K, tm), lambda i: (0, i)),
                   pl.BlockSpec((N_EXPERTS, LANES), lambda i: (0, 0)),
                   pl.BlockSpec((tm // min(COMB_TM, tm), N_EXPERTS, LANES), lambda i: (i, 0, 0))],
        out_shape=[jax.ShapeDtypeStruct((n, D_MODEL), F32),
                   jax.ShapeDtypeStruct((n * PACK_TILES, LANES), jnp.uint32),
                   jax.ShapeDtypeStruct((TOP_K, n), I32),
                   jax.ShapeDtypeStruct((TOP_K, n), F32),
                   jax.ShapeDtypeStruct((TOP_K, n), I32),
                   jax.ShapeDtypeStruct((N_EXPERTS, LANES), I32),
                   jax.ShapeDtypeStruct((n // min(COMB_TM, tm), N_EXPERTS, LANES), I32)],
        scratch_shapes=[pltpu.VMEM((N_EXPERTS, LANES), F32)],
        compiler_params=_cparams(("arbitrary",)),
        name="mid",
    )(x2d, ya, yc, kv, *consts_a)


def _pos_kernel(idx_ref, rank_ref, adj_ref, lpos_ref, *, tm, group):
    i = pl.program_id(0)
    eid = lax.broadcasted_iota(I32, (N_EXPERTS, tm), 0)
    for j in range(group):
        half = ((i * group + j) % 2) * _comb_cap(tm)
        cols = slice(j * tm, (j + 1) * tm)
        adj = adj_ref[j][:, 0:1] + half
        for k in range(TOP_K):
            hot = eid == idx_ref[k:k + 1, cols]
            lpos_ref[k:k + 1, cols] = (jnp.sum(jnp.where(hot, adj, 0), axis=0, keepdims=True)
                                       + rank_ref[k:k + 1, cols]) * ROW_TILES


def _pos(idx, rank, adj_l, *, tm):
    n = idx.shape[1]
    n_tiles = n // tm
    group = 4 if n_tiles % 4 == 0 else 1
    return pl.pallas_call(
        functools.partial(_pos_kernel, tm=tm, group=group),
        grid=(n_tiles // group,),
        in_specs=[pl.BlockSpec((TOP_K, tm * group), lambda i: (0, i)),
                  pl.BlockSpec((TOP_K, tm * group), lambda i: (0, i)),
                  pl.BlockSpec((group, N_EXPERTS, LANES), lambda i: (i, 0, 0))],
        out_specs=pl.BlockSpec((TOP_K, tm * group), lambda i: (0, i)),
        out_shape=jax.ShapeDtypeStruct((TOP_K, n), I32),
        compiler_params=_cparams(("arbitrary",)),
        name="slot_pos",
    )(idx, rank, adj_l)


RUN_BITS = tuple(1 << b for b in range(9, -1, -1))


def _pow2_runs(length, fn):
    off = jnp.int32(0)
    for bit in RUN_BITS:
        take = (length & bit) != 0

        @pl.when(take)
        def _(off=off, bit=bit):
            fn(off, bit)

        off = off + jnp.where(take, bit, 0)


def _dispatch_kernel(pad_start_ref, pad_cnt_ref, cstart_ref, clen_ref, cloc_ref, lpos_ref, h3_ref, xs_hbm,
                     cbuf, sems, zsem, *, tm, n_tiles):
    i = pl.program_id(0)
    slot = i % 2
    cap = _comb_cap(tm)
    rows_per_tile = TOP_K * tm

    @pl.when(i == 0)
    def _():
        zrows = RUN_BITS[0]
        cbuf[pl.ds(cap, zrows)] = jnp.zeros((zrows, PACK_TILES, LANES), jnp.uint32)

        def pad_pass(start):
            def per_expert(e, c):
                def one(off, bit):
                    cp = pltpu.make_async_copy(cbuf.at[pl.ds(cap, bit)],
                                               xs_hbm.at[pl.ds(pad_start_ref[e] + off, bit)], zsem)
                    cp.start() if start else cp.wait()

                _pow2_runs(pad_cnt_ref[e], one)
                return c

            lax.fori_loop(0, N_EXPERTS, per_expert, 0)

        pad_pass(True)
        pad_pass(False)

    def wait_tile(s):
        pltpu.make_async_copy(cbuf.at[pl.ds(s * cap, rows_per_tile)], xs_hbm.at[pl.ds(0, rows_per_tile)],
                              sems.at[s]).wait()

    @pl.when(i >= 2)
    def _():
        wait_tile(slot)

    def tok_group(tt, c):
        for u in range(COMB_UNROLL):
            t = tt * COMB_UNROLL + u
            row = h3_ref[t]
            for k in range(TOP_K):
                cbuf[lax.shift_right_logical(lpos_ref[k * tm + t], 3)] = row
        return c

    lax.fori_loop(0, tm // COMB_UNROLL, tok_group, 0)

    def per_run(e, c):
        src0 = slot * cap + cloc_ref[i * N_EXPERTS + e]
        dst0 = cstart_ref[i * N_EXPERTS + e]

        def one(off, bit):
            pltpu.make_async_copy(cbuf.at[pl.ds(src0 + off, bit)], xs_hbm.at[pl.ds(dst0 + off, bit)],
                                  sems.at[slot]).start()

        _pow2_runs(clen_ref[i * N_EXPERTS + e], one)
        return c

    lax.fori_loop(0, N_EXPERTS, per_run, 0)

    @pl.when(i == n_tiles - 1)
    def _():
        wait_tile(slot)
        if n_tiles >= 2:
            wait_tile(1 - slot)


def _dispatch(pad_start, pad_cnt, cstart, clen, cloc, lpos_tok, h3r, *, n_slots, tm):
    n = h3r.shape[0]
    n_tiles = n // tm
    gs = pltpu.PrefetchScalarGridSpec(
        num_scalar_prefetch=5,
        grid=(n_tiles,),
        in_specs=[pl.BlockSpec((TOP_K * tm,), lambda i, *_: (i,), memory_space=pltpu.SMEM),
                  pl.BlockSpec((tm, PACK_TILES, LANES), lambda i, *_: (i, 0, 0))],
        out_specs=pl.BlockSpec(memory_space=pl.ANY),
        scratch_shapes=[pltpu.VMEM((2 * _comb_cap(tm), PACK_TILES, LANES), jnp.uint32),
                        pltpu.SemaphoreType.DMA((2,)),
                        pltpu.SemaphoreType.DMA(())],
    )
    return pl.pallas_call(
        functools.partial(_dispatch_kernel, tm=tm, n_tiles=n_tiles),
        grid_spec=gs,
        out_shape=jax.ShapeDtypeStruct((n_slots, PACK_TILES, LANES), jnp.uint32),
        compiler_params=_cparams(("arbitrary",)),
        name="dispatch",
    )(pad_start, pad_cnt, cstart, clen, cloc, lpos_tok, h3r)


def _expert_kernel(first_blk_ref, nblk_ref, wu_ref, bu_ref, wd_ref, bd_ref, x_hbm, y_hbm,
                   xbuf, ybuf, wu_bf, wd_bf, xsem, ysem, *, tb):
    e = pl.program_id(0)
    nb = nblk_ref[e]
    b0 = first_blk_ref[e]
    xrows = tb * PACK_TILES
    yrows = tb * ROW_TILES

    def x_copy(j, slot):
        src = pl.multiple_of((b0 + j) * xrows, xrows)
        return pltpu.make_async_copy(x_hbm.at[pl.ds(src, xrows), :], xbuf.at[slot], xsem.at[slot])

    def y_copy(j, slot):
        dst = pl.multiple_of((b0 + j) * yrows, yrows)
        return pltpu.make_async_copy(ybuf.at[slot], y_hbm.at[pl.ds(dst, yrows), :], ysem.at[slot])

    @pl.when(nb > 0)
    def _():
        x_copy(0, 0).start(priority=1)
        wu_bf[...] = wu_ref[...].astype(BF16)
        wd_bf[...] = wd_ref[...].astype(BF16)

    def block(j, c):
        slot = j % 2
        x_copy(j, slot).wait()

        @pl.when(j + 1 < nb)
        def _():
            x_copy(j + 1, 1 - slot).start(priority=1)

        @pl.when(j >= 2)
        def _():
            y_copy(j - 2, slot).wait()

        words = [xbuf[slot, pl.ds(t, tb, stride=PACK_TILES), :] for t in range(PACK_TILES)]
        lo = [pltpu.bitcast(lax.shift_left(w, jnp.uint32(16)), F32) for w in words]
        hi = [pltpu.bitcast(w & jnp.uint32(HI_MASK), F32) for w in words]
        x = jnp.concatenate(lo + hi, axis=-1).astype(BF16)
        up = jnp.dot(x, wu_bf[...], preferred_element_type=F32) + bu_ref[...]
        glu = jnp.minimum(up[:, :D_FF], SWIGLU_LIMIT)
        lin = jnp.clip(up[:, D_FF:], -SWIGLU_LIMIT, SWIGLU_LIMIT)
        act = (glu * jax.nn.sigmoid(SWIGLU_ALPHA * glu) * (lin + 1.0)).astype(BF16)
        y = jnp.dot(act, wd_bf[...], preferred_element_type=F32) + bd_ref[...]
        for t in range(ROW_TILES):
            ybuf[slot, pl.ds(t, tb, stride=ROW_TILES), :] = y[:, t * LANES:(t + 1) * LANES]
        y_copy(j, slot).start()
        return c

    lax.fori_loop(0, nb, block, 0)

    @pl.when(nb >= 2)
    def _():
        y_copy(nb - 2, nb % 2).wait()

    @pl.when(nb >= 1)
    def _():
        y_copy(nb - 1, (nb - 1) % 2).wait()


def _expert(first_blk, nblk, xs, w_up, b_up, w_down, b_down, *, tb):
    n_slots = xs.shape[0] // PACK_TILES
    wspec = lambda shape: pl.BlockSpec((None,) + shape, lambda e, fb, nb: (e, 0, 0))
    gs = pltpu.PrefetchScalarGridSpec(
        num_scalar_prefetch=2,
        grid=(N_EXPERTS,),
        in_specs=[wspec((D_MODEL, 2 * D_FF)), wspec((1, 2 * D_FF)), wspec((D_FF, D_MODEL)), wspec((1, D_MODEL)),
                  pl.BlockSpec(memory_space=pl.ANY)],
        out_specs=pl.BlockSpec(memory_space=pl.ANY),
        scratch_shapes=[pltpu.VMEM((2, tb * PACK_TILES, LANES), jnp.uint32),
                        pltpu.VMEM((2, tb * ROW_TILES, LANES), F32),
                        pltpu.VMEM((D_MODEL, 2 * D_FF), BF16),
                        pltpu.VMEM((D_FF, D_MODEL), BF16),
                        pltpu.SemaphoreType.DMA((2,)),
                        pltpu.SemaphoreType.DMA((2,))],
    )
    return pl.pallas_call(
        functools.partial(_expert_kernel, tb=tb),
        grid_spec=gs,
        out_shape=jax.ShapeDtypeStruct(((n_slots + tb) * ROW_TILES, LANES), F32),
        compiler_params=_cparams(("arbitrary",)),
        name="expert_ffn",
    )(first_blk, nblk, w_up, b_up, w_down, b_down, xs)


COMB_PIECE = 32
COMB_UNROLL = 8


def _comb_cap(tm):
    return TOP_K * tm + N_EXPERTS * (COMB_PIECE - 1)


def _combine_kernel(cstart_ref, cpieces_ref, cloc_ref, lpos_ref, gate_ref, x2_ref, gfin_ref, y_hbm, out_ref,
                    chunks, rt, sems, *, tm, n_tiles):
    i = pl.program_id(0)
    slot = i % 2
    piece_rows = COMB_PIECE * ROW_TILES

    def piece_copy(tile, e, p, to_slot):
        src = (cstart_ref[tile * N_EXPERTS + e] + p * COMB_PIECE) * ROW_TILES
        dst = (to_slot * _comb_cap(tm) + cloc_ref[tile * N_EXPERTS + e] + p * COMB_PIECE) * ROW_TILES
        return pltpu.make_async_copy(
            y_hbm.at[pl.ds(pl.multiple_of(src, ROW_TILES), piece_rows), :],
            chunks.at[pl.ds(pl.multiple_of(dst, ROW_TILES), piece_rows), :],
            sems.at[to_slot])

    def for_pieces(tile, to_slot, start):
        def per_e(e, c):
            def per_p(p, c2):
                cp = piece_copy(tile, e, p, to_slot)
                if start:
                    cp.start()
                else:
                    cp.wait()
                return c2

            lax.fori_loop(0, cpieces_ref[tile * N_EXPERTS + e], per_p, 0)
            return c

        lax.fori_loop(0, N_EXPERTS, per_e, 0)

    @pl.when(i == 0)
    def _():
        for_pieces(0, 0, True)

    @pl.when(i + 1 < n_tiles)
    def _():
        for_pieces(i + 1, 1 - slot, True)

    for_pieces(i, slot, False)

    def tok_group(tt, c):
        for u in range(COMB_UNROLL):
            t = tt * COMB_UNROLL + u
            acc = None
            for k in range(TOP_K):
                r = pl.multiple_of(lpos_ref[k * tm + t], ROW_TILES)
                term = gate_ref[k * tm + t] * chunks[pl.ds(r, ROW_TILES), :]
                acc = term if acc is None else acc + term
            rt[pl.ds(pl.multiple_of(t * ROW_TILES, ROW_TILES), ROW_TILES), :] = acc
        return c

    lax.fori_loop(0, tm // COMB_UNROLL, tok_group, 0)

    moe = jnp.concatenate([rt[pl.ds(s, tm, stride=ROW_TILES), :] for s in range(ROW_TILES)], axis=-1)
    out_ref[...] = _rms(x2_ref[...] + moe, gfin_ref[...])


def _combine(cstart, cpieces, cloc, lpos, gate, x2, g_final, y, *, tm):
    n = x2.shape[0]
    n_tiles = n // tm
    gs = pltpu.PrefetchScalarGridSpec(
        num_scalar_prefetch=3,
        grid=(n_tiles,),
        in_specs=[pl.BlockSpec((TOP_K * tm,), lambda i, a, b, c: (i,), memory_space=pltpu.SMEM),
                  pl.BlockSpec((TOP_K * tm,), lambda i, a, b, c: (i,), memory_space=pltpu.SMEM),
                  pl.BlockSpec((tm, D_MODEL), lambda i, a, b, c: (i, 0)),
                  pl.BlockSpec((1, D_MODEL), lambda i, a, b, c: (0, 0)),
                  pl.BlockSpec(memory_space=pl.ANY)],
        out_specs=pl.BlockSpec((tm, D_MODEL), lambda i, a, b, c: (i, 0)),
        scratch_shapes=[pltpu.VMEM((2 * _comb_cap(tm) * ROW_TILES, LANES), F32),
                        pltpu.VMEM((tm * ROW_TILES, LANES), F32),
                        pltpu.SemaphoreType.DMA((2,))],
    )
    return pl.pallas_call(
        functools.partial(_combine_kernel, tm=tm, n_tiles=n_tiles),
        grid_spec=gs,
        out_shape=jax.ShapeDtypeStruct((n, D_MODEL), F32),
        compiler_params=_cparams(("arbitrary",)),
        name="combine",
    )(cstart, cpieces, cloc, lpos, gate, x2, g_final, y)


def _head_pad_cols(w, per_head_in, take, place):
    kdim = w.shape[0]
    w3 = w.reshape(kdim, MLA_HEADS, per_head_in)
    out = jnp.zeros((kdim, MLA_HEADS, HEAD_PAD), w.dtype)
    for (t0, t1), p0 in zip(take, place):
        out = out.at[:, :, p0:p0 + (t1 - t0)].set(w3[:, :, t0:t1])
    return out.reshape(kdim, MLA_HEADS * HEAD_PAD)


def kernel(x, mem, positions, g_mix, w_in, g_q, w_uq, g_kv, w_ukv, w_dw, b_dw, g_conv_ln, b_conv_ln, w_out,
           g_xattn, g_mem, w_xq, w_xkv, w_xo, g_ffn, w_router, b_router, w_up, b_up, w_down, b_down, g_final):
    batch, seq, _ = x.shape
    mem_len = mem.shape[1]
    n = batch * seq
    half = MLA_ROPE // 2
    r0 = MLA_NOPE
    assert w_in.shape[0] == 1, "one trunk layer"

    row = lambda v: v.reshape(1, -1).astype(F32)

    wi = w_in[0]
    o1, o2, o3 = Q_LORA, Q_LORA + KV_LORA, Q_LORA + KV_LORA + MLA_ROPE
    kr_blk = jnp.zeros((D_MODEL, HEAD_PAD), F32).at[:, r0:r0 + MLA_ROPE].set(wi[:, o2:o3])
    kr_swp = (jnp.zeros((D_MODEL, HEAD_PAD), F32)
              .at[:, r0:r0 + half].set(wi[:, o2 + half:o3])
              .at[:, r0 + half:r0 + MLA_ROPE].set(wi[:, o2:o2 + half]))
    w_in_p = jnp.concatenate([wi[:, :o2], kr_blk, kr_swp, wi[:, o3:]], axis=1).astype(BF16)
    per_q = MLA_NOPE + MLA_ROPE
    w_uq_p = _head_pad_cols(w_uq[0], per_q, [(0, per_q)], [0]).astype(BF16)
    w_uq_s = _head_pad_cols(w_uq[0], per_q, [(MLA_NOPE + half, per_q), (MLA_NOPE, MLA_NOPE + half)],
                            [r0, r0 + half]).astype(BF16)
    per_kv = MLA_NOPE + MLA_V
    w_uk_k = _head_pad_cols(w_ukv[0], per_kv, [(0, MLA_NOPE)], [0]).astype(BF16)
    w_uk_v = (w_ukv[0].reshape(KV_LORA, MLA_HEADS, per_kv)[:, :, MLA_NOPE:]
              .reshape(KV_LORA, MLA_WIDTH).astype(BF16))
    inv = ROPE_THETA ** (-jnp.arange(0, MLA_ROPE, 2, dtype=F32) / MLA_ROPE)
    inv_c = (jnp.zeros((LANES, 1), F32).at[r0:r0 + half, 0].set(inv).at[r0 + half:r0 + MLA_ROPE, 0].set(inv))
    sgn_c = (jnp.zeros((LANES, 1), F32).at[r0:r0 + half, 0].set(-1.0).at[r0 + half:r0 + MLA_ROPE, 0].set(1.0))

    x2d = x.reshape(n, D_MODEL)
    pos_row = positions.reshape(1, n).astype(I32)

    q_t, k, v_t, y_conv = _mix_in(x2d, pos_row, inv_c, sgn_c, row(g_mix[0]), w_in_p, row(g_q[0]),
                                  jnp.transpose(w_uq_p), jnp.transpose(w_uq_s), row(g_kv[0]), w_uk_k,
                                  jnp.transpose(w_uk_v), w_dw[0].astype(F32), row(b_dw[0]),
                                  row(g_conv_ln[0]), row(b_conv_ln[0]), seq=seq)
    y_mla = _attn(q_t, k, v_t, batch=batch, seq=seq)
    kv = _mem_kv(mem.reshape(batch * mem_len, D_MODEL), row(g_mem[0]), w_xkv[0].astype(BF16),
                 batch=batch, mem_len=mem_len)

    tm_mid = min(TM_MID, seq)
    wr = jnp.zeros((D_MODEL, LANES), F32).at[:, :N_EXPERTS].set(w_router[0])
    wr_h = wr.astype(BF16)
    wr_l = (wr - wr_h.astype(F32)).astype(BF16)
    b_r = jnp.broadcast_to(b_router[0].reshape(N_EXPERTS, 1), (N_EXPERTS, LANES)).astype(F32)[:, 0:1]
    tri = (lax.broadcasted_iota(I32, (tm_mid, tm_mid), 0)
           < lax.broadcasted_iota(I32, (tm_mid, tm_mid), 1)).astype(BF16)
    wo = w_out[0].astype(BF16)
    x2, h3r, idx, gate, rank, cnt, runs_l = _mid(
        x2d, y_mla, y_conv, kv, wo[:MLA_WIDTH], wo[MLA_WIDTH:], row(g_xattn[0]), w_xq[0].astype(BF16),
        w_xo[0].astype(BF16), row(g_ffn[0]), wr_h, wr_l, b_r, tri, seq=seq, mem_len=mem_len, tm=tm_mid)

    tb = TB_EXPERT
    counts = cnt[:, 0]
    padded = (counts + tb - 1) // tb * tb
    pend = jnp.cumsum(padded)
    pstart = pend - padded
    n_slots = n * TOP_K + N_EXPERTS * tb

    runs = runs_l[:, :, 0]
    run_len = jnp.concatenate([runs[1:], counts[None, :]], axis=0) - runs
    cstart = pstart[None, :] + runs
    cpieces = (run_len + COMB_PIECE - 1) // COMB_PIECE
    cloc = jnp.cumsum(cpieces * COMB_PIECE, axis=1) - cpieces * COMB_PIECE
    adj_l = jnp.broadcast_to((cloc - runs)[:, :, None], runs_l.shape).astype(I32)

    tm_comb = min(COMB_TM, tm_mid)
    lpos = _pos(idx, rank, adj_l, tm=tm_comb)
    tiled = lambda a: a.reshape(TOP_K, n // tm_comb, tm_comb).transpose(1, 0, 2).reshape(-1)
    lpos_tok = tiled(lpos)
    flat = lambda a: a.reshape(-1).astype(I32)
    xs = _dispatch((pstart + counts).astype(I32), (padded - counts).astype(I32), flat(cstart), flat(run_len),
                   flat(cloc), lpos_tok, h3r.reshape(n, PACK_TILES, LANES), n_slots=n_slots, tm=tm_comb)
    y = _expert((pstart // tb).astype(I32), (padded // tb).astype(I32), xs.reshape(n_slots * PACK_TILES, LANES), w_up[0],
                b_up[0].reshape(N_EXPERTS, 1, 2 * D_FF), w_down[0], b_down[0].reshape(N_EXPERTS, 1, D_MODEL), tb=tb)
    out = _combine(flat(cstart), flat(cpieces), flat(cloc), lpos_tok, tiled(gate),
                   x2, row(g_final), y, tm=tm_comb)
    return out.reshape(batch, seq, D_MODEL)
```

```python
import functools

import jax
import jax.numpy as jnp
from jax import lax
from jax.experimental import pallas as pl
from jax.experimental.pallas import tpu as pltpu

F32 = jnp.float32
BF16 = jnp.bfloat16
I32 = jnp.int32

D_MODEL = 1024
MLA_HEADS = 8
MLA_NOPE = 64
MLA_ROPE = 32
MLA_V = 64
MLA_WIDTH = MLA_HEADS * MLA_V
Q_LORA = 256
KV_LORA = 128
ROPE_THETA = 10000.0
CHUNK = 64
CONV_CH = 512
CONV_WIDTH = 31
X_HEADS = 4
X_HEAD_DIM = D_MODEL // X_HEADS
N_EXPERTS = 32
TOP_K = 4
D_FF = D_MODEL
SWIGLU_LIMIT = 7.0
SWIGLU_ALPHA = 1.702
NORM_EPS = 1e-5

LANES = 128
SUBLANES = 8
ROW_TILES = D_MODEL // LANES
PACK_TILES = ROW_TILES // 2
HI_MASK = 0xFFFF0000
VMEM_LIMIT_BYTES = 56 * 1024 * 1024

HEAD_PAD = LANES
CONV_HALO = 32

TM_MIX = 512
TQ_ATTN = 1024
TM_MID = 512
TB_EXPERT = 512
MID_SPLIT = 2
COMB_TM = 512


def _rms(x, g):
    return x * lax.rsqrt(jnp.mean(x * x, axis=-1, keepdims=True) + NORM_EPS) * g


def _cparams(sem):
    return pltpu.CompilerParams(dimension_semantics=sem, vmem_limit_bytes=VMEM_LIMIT_BYTES)


def _mix_in_kernel(x_ref, pos_ref, inv_ref, sgn_ref, gmix_ref, win_ref, gq_ref, wuq_ref, wuqs_ref,
                   gkv_ref, wukk_ref, wukv_ref, wdw_ref, bdw_ref, gln_ref, bln_ref,
                   q_ref, k_ref, v_ref, yc_ref, zbuf, zsh, *, tiles_per_batch, tm):
    i = pl.program_id(0)
    x = x_ref[...]
    h = _rms(x, gmix_ref[...]).astype(BF16)
    u = jnp.dot(h, win_ref[...], preferred_element_type=F32)
    cq = u[:, 0:256]
    ckv = u[:, 256:384]
    kr = u[:, 384:512]
    krs = u[:, 512:640]
    a = u[:, 640:1152]
    gate = u[:, 1152:1664]

    r0, r1 = MLA_NOPE, MLA_NOPE + MLA_ROPE
    ang_r = inv_ref[r0:r1, :] * pos_ref[...].astype(F32)
    cos_t = jnp.concatenate([jnp.ones((r0, tm), F32), jnp.cos(ang_r), jnp.ones((HEAD_PAD - r1, tm), F32)], axis=0)
    sin_t = jnp.concatenate([jnp.zeros((r0, tm), F32), jnp.sin(ang_r) * sgn_ref[r0:r1, :],
                             jnp.zeros((HEAD_PAD - r1, tm), F32)], axis=0)
    cosb = jnp.transpose(cos_t)
    sinb = jnp.transpose(sin_t)

    nt = (((1,), (1,)), ((), ()))
    cqn = _rms(cq, gq_ref[...]).astype(BF16)
    qm = lax.dot_general(wuq_ref[...], cqn, nt, preferred_element_type=F32)
    qs = lax.dot_general(wuqs_ref[...], cqn, nt, preferred_element_type=F32)
    for hd in range(MLA_HEADS):
        sl = slice(hd * HEAD_PAD, (hd + 1) * HEAD_PAD)
        q_ref[sl, :] = (qm[sl, :] * cos_t + qs[sl, :] * sin_t).astype(BF16)

    ckvn = _rms(ckv, gkv_ref[...]).astype(BF16)
    kk = jnp.dot(ckvn, wukk_ref[...], preferred_element_type=F32)
    v_ref[...] = lax.dot_general(wukv_ref[...], ckvn, nt, preferred_element_type=F32).astype(BF16)
    krot = kr * cosb + krs * sinb
    for hd in range(MLA_HEADS):
        sl = slice(hd * HEAD_PAD, (hd + 1) * HEAD_PAD)
        k_ref[:, sl] = (kk[:, sl] + krot).astype(BF16)

    z = a * jax.nn.sigmoid(gate)

    @pl.when(i % tiles_per_batch == 0)
    def _():
        zbuf[0:CONV_HALO, :] = jnp.zeros((CONV_HALO, CONV_CH), F32)

    zbuf[CONV_HALO:CONV_HALO + tm, :] = z
    off = CONV_HALO - (CONV_WIDTH - 1)
    rows = 32
    span = tm + CONV_HALO - SUBLANES
    step = 64
    for r in range(1, SUBLANES):
        for c0 in range(0, span, step):
            cl = min(step, span - c0)
            zsh[r - 1, c0:c0 + cl, :] = zbuf[c0 + r:c0 + r + cl, :]
    for r0 in range(0, tm, rows):
        acc = jnp.zeros((rows, CONV_CH), F32) + bdw_ref[...]
        for j in range(CONV_WIDTH):
            a8, ph = divmod(j + off, SUBLANES)
            lo_row = r0 + a8 * SUBLANES
            tap = zbuf[lo_row:lo_row + rows, :] if ph == 0 else zsh[ph - 1, lo_row:lo_row + rows, :]
            acc = acc + wdw_ref[j:j + 1, :] * tap
        mu = jnp.mean(acc, axis=-1, keepdims=True)
        cen = acc - mu
        var = jnp.mean(cen * cen, axis=-1, keepdims=True)
        y = cen * lax.rsqrt(var + NORM_EPS) * gln_ref[...] + bln_ref[...]
        yc_ref[r0:r0 + rows, :] = (y * jax.nn.sigmoid(y)).astype(BF16)
    zbuf[0:CONV_HALO, :] = zbuf[tm:tm + CONV_HALO, :]


def _mix_in(x2d, pos_row, inv_c, sgn_c, g_mix, w_in_p, g_q, w_uq_t, w_uq_st, g_kv, w_uk_k, w_uk_vt,
            w_dw, b_dw, g_ln, b_ln, *, seq):
    n = x2d.shape[0]
    tm = min(TM_MIX, seq)
    full = lambda a: pl.BlockSpec(a.shape, lambda i: (0,) * a.ndim)
    consts = [inv_c, sgn_c, g_mix, w_in_p, g_q, w_uq_t, w_uq_st, g_kv, w_uk_k, w_uk_vt, w_dw, b_dw, g_ln, b_ln]
    return pl.pallas_call(
        functools.partial(_mix_in_kernel, tiles_per_batch=seq // tm, tm=tm),
        grid=(n // tm,),
        in_specs=[pl.BlockSpec((tm, D_MODEL), lambda i: (i, 0)),
                  pl.BlockSpec((1, tm), lambda i: (0, i))] + [full(a) for a in consts],
        out_specs=[pl.BlockSpec((MLA_HEADS * HEAD_PAD, tm), lambda i: (0, i)),
                   pl.BlockSpec((tm, MLA_HEADS * HEAD_PAD), lambda i: (i, 0)),
                   pl.BlockSpec((MLA_WIDTH, tm), lambda i: (0, i)),
                   pl.BlockSpec((tm, CONV_CH), lambda i: (i, 0))],
        out_shape=[jax.ShapeDtypeStruct((MLA_HEADS * HEAD_PAD, n), BF16),
                   jax.ShapeDtypeStruct((n, MLA_HEADS * HEAD_PAD), BF16),
                   jax.ShapeDtypeStruct((MLA_WIDTH, n), BF16),
                   jax.ShapeDtypeStruct((n, CONV_CH), BF16)],
        scratch_shapes=[pltpu.VMEM((CONV_HALO + tm + SUBLANES, CONV_CH), F32),
                        pltpu.VMEM((SUBLANES - 1, CONV_HALO + tm, CONV_CH), F32)],
        compiler_params=_cparams(("arbitrary",)),
        name="mix_in",
    )(x2d, pos_row, *consts)


ATTN_GROUP = 2


TK_ATTN = 1024


def _attn_kernel(q_ref, k_ref, v_ref, o_ref, *s_scr, tq):
    i = pl.program_id(1)
    tk = min(TK_ATTN, tq)
    per_q = tq // tk
    c2 = (MLA_NOPE + MLA_ROPE) ** -0.5 * 1.4426950408889634
    key_c = lax.broadcasted_iota(I32, (tk, tq), 0) // CHUNK
    qry_c = lax.broadcasted_iota(I32, (tk, tq), 1) // CHUNK

    for g in range(MLA_HEADS // ATTN_GROUP):
        heads = list(range(g * ATTN_GROUP, (g + 1) * ATTN_GROUP))
        qts = [q_ref[hd * HEAD_PAD:(hd + 1) * HEAD_PAD, :] for hd in heads]

        def step(j, carry, diag):
            start = pl.multiple_of(j * tk, tk)
            new = []
            for n_, hd in enumerate(heads):
                m, l, acc = carry[n_]
                kj = k_ref[pl.ds(start, tk), hd * HEAD_PAD:(hd + 1) * HEAD_PAD]
                vj = v_ref[hd * MLA_V:(hd + 1) * MLA_V, pl.ds(start, tk)]
                s = jnp.dot(kj, qts[n_], preferred_element_type=F32)
                if diag is not None:
                    s = jnp.where(key_c + diag * (tk // CHUNK) <= qry_c, s, -jnp.inf)
                m_new = jnp.maximum(m, jnp.max(s, axis=0, keepdims=True))
                s_scr[n_][...] = s
                s = s_scr[n_][...]
                alpha = jnp.exp2((m - m_new) * c2)
                p = jnp.exp2((s - m_new) * c2)
                l = alpha * l + jnp.sum(p, axis=0, keepdims=True)
                acc = alpha * acc + jnp.dot(vj, p.astype(BF16), preferred_element_type=F32)
                new.append((m_new, l, acc))
            return tuple(new)

        init = tuple((jnp.full((1, tq), -jnp.inf, F32), jnp.zeros((1, tq), F32), jnp.zeros((MLA_V, tq), F32))
                     for _ in heads)
        carry = lax.fori_loop(0, i * per_q, functools.partial(step, diag=None), init)
        for d in range(per_q):
            carry = step(i * per_q + d, carry, d)
        for n_, hd in enumerate(heads):
            _, l, acc = carry[n_]
            o_ref[:, hd * MLA_V:(hd + 1) * MLA_V] = jnp.transpose(acc / l).astype(BF16)


def _attn(q_t, k, v_t, *, batch, seq):
    tq = min(TQ_ATTN, seq)
    nq = seq // tq
    return pl.pallas_call(
        functools.partial(_attn_kernel, tq=tq),
        grid=(batch, nq),
        in_specs=[pl.BlockSpec((MLA_HEADS * HEAD_PAD, tq), lambda b, i: (0, b * nq + i)),
                  pl.BlockSpec((seq, MLA_HEADS * HEAD_PAD), lambda b, i: (b, 0)),
                  pl.BlockSpec((MLA_WIDTH, seq), lambda b, i: (0, b))],
        out_specs=pl.BlockSpec((tq, MLA_WIDTH), lambda b, i: (b * nq + i, 0)),
        out_shape=jax.ShapeDtypeStruct((batch * seq, MLA_WIDTH), BF16),
        scratch_shapes=[pltpu.VMEM((min(TK_ATTN, tq), tq), F32) for _ in range(ATTN_GROUP)],
        compiler_params=_cparams(("arbitrary", "arbitrary")),
        name="mla_attn",
    )(q_t, k, v_t)


def _mem_kv_kernel(mem_ref, g_ref, w_ref, kv_ref):
    mn = _rms(mem_ref[...], g_ref[...]).astype(BF16)
    kv_ref[...] = jnp.dot(mn, w_ref[...], preferred_element_type=F32).astype(BF16)


def _mem_kv(mem2d, g_mem, w_xkv, *, batch, mem_len):
    return pl.pallas_call(
        _mem_kv_kernel,
        grid=(batch,),
        in_specs=[pl.BlockSpec((mem_len, D_MODEL), lambda b: (b, 0)),
                  pl.BlockSpec((1, D_MODEL), lambda b: (0, 0)),
                  pl.BlockSpec((D_MODEL, 2 * D_MODEL), lambda b: (0, 0))],
        out_specs=pl.BlockSpec((mem_len, 2 * D_MODEL), lambda b: (b, 0)),
        out_shape=jax.ShapeDtypeStruct((batch * mem_len, 2 * D_MODEL), BF16),
        compiler_params=_cparams(("arbitrary",)),
        name="mem_kv",
    )(mem2d, g_mem, w_xkv)


def _mid_kernel(x_ref, ya_ref, yc_ref, kv_ref, woa_ref, wob_ref, gx_ref, wxq_ref, wxo_ref, gf_ref,
                wrh_ref, wrl_ref, br_ref, tri_ref,
                x2_ref, h3_ref, idx_ref, gate_ref, rank_ref, cnt_ref, runs_ref, run_ref, *, tm):
    i = pl.program_id(0)

    @pl.when(i == 0)
    def _():
        run_ref[...] = jnp.zeros_like(run_ref)

    nt = (((1,), (1,)), ((), ()))
    half_d = D_MODEL // 2
    rows = tm // MID_SPLIT
    lgs = []
    for part in range(MID_SPLIT):
        rs = slice(part * rows, (part + 1) * rows)
        x1 = (x_ref[rs, :]
              + jnp.dot(ya_ref[rs, :], woa_ref[...], preferred_element_type=F32)
              + jnp.dot(yc_ref[rs, :], wob_ref[...], preferred_element_type=F32))
        h = _rms(x1, gx_ref[...]).astype(BF16)
        q = jnp.dot(h, wxq_ref[...], preferred_element_type=F32).astype(BF16)
        os = []
        for hd in range(X_HEADS):
            sl = slice(hd * X_HEAD_DIM, (hd + 1) * X_HEAD_DIM)
            vsl = slice(D_MODEL + hd * X_HEAD_DIM, D_MODEL + (hd + 1) * X_HEAD_DIM)
            s = lax.dot_general(q[:, sl], kv_ref[:, sl], nt, preferred_element_type=F32) * (X_HEAD_DIM ** -0.5)
            e = jnp.exp(s - jnp.max(s, axis=-1, keepdims=True))
            p = (e / jnp.sum(e, axis=-1, keepdims=True)).astype(BF16)
            os.append(jnp.dot(p, kv_ref[:, vsl], preferred_element_type=F32).astype(BF16))
        o = jnp.concatenate(os, axis=-1)
        x2 = x1 + jnp.dot(o, wxo_ref[...], preferred_element_type=F32)
        x2_ref[rs, :] = x2
        h3 = _rms(x2, gf_ref[...])
        hh = h3.astype(BF16)
        hf = hh.astype(F32)
        word = (lax.shift_right_logical(pltpu.bitcast(hf[:, :half_d], jnp.uint32), jnp.uint32(16))
                | (pltpu.bitcast(hf[:, half_d:], jnp.uint32) & jnp.uint32(HI_MASK)))
        for t in range(PACK_TILES):
            h3_ref[pl.ds(part * rows * PACK_TILES + t, rows, stride=PACK_TILES), :] = word[:, t * LANES:(t + 1) * LANES]

        hl = (h3 - hf).astype(BF16)
        lgs.append(jnp.dot(hh, wrh_ref[...], preferred_element_type=F32)
                   + (jnp.dot(hh, wrl_ref[...], preferred_element_type=F32)
                      + jnp.dot(hl, wrh_ref[...], preferred_element_type=F32)))
    lg = jnp.concatenate(lgs, axis=0)
    lgt = jnp.transpose(lg)[0:N_EXPERTS, :] + br_ref[...]
    eid = lax.broadcasted_iota(I32, (N_EXPERTS, tm), 0)
    vals, hots = [], []
    cur = lgt
    for k in range(TOP_K):
        mx = jnp.max(cur, axis=0, keepdims=True)
        ik = jnp.min(jnp.where(cur == mx, eid, N_EXPERTS), axis=0, keepdims=True)
        hot = eid == ik
        cur = jnp.where(hot, -jnp.inf, cur)
        idx_ref[k:k + 1, :] = ik
        vals.append(mx)
        hots.append(hot)
    es = [jnp.exp(v - vals[0]) for v in vals]
    den = es[0] + es[1] + es[2] + es[3]
    for k in range(TOP_K):
        gate_ref[k:k + 1, :] = es[k] / den
    cnt = jnp.zeros((N_EXPERTS, tm), F32)
    for k in range(TOP_K):
        cnt = cnt + hots[k].astype(F32)
    run = run_ref[:, 0:1]
    ctm = min(COMB_TM, tm)
    for j in range(tm // ctm):
        before = run if j == 0 else run + jnp.sum(cnt[:, :j * ctm], axis=1, keepdims=True)
        runs_ref[j] = jnp.broadcast_to(before, (N_EXPERTS, LANES)).astype(I32)
    tot = jnp.dot(cnt.astype(BF16), tri_ref[...], preferred_element_type=F32) + run
    for k in range(TOP_K):
        rank_ref[k:k + 1, :] = jnp.sum(jnp.where(hots[k], tot, 0.0), axis=0, keepdims=True).astype(I32)
    new_run = run + jnp.sum(cnt, axis=1, keepdims=True)
    run_ref[...] = jnp.broadcast_to(new_run, run_ref.shape)
    cnt_ref[...] = jnp.broadcast_to(new_run, cnt_ref.shape).astype(I32)


def _mid(x2d, ya, yc, kv, w_oa, w_ob, g_x, w_xq, w_xo, g_f, wr_h, wr_l, b_r, tri, *, seq, mem_len, tm):
    n = x2d.shape[0]
    tpb = seq // tm
    full = lambda a: pl.BlockSpec(a.shape, lambda i: (0,) * a.ndim)
    consts_a = [w_oa, w_ob, g_x, w_xq, w_xo, g_f, wr_h, wr_l, b_r, tri]
    return pl.pallas_call(
        functools.partial(_mid_kernel, tm=tm),
        grid=(n // tm,),
        in_specs=[pl.BlockSpec((tm, D_MODEL), lambda i: (i, 0)),
                  pl.BlockSpec((tm, MLA_WIDTH), lambda i: (i, 0)),
                  pl.BlockSpec((tm, CONV_CH), lambda i: (i, 0)),
                  pl.BlockSpec((mem_len, 2 * D_MODEL), lambda i: (i // tpb, 0))] + [full(a) for a in consts_a],
        out_specs=[pl.BlockSpec((tm, D_MODEL), lambda i: (i, 0)),
                   pl.BlockSpec((tm * PACK_TILES, LANES), lambda i: (i, 0)),
                   pl.BlockSpec((TOP_K, tm), lambda i: (0, i)),
                   pl.BlockSpec((TOP_K, tm), lambda i: (0, i)),
                   pl.BlockSpec((TOP_K, tm), lambda i: (0, i)),
                   pl.BlockSpec((N_EXPERTS, LANES), lambda i: (0, 0)),
                   pl.BlockSpec((tm // min(COMB_TM, tm), N_EXPERTS, LANES), lambda i: (i, 0, 0))],
        out_shape=[jax.ShapeDtypeStruct((n, D_MODEL), F32),
                   jax.ShapeDtypeStruct((n * PACK_TILES, LANES), jnp.uint32),
                   jax.ShapeDtypeStruct((TOP_K, n), I32),
                   jax.ShapeDtypeStruct((TOP_K, n), F32),
                   jax.ShapeDtypeStruct((TOP_K, n), I32),
                   jax.ShapeDtypeStruct((N_EXPERTS, LANES), I32),
                   jax.ShapeDtypeStruct((n // min(COMB_TM, tm), N_EXPERTS, LANES), I32)],
        scratch_shapes=[pltpu.VMEM((N_EXPERTS, LANES), F32)],
        compiler_params=_cparams(("arbitrary",)),
        name="mid",
    )(x2d, ya, yc, kv, *consts_a)


def _pos_kernel(idx_ref, rank_ref, adj_ref, lpos_ref, *, tm, group):
    i = pl.program_id(0)
    eid = lax.broadcasted_iota(I32, (N_EXPERTS, tm), 0)
    for j in range(group):
        half = ((i * group + j) % 2) * _comb_cap(tm)
        cols = slice(j * tm, (j + 1) * tm)
        adj = adj_ref[j][:, 0:1] + half
        for k in range(TOP_K):
            hot = eid == idx_ref[k:k + 1, cols]
            lpos_ref[k:k + 1, cols] = (jnp.sum(jnp.where(hot, adj, 0), axis=0, keepdims=True)
                                       + rank_ref[k:k + 1, cols]) * ROW_TILES


def _pos(idx, rank, adj_l, *, tm):
    n = idx.shape[1]
    n_tiles = n // tm
    group = 4 if n_tiles % 4 == 0 else 1
    return pl.pallas_call(
        functools.partial(_pos_kernel, tm=tm, group=group),
        grid=(n_tiles // group,),
        in_specs=[pl.BlockSpec((TOP_K, tm * group), lambda i: (0, i)),
                  pl.BlockSpec((TOP_K, tm * group), lambda i: (0, i)),
                  pl.BlockSpec((group, N_EXPERTS, LANES), lambda i: (i, 0, 0))],
        out_specs=pl.BlockSpec((TOP_K, tm * group), lambda i: (0, i)),
        out_shape=jax.ShapeDtypeStruct((TOP_K, n), I32),
        compiler_params=_cparams(("arbitrary",)),
        name="slot_pos",
    )(idx, rank, adj_l)


RUN_BITS = tuple(1 << b for b in range(9, -1, -1))


def _pow2_runs(length, fn):
    off = jnp.int32(0)
    for bit in RUN_BITS:
        take = (length & bit) != 0

        @pl.when(take)
        def _(off=off, bit=bit):
            fn(off, bit)

        off = off + jnp.where(take, bit, 0)


def _dispatch_kernel(pad_start_ref, pad_cnt_ref, cstart_ref, clen_ref, cloc_ref, lpos_ref, h3_ref, xs_hbm,
                     cbuf, sems, zsem, *, tm, n_tiles):
    i = pl.program_id(0)
    slot = i % 2
    cap = _comb_cap(tm)
    rows_per_tile = TOP_K * tm

    @pl.when(i == 0)
    def _():
        zrows = RUN_BITS[0]
        cbuf[pl.ds(cap, zrows)] = jnp.zeros((zrows, PACK_TILES, LANES), jnp.uint32)

        def pad_pass(start):
            def per_expert(e, c):
                def one(off, bit):
                    cp = pltpu.make_async_copy(cbuf.at[pl.ds(cap, bit)],
                                               xs_hbm.at[pl.ds(pad_start_ref[e] + off, bit)], zsem)
                    cp.start() if start else cp.wait()

                _pow2_runs(pad_cnt_ref[e], one)
                return c

            lax.fori_loop(0, N_EXPERTS, per_expert, 0)

        pad_pass(True)
        pad_pass(False)

    def wait_tile(s):
        pltpu.make_async_copy(cbuf.at[pl.ds(s * cap, rows_per_tile)], xs_hbm.at[pl.ds(0, rows_per_tile)],
                              sems.at[s]).wait()

    @pl.when(i >= 2)
    def _():
        wait_tile(slot)

    def tok_group(tt, c):
        for u in range(COMB_UNROLL):
            t = tt * COMB_UNROLL + u
            row = h3_ref[t]
            for k in range(TOP_K):
                cbuf[lax.shift_right_logical(lpos_ref[k * tm + t], 3)] = row
        return c

    lax.fori_loop(0, tm // COMB_UNROLL, tok_group, 0)

    def per_run(e, c):
        src0 = slot * cap + cloc_ref[i * N_EXPERTS + e]
        dst0 = cstart_ref[i * N_EXPERTS + e]

        def one(off, bit):
            pltpu.make_async_copy(cbuf.at[pl.ds(src0 + off, bit)], xs_hbm.at[pl.ds(dst0 + off, bit)],
                                  sems.at[slot]).start()

        _pow2_runs(clen_ref[i * N_EXPERTS + e], one)
        return c

    lax.fori_loop(0, N_EXPERTS, per_run, 0)

    @pl.when(i == n_tiles - 1)
    def _():
        wait_tile(slot)
        if n_tiles >= 2:
            wait_tile(1 - slot)


def _dispatch(pad_start, pad_cnt, cstart, clen, cloc, lpos_tok, h3r, *, n_slots, tm):
    n = h3r.shape[0]
    n_tiles = n // tm
    gs = pltpu.PrefetchScalarGridSpec(
        num_scalar_prefetch=5,
        grid=(n_tiles,),
        in_specs=[pl.BlockSpec((TOP_K * tm,), lambda i, *_: (i,), memory_space=pltpu.SMEM),
                  pl.BlockSpec((tm, PACK_TILES, LANES), lambda i, *_: (i, 0, 0))],
        out_specs=pl.BlockSpec(memory_space=pl.ANY),
        scratch_shapes=[pltpu.VMEM((2 * _comb_cap(tm), PACK_TILES, LANES), jnp.uint32),
                        pltpu.SemaphoreType.DMA((2,)),
                        pltpu.SemaphoreType.DMA(())],
    )
    return pl.pallas_call(
        functools.partial(_dispatch_kernel, tm=tm, n_tiles=n_tiles),
        grid_spec=gs,
        out_shape=jax.ShapeDtypeStruct((n_slots, PACK_TILES, LANES), jnp.uint32),
        compiler_params=_cparams(("arbitrary",)),
        name="dispatch",
    )(pad_start, pad_cnt, cstart, clen, cloc, lpos_tok, h3r)


def _expert_kernel(blk_e_ref, nused_ref, x_ref, wu_ref, bu_ref, wd_ref, bd_ref, y_ref, wu_bf, wd_bf, *, tb):
    b = pl.program_id(0)

    @pl.when(b < nused_ref[0])
    def _():
        prev = blk_e_ref[jnp.maximum(b - 1, 0)]

        @pl.when(jnp.logical_or(b == 0, blk_e_ref[b] != prev))
        def _():
            wu_bf[...] = wu_ref[...].astype(BF16)
            wd_bf[...] = wd_ref[...].astype(BF16)

        words = [x_ref[pl.ds(t, tb, stride=PACK_TILES), :] for t in range(PACK_TILES)]
        lo = [pltpu.bitcast(lax.shift_left(w, jnp.uint32(16)), F32) for w in words]
        hi = [pltpu.bitcast(w & jnp.uint32(HI_MASK), F32) for w in words]
        x = jnp.concatenate(lo + hi, axis=-1).astype(BF16)
        up = jnp.dot(x, wu_bf[...], preferred_element_type=F32) + bu_ref[...]
        glu = jnp.minimum(up[:, :D_FF], SWIGLU_LIMIT)
        lin = jnp.clip(up[:, D_FF:], -SWIGLU_LIMIT, SWIGLU_LIMIT)
        act = (glu * jax.nn.sigmoid(SWIGLU_ALPHA * glu) * (lin + 1.0)).astype(BF16)
        y = jnp.dot(act, wd_bf[...], preferred_element_type=F32) + bd_ref[...]
        for t in range(ROW_TILES):
            y_ref[pl.ds(t, tb, stride=ROW_TILES), :] = y[:, t * LANES:(t + 1) * LANES]


def _expert(blk_e, nused, xs, w_up, b_up, w_down, b_down, *, tb):
    n_slots = xs.shape[0] // PACK_TILES
    nb = n_slots // tb
    blk = lambda b, be, nu: jnp.minimum(b, nu[0] - 1)
    gs = pltpu.PrefetchScalarGridSpec(
        num_scalar_prefetch=2,
        grid=(nb,),
        in_specs=[pl.BlockSpec((tb * PACK_TILES, LANES), lambda b, be, nu: (blk(b, be, nu), 0)),
                  pl.BlockSpec((None, D_MODEL, 2 * D_FF), lambda b, be, nu: (be[blk(b, be, nu)], 0, 0)),
                  pl.BlockSpec((None, 1, 2 * D_FF), lambda b, be, nu: (be[blk(b, be, nu)], 0, 0)),
                  pl.BlockSpec((None, D_FF, D_MODEL), lambda b, be, nu: (be[blk(b, be, nu)], 0, 0)),
                  pl.BlockSpec((None, 1, D_MODEL), lambda b, be, nu: (be[blk(b, be, nu)], 0, 0))],
        out_specs=pl.BlockSpec((tb * ROW_TILES, LANES), lambda b, be, nu: (blk(b, be, nu), 0)),
        scratch_shapes=[pltpu.VMEM((D_MODEL, 2 * D_FF), BF16),
                        pltpu.VMEM((D_FF, D_MODEL), BF16)],
    )
    return pl.pallas_call(
        functools.partial(_expert_kernel, tb=tb),
        grid_spec=gs,
        out_shape=jax.ShapeDtypeStruct(((n_slots + tb) * ROW_TILES, LANES), F32),
        compiler_params=_cparams(("arbitrary",)),
        name="expert_ffn",
    )(blk_e, nused, xs, w_up, b_up, w_down, b_down)


COMB_PIECE = 16
COMB_UNROLL = 8


def _comb_cap(tm):
    return TOP_K * tm + N_EXPERTS * (COMB_PIECE - 1)


def _combine_kernel(cstart_ref, cpieces_ref, cloc_ref, lpos_ref, gate_ref, x2_ref, gfin_ref, y_hbm, out_ref,
                    chunks, rt, sems, *, tm, n_tiles):
    i = pl.program_id(0)
    slot = i % 2
    piece_rows = COMB_PIECE * ROW_TILES

    def piece_copy(tile, e, p, to_slot):
        src = (cstart_ref[tile * N_EXPERTS + e] + p * COMB_PIECE) * ROW_TILES
        dst = (to_slot * _comb_cap(tm) + cloc_ref[tile * N_EXPERTS + e] + p * COMB_PIECE) * ROW_TILES
        return pltpu.make_async_copy(
            y_hbm.at[pl.ds(pl.multiple_of(src, ROW_TILES), piece_rows), :],
            chunks.at[pl.ds(pl.multiple_of(dst, ROW_TILES), piece_rows), :],
            sems.at[to_slot])

    def for_pieces(tile, to_slot, start):
        def per_e(e, c):
            def per_p(p, c2):
                cp = piece_copy(tile, e, p, to_slot)
                if start:
                    cp.start()
                else:
                    cp.wait()
                return c2

            lax.fori_loop(0, cpieces_ref[tile * N_EXPERTS + e], per_p, 0)
            return c

        lax.fori_loop(0, N_EXPERTS, per_e, 0)

    @pl.when(i == 0)
    def _():
        for_pieces(0, 0, True)

    @pl.when(i + 1 < n_tiles)
    def _():
        for_pieces(i + 1, 1 - slot, True)

    for_pieces(i, slot, False)

    def tok_group(tt, c):
        for u in range(COMB_UNROLL):
            t = tt * COMB_UNROLL + u
            acc = None
            for k in range(TOP_K):
                r = pl.multiple_of(lpos_ref[k * tm + t], ROW_TILES)
                term = gate_ref[k * tm + t] * chunks[pl.ds(r, ROW_TILES), :]
                acc = term if acc is None else acc + term
            rt[pl.ds(pl.multiple_of(t * ROW_TILES, ROW_TILES), ROW_TILES), :] = acc
        return c

    lax.fori_loop(0, tm // COMB_UNROLL, tok_group, 0)

    moe = jnp.concatenate([rt[pl.ds(s, tm, stride=ROW_TILES), :] for s in range(ROW_TILES)], axis=-1)
    out_ref[...] = _rms(x2_ref[...] + moe, gfin_ref[...])


def _combine(cstart, cpieces, cloc, lpos, gate, x2, g_final, y, *, tm):
    n = x2.shape[0]
    n_tiles = n // tm
    gs = pltpu.PrefetchScalarGridSpec(
        num_scalar_prefetch=3,
        grid=(n_tiles,),
        in_specs=[pl.BlockSpec((TOP_K * tm,), lambda i, a, b, c: (i,), memory_space=pltpu.SMEM),
                  pl.BlockSpec((TOP_K * tm,), lambda i, a, b, c: (i,), memory_space=pltpu.SMEM),
                  pl.BlockSpec((tm, D_MODEL), lambda i, a, b, c: (i, 0)),
                  pl.BlockSpec((1, D_MODEL), lambda i, a, b, c: (0, 0)),
                  pl.BlockSpec(memory_space=pl.ANY)],
        out_specs=pl.BlockSpec((tm, D_MODEL), lambda i, a, b, c: (i, 0)),
        scratch_shapes=[pltpu.VMEM((2 * _comb_cap(tm) * ROW_TILES, LANES), F32),
                        pltpu.VMEM((tm * ROW_TILES, LANES), F32),
                        pltpu.SemaphoreType.DMA((2,))],
    )
    return pl.pallas_call(
        functools.partial(_combine_kernel, tm=tm, n_tiles=n_tiles),
        grid_spec=gs,
        out_shape=jax.ShapeDtypeStruct((n, D_MODEL), F32),
        compiler_params=_cparams(("arbitrary",)),
        name="combine",
    )(cstart, cpieces, cloc, lpos, gate, x2, g_final, y)


def _head_pad_cols(w, per_head_in, take, place):
    kdim = w.shape[0]
    w3 = w.reshape(kdim, MLA_HEADS, per_head_in)
    out = jnp.zeros((kdim, MLA_HEADS, HEAD_PAD), w.dtype)
    for (t0, t1), p0 in zip(take, place):
        out = out.at[:, :, p0:p0 + (t1 - t0)].set(w3[:, :, t0:t1])
    return out.reshape(kdim, MLA_HEADS * HEAD_PAD)


def kernel(x, mem, positions, g_mix, w_in, g_q, w_uq, g_kv, w_ukv, w_dw, b_dw, g_conv_ln, b_conv_ln, w_out,
           g_xattn, g_mem, w_xq, w_xkv, w_xo, g_ffn, w_router, b_router, w_up, b_up, w_down, b_down, g_final):
    batch, seq, _ = x.shape
    mem_len = mem.shape[1]
    n = batch * seq
    half = MLA_ROPE // 2
    r0 = MLA_NOPE
    assert w_in.shape[0] == 1, "one trunk layer"

    row = lambda v: v.reshape(1, -1).astype(F32)

    wi = w_in[0]
    o1, o2, o3 = Q_LORA, Q_LORA + KV_LORA, Q_LORA + KV_LORA + MLA_ROPE
    kr_blk = jnp.zeros((D_MODEL, HEAD_PAD), F32).at[:, r0:r0 + MLA_ROPE].set(wi[:, o2:o3])
    kr_swp = (jnp.zeros((D_MODEL, HEAD_PAD), F32)
              .at[:, r0:r0 + half].set(wi[:, o2 + half:o3])
              .at[:, r0 + half:r0 + MLA_ROPE].set(wi[:, o2:o2 + half]))
    w_in_p = jnp.concatenate([wi[:, :o2], kr_blk, kr_swp, wi[:, o3:]], axis=1).astype(BF16)
    per_q = MLA_NOPE + MLA_ROPE
    w_uq_p = _head_pad_cols(w_uq[0], per_q, [(0, per_q)], [0]).astype(BF16)
    w_uq_s = _head_pad_cols(w_uq[0], per_q, [(MLA_NOPE + half, per_q), (MLA_NOPE, MLA_NOPE + half)],
                            [r0, r0 + half]).astype(BF16)
    per_kv = MLA_NOPE + MLA_V
    w_uk_k = _head_pad_cols(w_ukv[0], per_kv, [(0, MLA_NOPE)], [0]).astype(BF16)
    w_uk_v = (w_ukv[0].reshape(KV_LORA, MLA_HEADS, per_kv)[:, :, MLA_NOPE:]
              .reshape(KV_LORA, MLA_WIDTH).astype(BF16))
    inv = ROPE_THETA ** (-jnp.arange(0, MLA_ROPE, 2, dtype=F32) / MLA_ROPE)
    inv_c = (jnp.zeros((LANES, 1), F32).at[r0:r0 + half, 0].set(inv).at[r0 + half:r0 + MLA_ROPE, 0].set(inv))
    sgn_c = (jnp.zeros((LANES, 1), F32).at[r0:r0 + half, 0].set(-1.0).at[r0 + half:r0 + MLA_ROPE, 0].set(1.0))

    x2d = x.reshape(n, D_MODEL)
    pos_row = positions.reshape(1, n).astype(I32)

    q_t, k, v_t, y_conv = _mix_in(x2d, pos_row, inv_c, sgn_c, row(g_mix[0]), w_in_p, row(g_q[0]),
                                  jnp.transpose(w_uq_p), jnp.transpose(w_uq_s), row(g_kv[0]), w_uk_k,
                                  jnp.transpose(w_uk_v), w_dw[0].astype(F32), row(b_dw[0]),
                                  row(g_conv_ln[0]), row(b_conv_ln[0]), seq=seq)
    y_mla = _attn(q_t, k, v_t, batch=batch, seq=seq)
    kv = _mem_kv(mem.reshape(batch * mem_len, D_MODEL), row(g_mem[0]), w_xkv[0].astype(BF16),
                 batch=batch, mem_len=mem_len)

    tm_mid = min(TM_MID, seq)
    wr = jnp.zeros((D_MODEL, LANES), F32).at[:, :N_EXPERTS].set(w_router[0])
    wr_h = wr.astype(BF16)
    wr_l = (wr - wr_h.astype(F32)).astype(BF16)
    b_r = jnp.broadcast_to(b_router[0].reshape(N_EXPERTS, 1), (N_EXPERTS, LANES)).astype(F32)[:, 0:1]
    tri = (lax.broadcasted_iota(I32, (tm_mid, tm_mid), 0)
           < lax.broadcasted_iota(I32, (tm_mid, tm_mid), 1)).astype(BF16)
    wo = w_out[0].astype(BF16)
    x2, h3r, idx, gate, rank, cnt, runs_l = _mid(
        x2d, y_mla, y_conv, kv, wo[:MLA_WIDTH], wo[MLA_WIDTH:], row(g_xattn[0]), w_xq[0].astype(BF16),
        w_xo[0].astype(BF16), row(g_ffn[0]), wr_h, wr_l, b_r, tri, seq=seq, mem_len=mem_len, tm=tm_mid)

    tb = TB_EXPERT
    counts = cnt[:, 0]
    padded = (counts + tb - 1) // tb * tb
    pend = jnp.cumsum(padded)
    pstart = pend - padded
    n_slots = n * TOP_K + N_EXPERTS * tb
    nb = n_slots // tb
    blk_first = jnp.arange(nb, dtype=I32) * tb
    blk_e = jnp.minimum(jnp.sum((pend[None, :] <= blk_first[:, None]).astype(I32), axis=1),
                        N_EXPERTS - 1).astype(I32)
    nused = (pend[-1:] // tb).astype(I32)

    runs = runs_l[:, :, 0]
    run_len = jnp.concatenate([runs[1:], counts[None, :]], axis=0) - runs
    cstart = pstart[None, :] + runs
    cpieces = (run_len + COMB_PIECE - 1) // COMB_PIECE
    cloc = jnp.cumsum(cpieces * COMB_PIECE, axis=1) - cpieces * COMB_PIECE
    adj_l = jnp.broadcast_to((cloc - runs)[:, :, None], runs_l.shape).astype(I32)

    tm_comb = min(COMB_TM, tm_mid)
    lpos = _pos(idx, rank, adj_l, tm=tm_comb)
    tiled = lambda a: a.reshape(TOP_K, n // tm_comb, tm_comb).transpose(1, 0, 2).reshape(-1)
    lpos_tok = tiled(lpos)
    flat = lambda a: a.reshape(-1).astype(I32)
    xs = _dispatch((pstart + counts).astype(I32), (padded - counts).astype(I32), flat(cstart), flat(run_len),
                   flat(cloc), lpos_tok, h3r.reshape(n, PACK_TILES, LANES), n_slots=n_slots, tm=tm_comb)
    y = _expert(blk_e, nused, xs.reshape(n_slots * PACK_TILES, LANES), w_up[0],
                b_up[0].reshape(N_EXPERTS, 1, 2 * D_FF), w_down[0], b_down[0].reshape(N_EXPERTS, 1, D_MODEL), tb=tb)
    out = _combine(flat(cstart), flat(cpieces), flat(cloc), lpos_tok, tiled(gate),
                   x2, row(g_final), y, tm=tm_comb)
    return out.reshape(batch, seq, D_MODEL)
```

```python
import functools

import jax
import jax.numpy as jnp
from jax import lax
from jax.experimental import pallas as pl
from jax.experimental.pallas import tpu as pltpu

F32 = jnp.float32
BF16 = jnp.bfloat16
I32 = jnp.int32

D_MODEL = 1024
MLA_HEADS = 8
MLA_NOPE = 64
MLA_ROPE = 32
MLA_V = 64
MLA_WIDTH = MLA_HEADS * MLA_V
Q_LORA = 256
KV_LORA = 128
ROPE_THETA = 10000.0
CHUNK = 64
CONV_CH = 512
CONV_WIDTH = 31
X_HEADS = 4
X_HEAD_DIM = D_MODEL // X_HEADS
N_EXPERTS = 32
TOP_K = 4
D_FF = D_MODEL
SWIGLU_LIMIT = 7.0
SWIGLU_ALPHA = 1.702
NORM_EPS = 1e-5

LANES = 128
SUBLANES = 8
ROW_TILES = D_MODEL // LANES
PACK_TILES = ROW_TILES // 2
HI_MASK = 0xFFFF0000
VMEM_LIMIT_BYTES = 56 * 1024 * 1024

HEAD_PAD = LANES
CONV_HALO = 32

TM_MIX = 512
TQ_ATTN = 1024
TM_MID = 512
TB_EXPERT = 512
MID_SPLIT = 2
COMB_TM = 512


def _rms(x, g):
    return x * lax.rsqrt(jnp.mean(x * x, axis=-1, keepdims=True) + NORM_EPS) * g


def _cparams(sem):
    return pltpu.CompilerParams(dimension_semantics=sem, vmem_limit_bytes=VMEM_LIMIT_BYTES)


def _mix_in_kernel(x_ref, pos_ref, inv_ref, sgn_ref, gmix_ref, win_ref, gq_ref, wuq_ref, wuqs_ref,
                   gkv_ref, wukk_ref, wukv_ref, wdw_ref, bdw_ref, gln_ref, bln_ref,
                   q_ref, k_ref, v_ref, yc_ref, zbuf, zsh, *, tiles_per_batch, tm):
    i = pl.program_id(0)
    x = x_ref[...]
    h = _rms(x, gmix_ref[...]).astype(BF16)
    u = jnp.dot(h, win_ref[...], preferred_element_type=F32)
    c0 = Q_LORA + KV_LORA
    c1 = c0 + 2 * HEAD_PAD
    cq = u[:, 0:Q_LORA]
    ckv = u[:, Q_LORA:c0]
    kr = u[:, c0:c0 + HEAD_PAD]
    krs = u[:, c0 + HEAD_PAD:c1]
    a = u[:, c1:c1 + CONV_CH]
    gate = u[:, c1 + CONV_CH:c1 + 2 * CONV_CH]

    r0, r1 = MLA_NOPE, MLA_NOPE + MLA_ROPE
    ang_r = inv_ref[r0:r1, :] * pos_ref[...].astype(F32)
    cos_t = jnp.concatenate([jnp.ones((r0, tm), F32), jnp.cos(ang_r), jnp.ones((HEAD_PAD - r1, tm), F32)], axis=0)
    sin_t = jnp.concatenate([jnp.zeros((r0, tm), F32), jnp.sin(ang_r) * sgn_ref[r0:r1, :],
                             jnp.zeros((HEAD_PAD - r1, tm), F32)], axis=0)
    cosb = jnp.transpose(cos_t)
    sinb = jnp.transpose(sin_t)

    nt = (((1,), (1,)), ((), ()))
    cqn = _rms(cq, gq_ref[...]).astype(BF16)
    qm = lax.dot_general(wuq_ref[...], cqn, nt, preferred_element_type=F32)
    qs = lax.dot_general(wuqs_ref[...], cqn, nt, preferred_element_type=F32)
    for hd in range(MLA_HEADS):
        sl = slice(hd * HEAD_PAD, (hd + 1) * HEAD_PAD)
        q_ref[sl, :] = (qm[sl, :] * cos_t + qs[sl, :] * sin_t).astype(BF16)

    ckvn = _rms(ckv, gkv_ref[...]).astype(BF16)
    kk = jnp.dot(ckvn, wukk_ref[...], preferred_element_type=F32)
    v_ref[...] = lax.dot_general(wukv_ref[...], ckvn, nt, preferred_element_type=F32).astype(BF16)
    krot = kr * cosb + krs * sinb
    for hd in range(MLA_HEADS):
        sl = slice(hd * HEAD_PAD, (hd + 1) * HEAD_PAD)
        k_ref[:, sl] = (kk[:, sl] + krot).astype(BF16)

    z = a * jax.nn.sigmoid(gate)

    @pl.when(i % tiles_per_batch == 0)
    def _():
        zbuf[0:CONV_HALO, :] = jnp.zeros((CONV_HALO, CONV_CH), F32)

    zbuf[CONV_HALO:CONV_HALO + tm, :] = z
    off = CONV_HALO - (CONV_WIDTH - 1)
    rows = 32
    span = tm + CONV_HALO - SUBLANES
    step = 64
    for r in range(1, SUBLANES):
        for c0 in range(0, span, step):
            cl = min(step, span - c0)
            zsh[r - 1, c0:c0 + cl, :] = zbuf[c0 + r:c0 + r + cl, :]
    for r0 in range(0, tm, rows):
        acc = jnp.zeros((rows, CONV_CH), F32) + bdw_ref[...]
        for j in range(CONV_WIDTH):
            a8, ph = divmod(j + off, SUBLANES)
            lo_row = r0 + a8 * SUBLANES
            tap = zbuf[lo_row:lo_row + rows, :] if ph == 0 else zsh[ph - 1, lo_row:lo_row + rows, :]
            acc = acc + wdw_ref[j:j + 1, :] * tap
        mu = jnp.mean(acc, axis=-1, keepdims=True)
        cen = acc - mu
        var = jnp.mean(cen * cen, axis=-1, keepdims=True)
        y = cen * lax.rsqrt(var + NORM_EPS) * gln_ref[...] + bln_ref[...]
        yc_ref[r0:r0 + rows, :] = (y * jax.nn.sigmoid(y)).astype(BF16)
    zbuf[0:CONV_HALO, :] = zbuf[tm:tm + CONV_HALO, :]


def _mix_in(x2d, pos_row, inv_c, sgn_c, g_mix, w_in_p, g_q, w_uq_t, w_uq_st, g_kv, w_uk_k, w_uk_vt,
            w_dw, b_dw, g_ln, b_ln, *, seq):
    n = x2d.shape[0]
    tm = min(TM_MIX, seq)
    full = lambda a: pl.BlockSpec(a.shape, lambda i: (0,) * a.ndim)
    consts = [inv_c, sgn_c, g_mix, w_in_p, g_q, w_uq_t, w_uq_st, g_kv, w_uk_k, w_uk_vt, w_dw, b_dw, g_ln, b_ln]
    return pl.pallas_call(
        functools.partial(_mix_in_kernel, tiles_per_batch=seq // tm, tm=tm),
        grid=(n // tm,),
        in_specs=[pl.BlockSpec((tm, D_MODEL), lambda i: (i, 0)),
                  pl.BlockSpec((1, tm), lambda i: (0, i))] + [full(a) for a in consts],
        out_specs=[pl.BlockSpec((MLA_HEADS * HEAD_PAD, tm), lambda i: (0, i)),
                   pl.BlockSpec((tm, MLA_HEADS * HEAD_PAD), lambda i: (i, 0)),
                   pl.BlockSpec((MLA_WIDTH, tm), lambda i: (0, i)),
                   pl.BlockSpec((tm, CONV_CH), lambda i: (i, 0))],
        out_shape=[jax.ShapeDtypeStruct((MLA_HEADS * HEAD_PAD, n), BF16),
                   jax.ShapeDtypeStruct((n, MLA_HEADS * HEAD_PAD), BF16),
                   jax.ShapeDtypeStruct((MLA_WIDTH, n), BF16),
                   jax.ShapeDtypeStruct((n, CONV_CH), BF16)],
        scratch_shapes=[pltpu.VMEM((CONV_HALO + tm + SUBLANES, CONV_CH), F32),
                        pltpu.VMEM((SUBLANES - 1, CONV_HALO + tm, CONV_CH), F32)],
        compiler_params=_cparams(("arbitrary",)),
        name="mix_in",
    )(x2d, pos_row, *consts)


ATTN_GROUP = 2


TK_ATTN = 1024
DENOM_ROWS = 16


def _attn_kernel(q_ref, k_ref, v_ref, o_ref, *s_scr, tq):
    i = pl.program_id(1)
    tk = min(TK_ATTN, tq)
    per_q = tq // tk
    c2 = (MLA_NOPE + MLA_ROPE) ** -0.5 * 1.4426950408889634
    key_c = lax.broadcasted_iota(I32, (tk, tq), 0) // CHUNK
    qry_c = lax.broadcasted_iota(I32, (tk, tq), 1) // CHUNK

    for g in range(MLA_HEADS // ATTN_GROUP):
        heads = list(range(g * ATTN_GROUP, (g + 1) * ATTN_GROUP))
        qts = [q_ref[hd * HEAD_PAD:(hd + 1) * HEAD_PAD, :] for hd in heads]

        def step(j, carry, diag):
            start = pl.multiple_of(j * tk, tk)
            new = []
            for n_, hd in enumerate(heads):
                m, acc = carry[n_]
                kj = k_ref[pl.ds(start, tk), hd * HEAD_PAD:(hd + 1) * HEAD_PAD]
                vj = jnp.concatenate([v_ref[hd * MLA_V:(hd + 1) * MLA_V, pl.ds(start, tk)],
                                      jnp.ones((DENOM_ROWS, tk), BF16)], axis=0)
                s = jnp.dot(kj, qts[n_], preferred_element_type=F32)
                if diag is not None:
                    s = jnp.where(key_c + diag * (tk // CHUNK) <= qry_c, s, -jnp.inf)
                m_new = jnp.maximum(m, jnp.max(s, axis=0, keepdims=True))
                s_scr[n_][...] = s
                s = s_scr[n_][...]
                alpha = jnp.exp2((m - m_new) * c2)
                p = jnp.exp2((s - m_new) * c2)
                acc = alpha * acc + jnp.dot(vj, p.astype(BF16), preferred_element_type=F32)
                new.append((m_new, acc))
            return tuple(new)

        init = tuple((jnp.full((1, tq), -jnp.inf, F32), jnp.zeros((MLA_V + DENOM_ROWS, tq), F32)) for _ in heads)
        carry = lax.fori_loop(0, i * per_q, functools.partial(step, diag=None), init)
        for d in range(per_q):
            carry = step(i * per_q + d, carry, d)
        for n_, hd in enumerate(heads):
            _, acc = carry[n_]
            out = acc[0:MLA_V, :] / acc[MLA_V:MLA_V + 1, :]
            o_ref[:, hd * MLA_V:(hd + 1) * MLA_V] = jnp.transpose(out).astype(BF16)


def _attn(q_t, k, v_t, *, batch, seq):
    tq = min(TQ_ATTN, seq)
    nq = seq // tq
    return pl.pallas_call(
        functools.partial(_attn_kernel, tq=tq),
        grid=(batch, nq),
        in_specs=[pl.BlockSpec((MLA_HEADS * HEAD_PAD, tq), lambda b, i: (0, b * nq + i)),
                  pl.BlockSpec((seq, MLA_HEADS * HEAD_PAD), lambda b, i: (b, 0)),
                  pl.BlockSpec((MLA_WIDTH, seq), lambda b, i: (0, b))],
        out_specs=pl.BlockSpec((tq, MLA_WIDTH), lambda b, i: (b * nq + i, 0)),
        out_shape=jax.ShapeDtypeStruct((batch * seq, MLA_WIDTH), BF16),
        scratch_shapes=[pltpu.VMEM((min(TK_ATTN, tq), tq), F32) for _ in range(ATTN_GROUP)],
        compiler_params=_cparams(("arbitrary", "arbitrary")),
        name="mla_attn",
    )(q_t, k, v_t)


def _mem_kv_kernel(mem_ref, g_ref, w_ref, kv_ref):
    mn = _rms(mem_ref[...], g_ref[...]).astype(BF16)
    kv_ref[...] = jnp.dot(mn, w_ref[...], preferred_element_type=F32).astype(BF16)


def _mem_kv(mem2d, g_mem, w_xkv, *, batch, mem_len):
    return pl.pallas_call(
        _mem_kv_kernel,
        grid=(batch,),
        in_specs=[pl.BlockSpec((mem_len, D_MODEL), lambda b: (b, 0)),
                  pl.BlockSpec((1, D_MODEL), lambda b: (0, 0)),
                  pl.BlockSpec((D_MODEL, 2 * D_MODEL), lambda b: (0, 0))],
        out_specs=pl.BlockSpec((mem_len, 2 * D_MODEL), lambda b: (b, 0)),
        out_shape=jax.ShapeDtypeStruct((batch * mem_len, 2 * D_MODEL), BF16),
        compiler_params=_cparams(("arbitrary",)),
        name="mem_kv",
    )(mem2d, g_mem, w_xkv)


def _mid_kernel(x_ref, ya_ref, yc_ref, kv_ref, woa_ref, wob_ref, gx_ref, wxq_ref, wxo_ref, gf_ref,
                wrh_ref, wrl_ref, br_ref, tri_ref,
                x2_ref, h3_ref, idx_ref, gate_ref, rank_ref, cnt_ref, runs_ref, run_ref, *, tm):
    i = pl.program_id(0)

    @pl.when(i == 0)
    def _():
        run_ref[...] = jnp.zeros_like(run_ref)

    nt = (((1,), (1,)), ((), ()))
    half_d = D_MODEL // 2
    rows = tm // MID_SPLIT
    lgs = []
    for part in range(MID_SPLIT):
        rs = slice(part * rows, (part + 1) * rows)
        x1 = (x_ref[rs, :]
              + jnp.dot(ya_ref[rs, :], woa_ref[...], preferred_element_type=F32)
              + jnp.dot(yc_ref[rs, :], wob_ref[...], preferred_element_type=F32))
        h = _rms(x1, gx_ref[...]).astype(BF16)
        q = jnp.dot(h, wxq_ref[...], preferred_element_type=F32).astype(BF16)
        os = []
        for hd in range(X_HEADS):
            sl = slice(hd * X_HEAD_DIM, (hd + 1) * X_HEAD_DIM)
            vsl = slice(D_MODEL + hd * X_HEAD_DIM, D_MODEL + (hd + 1) * X_HEAD_DIM)
            s = lax.dot_general(q[:, sl], kv_ref[:, sl], nt, preferred_element_type=F32) * (X_HEAD_DIM ** -0.5)
            e = jnp.exp(s - jnp.max(s, axis=-1, keepdims=True))
            p = (e / jnp.sum(e, axis=-1, keepdims=True)).astype(BF16)
            os.append(jnp.dot(p, kv_ref[:, vsl], preferred_element_type=F32).astype(BF16))
        o = jnp.concatenate(os, axis=-1)
        x2 = x1 + jnp.dot(o, wxo_ref[...], preferred_element_type=F32)
        x2_ref[rs, :] = x2
        h3 = _rms(x2, gf_ref[...])
        hh = h3.astype(BF16)
        hf = hh.astype(F32)
        word = (lax.shift_right_logical(pltpu.bitcast(hf[:, :half_d], jnp.uint32), jnp.uint32(16))
                | (pltpu.bitcast(hf[:, half_d:], jnp.uint32) & jnp.uint32(HI_MASK)))
        for t in range(PACK_TILES):
            h3_ref[pl.ds(part * rows * PACK_TILES + t, rows, stride=PACK_TILES), :] = word[:, t * LANES:(t + 1) * LANES]

        hl = (h3 - hf).astype(BF16)
        lgs.append(jnp.dot(hh, wrh_ref[...], preferred_element_type=F32)
                   + (jnp.dot(hh, wrl_ref[...], preferred_element_type=F32)
                      + jnp.dot(hl, wrh_ref[...], preferred_element_type=F32)))
    lg = jnp.concatenate(lgs, axis=0)
    lgt = jnp.transpose(lg)[0:N_EXPERTS, :] + br_ref[...]
    eid = lax.broadcasted_iota(I32, (N_EXPERTS, tm), 0)
    vals, hots = [], []
    cur = lgt
    for k in range(TOP_K):
        mx = jnp.max(cur, axis=0, keepdims=True)
        ik = jnp.min(jnp.where(cur == mx, eid, N_EXPERTS), axis=0, keepdims=True)
        hot = eid == ik
        cur = jnp.where(hot, -jnp.inf, cur)
        idx_ref[k:k + 1, :] = ik
        vals.append(mx)
        hots.append(hot)
    es = [jnp.exp(v - vals[0]) for v in vals]
    den = es[0] + es[1] + es[2] + es[3]
    for k in range(TOP_K):
        gate_ref[k:k + 1, :] = es[k] / den
    cnt = jnp.zeros((N_EXPERTS, tm), F32)
    for k in range(TOP_K):
        cnt = cnt + hots[k].astype(F32)
    run = run_ref[:, 0:1]
    ctm = min(COMB_TM, tm)
    for j in range(tm // ctm):
        before = run if j == 0 else run + jnp.sum(cnt[:, :j * ctm], axis=1, keepdims=True)
        runs_ref[j] = jnp.broadcast_to(before, (N_EXPERTS, LANES)).astype(I32)
    tot = jnp.dot(cnt.astype(BF16), tri_ref[...], preferred_element_type=F32) + run
    for k in range(TOP_K):
        rank_ref[k:k + 1, :] = jnp.sum(jnp.where(hots[k], tot, 0.0), axis=0, keepdims=True).astype(I32)
    new_run = run + jnp.sum(cnt, axis=1, keepdims=True)
    run_ref[...] = jnp.broadcast_to(new_run, run_ref.shape)
    cnt_ref[...] = jnp.broadcast_to(new_run, cnt_ref.shape).astype(I32)


def _mid(x2d, ya, yc, kv, w_oa, w_ob, g_x, w_xq, w_xo, g_f, wr_h, wr_l, b_r, tri, *, seq, mem_len, tm):
    n = x2d.shape[0]
    tpb = seq // tm
    full = lambda a: pl.BlockSpec(a.shape, lambda i: (0,) * a.ndim)
    consts_a = [w_oa, w_ob, g_x, w_xq, w_xo, g_f, wr_h, wr_l, b_r, tri]
    return pl.pallas_call(
        functools.partial(_mid_kernel, tm=tm),
        grid=(n // tm,),
        in_specs=[pl.BlockSpec((tm, D_MODEL), lambda i: (i, 0)),
                  pl.BlockSpec((tm, MLA_WIDTH), lambda i: (i, 0)),
                  pl.BlockSpec((tm, CONV_CH), lambda i: (i, 0)),
                  pl.BlockSpec((mem_len, 2 * D_MODEL), lambda i: (i // tpb, 0))] + [full(a) for a in consts_a],
        out_specs=[pl.BlockSpec((tm, D_MODEL), lambda i: (i, 0)),
                   pl.BlockSpec((tm * PACK_TILES, LANES), lambda i: (i, 0)),
                   pl.BlockSpec((TOP_K, tm), lambda i: (0, i)),
                   pl.BlockSpec((TOP_K, tm), lambda i: (0, i)),
                   pl.BlockSpec((TOP_K, tm), lambda i: (0, i)),
                   pl.BlockSpec((N_EXPERTS, LANES), lambda i: (0, 0)),
                   pl.BlockSpec((tm // min(COMB_TM, tm), N_EXPERTS, LANES), lambda i: (i, 0, 0))],
        out_shape=[jax.ShapeDtypeStruct((n, D_MODEL), F32),
                   jax.ShapeDtypeStruct((n * PACK_TILES, LANES), jnp.uint32),
                   jax.ShapeDtypeStruct((TOP_K, n), I32),
                   jax.ShapeDtypeStruct((TOP_K, n), F32),
                   jax.ShapeDtypeStruct((TOP_K, n), I32),
                   jax.ShapeDtypeStruct((N_EXPERTS, LANES), I32),
                   jax.ShapeDtypeStruct((n // min(COMB_TM, tm), N_EXPERTS, LANES), I32)],
        scratch_shapes=[pltpu.VMEM((N_EXPERTS, LANES), F32)],
        compiler_params=_cparams(("arbitrary",)),
        name="mid",
    )(x2d, ya, yc, kv, *consts_a)


def _pos_kernel(idx_ref, rank_ref, adj_ref, lpos_ref, *, tm, group):
    i = pl.program_id(0)
    eid = lax.broadcasted_iota(I32, (N_EXPERTS, tm), 0)
    for j in range(group):
        half = ((i * group + j) % 2) * _comb_cap(tm)
        cols = slice(j * tm, (j + 1) * tm)
        adj = adj_ref[j][:, 0:1] + half
        for k in range(TOP_K):
            hot = eid == idx_ref[k:k + 1, cols]
            lpos_ref[k:k + 1, cols] = (jnp.sum(jnp.where(hot, adj, 0), axis=0, keepdims=True)
                                       + rank_ref[k:k + 1, cols]) * ROW_TILES


def _pos(idx, rank, adj_l, *, tm):
    n = idx.shape[1]
    n_tiles = n // tm
    group = 4 if n_tiles % 4 == 0 else 1
    return pl.pallas_call(
        functools.partial(_pos_kernel, tm=tm, group=group),
        grid=(n_tiles // group,),
        in_specs=[pl.BlockSpec((TOP_K, tm * group), lambda i: (0, i)),
                  pl.BlockSpec((TOP_K, tm * group), lambda i: (0, i)),
                  pl.BlockSpec((group, N_EXPERTS, LANES), lambda i: (i, 0, 0))],
        out_specs=pl.BlockSpec((TOP_K, tm * group), lambda i: (0, i)),
        out_shape=jax.ShapeDtypeStruct((TOP_K, n), I32),
        compiler_params=_cparams(("arbitrary",)),
        name="slot_pos",
    )(idx, rank, adj_l)


RUN_BITS = tuple(1 << b for b in range(9, -1, -1))


def _pow2_runs(length, fn):
    off = jnp.int32(0)
    for bit in RUN_BITS:
        take = (length & bit) != 0

        @pl.when(take)
        def _(off=off, bit=bit):
            fn(off, bit)

        off = off + jnp.where(take, bit, 0)


def _dispatch_kernel(pad_start_ref, pad_cnt_ref, cstart_ref, clen_ref, cloc_ref, lpos_ref, h3_ref, xs_hbm,
                     cbuf, sems, zsem, *, tm, n_tiles):
    i = pl.program_id(0)
    slot = i % 2
    cap = _comb_cap(tm)
    rows_per_tile = TOP_K * tm

    @pl.when(i == 0)
    def _():
        zrows = RUN_BITS[0]
        cbuf[pl.ds(cap, zrows)] = jnp.zeros((zrows, PACK_TILES, LANES), jnp.uint32)

        def pad_pass(start):
            def per_expert(e, c):
                def one(off, bit):
                    cp = pltpu.make_async_copy(cbuf.at[pl.ds(cap, bit)],
                                               xs_hbm.at[pl.ds(pad_start_ref[e] + off, bit)], zsem)
                    cp.start() if start else cp.wait()

                _pow2_runs(pad_cnt_ref[e], one)
                return c

            lax.fori_loop(0, N_EXPERTS, per_expert, 0)

        pad_pass(True)
        pad_pass(False)

    def wait_tile(s):
        pltpu.make_async_copy(cbuf.at[pl.ds(s * cap, rows_per_tile)], xs_hbm.at[pl.ds(0, rows_per_tile)],
                              sems.at[s]).wait()

    @pl.when(i >= 2)
    def _():
        wait_tile(slot)

    def tok_group(tt, c):
        for u in range(COMB_UNROLL):
            t = tt * COMB_UNROLL + u
            row = h3_ref[t]
            for k in range(TOP_K):
                cbuf[lax.shift_right_logical(lpos_ref[k * tm + t], 3)] = row
        return c

    lax.fori_loop(0, tm // COMB_UNROLL, tok_group, 0)

    def per_run(e, c):
        src0 = slot * cap + cloc_ref[i * N_EXPERTS + e]
        dst0 = cstart_ref[i * N_EXPERTS + e]

        def one(off, bit):
            pltpu.make_async_copy(cbuf.at[pl.ds(src0 + off, bit)], xs_hbm.at[pl.ds(dst0 + off, bit)],
                                  sems.at[slot]).start()

        _pow2_runs(clen_ref[i * N_EXPERTS + e], one)
        return c

    lax.fori_loop(0, N_EXPERTS, per_run, 0)

    @pl.when(i == n_tiles - 1)
    def _():
        wait_tile(slot)
        if n_tiles >= 2:
            wait_tile(1 - slot)


def _dispatch(pad_start, pad_cnt, cstart, clen, cloc, lpos_tok, h3r, *, n_slots, tm):
    n = h3r.shape[0]
    n_tiles = n // tm
    gs = pltpu.PrefetchScalarGridSpec(
        num_scalar_prefetch=5,
        grid=(n_tiles,),
        in_specs=[pl.BlockSpec((TOP_K * tm,), lambda i, *_: (i,), memory_space=pltpu.SMEM),
                  pl.BlockSpec((tm, PACK_TILES, LANES), lambda i, *_: (i, 0, 0))],
        out_specs=pl.BlockSpec(memory_space=pl.ANY),
        scratch_shapes=[pltpu.VMEM((2 * _comb_cap(tm), PACK_TILES, LANES), jnp.uint32),
                        pltpu.SemaphoreType.DMA((2,)),
                        pltpu.SemaphoreType.DMA(())],
    )
    return pl.pallas_call(
        functools.partial(_dispatch_kernel, tm=tm, n_tiles=n_tiles),
        grid_spec=gs,
        out_shape=jax.ShapeDtypeStruct((n_slots, PACK_TILES, LANES), jnp.uint32),
        compiler_params=_cparams(("arbitrary",)),
        name="dispatch",
    )(pad_start, pad_cnt, cstart, clen, cloc, lpos_tok, h3r)


def _expert_kernel(blk_e_ref, nused_ref, x_ref, wu_ref, bu_ref, wd_ref, bd_ref, y_ref, wu_bf, wd_bf, *, tb):
    b = pl.program_id(0)

    @pl.when(b < nused_ref[0])
    def _():
        prev = blk_e_ref[jnp.maximum(b - 1, 0)]

        @pl.when(jnp.logical_or(b == 0, blk_e_ref[b] != prev))
        def _():
            wu_bf[...] = wu_ref[...].astype(BF16)
            wd_bf[...] = wd_ref[...].astype(BF16)

        words = [x_ref[pl.ds(t, tb, stride=PACK_TILES), :] for t in range(PACK_TILES)]
        lo = [pltpu.bitcast(lax.shift_left(w, jnp.uint32(16)), F32) for w in words]
        hi = [pltpu.bitcast(w & jnp.uint32(HI_MASK), F32) for w in words]
        x = jnp.concatenate(lo + hi, axis=-1).astype(BF16)
        up = jnp.dot(x, wu_bf[...], preferred_element_type=F32) + bu_ref[...]
        glu = jnp.minimum(up[:, :D_FF], SWIGLU_LIMIT)
        lin = jnp.clip(up[:, D_FF:], -SWIGLU_LIMIT, SWIGLU_LIMIT)
        act = (glu * jax.nn.sigmoid(SWIGLU_ALPHA * glu) * (lin + 1.0)).astype(BF16)
        y = jnp.dot(act, wd_bf[...], preferred_element_type=F32) + bd_ref[...]
        for t in range(ROW_TILES):
            y_ref[pl.ds(t, tb, stride=ROW_TILES), :] = y[:, t * LANES:(t + 1) * LANES]


def _expert(blk_e, nused, xs, w_up, b_up, w_down, b_down, *, tb):
    n_slots = xs.shape[0] // PACK_TILES
    nb = n_slots // tb
    blk = lambda b, be, nu: jnp.minimum(b, nu[0] - 1)
    gs = pltpu.PrefetchScalarGridSpec(
        num_scalar_prefetch=2,
        grid=(nb,),
        in_specs=[pl.BlockSpec((tb * PACK_TILES, LANES), lambda b, be, nu: (blk(b, be, nu), 0)),
                  pl.BlockSpec((None, D_MODEL, 2 * D_FF), lambda b, be, nu: (be[blk(b, be, nu)], 0, 0)),
                  pl.BlockSpec((None, 1, 2 * D_FF), lambda b, be, nu: (be[blk(b, be, nu)], 0, 0)),
                  pl.BlockSpec((None, D_FF, D_MODEL), lambda b, be, nu: (be[blk(b, be, nu)], 0, 0)),
                  pl.BlockSpec((None, 1, D_MODEL), lambda b, be, nu: (be[blk(b, be, nu)], 0, 0))],
        out_specs=pl.BlockSpec((tb * ROW_TILES, LANES), lambda b, be, nu: (blk(b, be, nu), 0)),
        scratch_shapes=[pltpu.VMEM((D_MODEL, 2 * D_FF), BF16),
                        pltpu.VMEM((D_FF, D_MODEL), BF16)],
    )
    return pl.pallas_call(
        functools.partial(_expert_kernel, tb=tb),
        grid_spec=gs,
        out_shape=jax.ShapeDtypeStruct(((n_slots + tb) * ROW_TILES, LANES), F32),
        compiler_params=_cparams(("arbitrary",)),
        name="expert_ffn",
    )(blk_e, nused, xs, w_up, b_up, w_down, b_down)


COMB_PIECE = 32
COMB_UNROLL = 8


def _comb_cap(tm):
    return TOP_K * tm + N_EXPERTS * (COMB_PIECE - 1)


def _combine_kernel(cstart_ref, cpieces_ref, cloc_ref, lpos_ref, gate_ref, x2_ref, gfin_ref, y_hbm, out_ref,
                    chunks, rt, sems, *, tm, n_tiles):
    i = pl.program_id(0)
    slot = i % 2
    piece_rows = COMB_PIECE * ROW_TILES

    def piece_copy(tile, e, p, to_slot):
        src = (cstart_ref[tile * N_EXPERTS + e] + p * COMB_PIECE) * ROW_TILES
        dst = (to_slot * _comb_cap(tm) + cloc_ref[tile * N_EXPERTS + e] + p * COMB_PIECE) * ROW_TILES
        return pltpu.make_async_copy(
            y_hbm.at[pl.ds(pl.multiple_of(src, ROW_TILES), piece_rows), :],
            chunks.at[pl.ds(pl.multiple_of(dst, ROW_TILES), piece_rows), :],
            sems.at[to_slot])

    def for_pieces(tile, to_slot, start):
        def per_e(e, c):
            def per_p(p, c2):
                cp = piece_copy(tile, e, p, to_slot)
                if start:
                    cp.start()
                else:
                    cp.wait()
                return c2

            lax.fori_loop(0, cpieces_ref[tile * N_EXPERTS + e], per_p, 0)
            return c

        lax.fori_loop(0, N_EXPERTS, per_e, 0)

    @pl.when(i == 0)
    def _():
        for_pieces(0, 0, True)

    @pl.when(i + 1 < n_tiles)
    def _():
        for_pieces(i + 1, 1 - slot, True)

    for_pieces(i, slot, False)

    def tok_group(tt, c):
        for u in range(COMB_UNROLL):
            t = tt * COMB_UNROLL + u
            acc = None
            for k in range(TOP_K):
                r = pl.multiple_of(lpos_ref[k * tm + t], ROW_TILES)
                term = gate_ref[k * tm + t] * chunks[pl.ds(r, ROW_TILES), :]
                acc = term if acc is None else acc + term
            rt[pl.ds(pl.multiple_of(t * ROW_TILES, ROW_TILES), ROW_TILES), :] = acc
        return c

    lax.fori_loop(0, tm // COMB_UNROLL, tok_group, 0)

    moe = jnp.concatenate([rt[pl.ds(s, tm, stride=ROW_TILES), :] for s in range(ROW_TILES)], axis=-1)
    out_ref[...] = _rms(x2_ref[...] + moe, gfin_ref[...])


def _combine(cstart, cpieces, cloc, lpos, gate, x2, g_final, y, *, tm):
    n = x2.shape[0]
    n_tiles = n // tm
    gs = pltpu.PrefetchScalarGridSpec(
        num_scalar_prefetch=3,
        grid=(n_tiles,),
        in_specs=[pl.BlockSpec((TOP_K * tm,), lambda i, a, b, c: (i,), memory_space=pltpu.SMEM),
                  pl.BlockSpec((TOP_K * tm,), lambda i, a, b, c: (i,), memory_space=pltpu.SMEM),
                  pl.BlockSpec((tm, D_MODEL), lambda i, a, b, c: (i, 0)),
                  pl.BlockSpec((1, D_MODEL), lambda i, a, b, c: (0, 0)),
                  pl.BlockSpec(memory_space=pl.ANY)],
        out_specs=pl.BlockSpec((tm, D_MODEL), lambda i, a, b, c: (i, 0)),
        scratch_shapes=[pltpu.VMEM((2 * _comb_cap(tm) * ROW_TILES, LANES), F32),
                        pltpu.VMEM((tm * ROW_TILES, LANES), F32),
                        pltpu.SemaphoreType.DMA((2,))],
    )
    return pl.pallas_call(
        functools.partial(_combine_kernel, tm=tm, n_tiles=n_tiles),
        grid_spec=gs,
        out_shape=jax.ShapeDtypeStruct((n, D_MODEL), F32),
        compiler_params=_cparams(("arbitrary",)),
        name="combine",
    )(cstart, cpieces, cloc, lpos, gate, x2, g_final, y)


def _head_pad_cols(w, per_head_in, take, place):
    kdim = w.shape[0]
    w3 = w.reshape(kdim, MLA_HEADS, per_head_in)
    out = jnp.zeros((kdim, MLA_HEADS, HEAD_PAD), w.dtype)
    for (t0, t1), p0 in zip(take, place):
        out = out.at[:, :, p0:p0 + (t1 - t0)].set(w3[:, :, t0:t1])
    return out.reshape(kdim, MLA_HEADS * HEAD_PAD)


def kernel(x, mem, positions, g_mix, w_in, g_q, w_uq, g_kv, w_ukv, w_dw, b_dw, g_conv_ln, b_conv_ln, w_out,
           g_xattn, g_mem, w_xq, w_xkv, w_xo, g_ffn, w_router, b_router, w_up, b_up, w_down, b_down, g_final):
    batch, seq, _ = x.shape
    mem_len = mem.shape[1]
    n = batch * seq
    half = MLA_ROPE // 2
    r0 = MLA_NOPE
    assert w_in.shape[0] == 1, "one trunk layer"

    row = lambda v: v.reshape(1, -1).astype(F32)

    wi = w_in[0]
    o1, o2, o3 = Q_LORA, Q_LORA + KV_LORA, Q_LORA + KV_LORA + MLA_ROPE
    kr_blk = jnp.zeros((D_MODEL, HEAD_PAD), F32).at[:, r0:r0 + MLA_ROPE].set(wi[:, o2:o3])
    kr_swp = (jnp.zeros((D_MODEL, HEAD_PAD), F32)
              .at[:, r0:r0 + half].set(wi[:, o2 + half:o3])
              .at[:, r0 + half:r0 + MLA_ROPE].set(wi[:, o2:o2 + half]))
    w_in_p = jnp.concatenate([wi[:, :o2], kr_blk, kr_swp, wi[:, o3:]], axis=1).astype(BF16)
    per_q = MLA_NOPE + MLA_ROPE
    w_uq_p = _head_pad_cols(w_uq[0], per_q, [(0, per_q)], [0]).astype(BF16)
    w_uq_s = _head_pad_cols(w_uq[0], per_q, [(MLA_NOPE + half, per_q), (MLA_NOPE, MLA_NOPE + half)],
                            [r0, r0 + half]).astype(BF16)
    per_kv = MLA_NOPE + MLA_V
    w_uk_k = _head_pad_cols(w_ukv[0], per_kv, [(0, MLA_NOPE)], [0]).astype(BF16)
    w_uk_v = (w_ukv[0].reshape(KV_LORA, MLA_HEADS, per_kv)[:, :, MLA_NOPE:]
              .reshape(KV_LORA, MLA_WIDTH).astype(BF16))
    inv = ROPE_THETA ** (-jnp.arange(0, MLA_ROPE, 2, dtype=F32) / MLA_ROPE)
    inv_c = (jnp.zeros((LANES, 1), F32).at[r0:r0 + half, 0].set(inv).at[r0 + half:r0 + MLA_ROPE, 0].set(inv))
    sgn_c = (jnp.zeros((LANES, 1), F32).at[r0:r0 + half, 0].set(-1.0).at[r0 + half:r0 + MLA_ROPE, 0].set(1.0))

    x2d = x.reshape(n, D_MODEL)
    pos_row = positions.reshape(1, n).astype(I32)

    q_t, k, v_t, y_conv = _mix_in(x2d, pos_row, inv_c, sgn_c, row(g_mix[0]), w_in_p, row(g_q[0]),
                                  jnp.transpose(w_uq_p), jnp.transpose(w_uq_s), row(g_kv[0]), w_uk_k,
                                  jnp.transpose(w_uk_v), w_dw[0].astype(F32), row(b_dw[0]),
                                  row(g_conv_ln[0]), row(b_conv_ln[0]), seq=seq)
    y_mla = _attn(q_t, k, v_t, batch=batch, seq=seq)
    kv = _mem_kv(mem.reshape(batch * mem_len, D_MODEL), row(g_mem[0]), w_xkv[0].astype(BF16),
                 batch=batch, mem_len=mem_len)

    tm_mid = min(TM_MID, seq)
    wr = jnp.zeros((D_MODEL, LANES), F32).at[:, :N_EXPERTS].set(w_router[0])
    wr_h = wr.astype(BF16)
    wr_l = (wr - wr_h.astype(F32)).astype(BF16)
    b_r = b_router[0].reshape(N_EXPERTS, 1).astype(F32)
    tri = (lax.broadcasted_iota(I32, (tm_mid, tm_mid), 0)
           < lax.broadcasted_iota(I32, (tm_mid, tm_mid), 1)).astype(BF16)
    wo = w_out[0].astype(BF16)
    x2, h3r, idx, gate, rank, cnt, runs_l = _mid(
        x2d, y_mla, y_conv, kv, wo[:MLA_WIDTH], wo[MLA_WIDTH:], row(g_xattn[0]), w_xq[0].astype(BF16),
        w_xo[0].astype(BF16), row(g_ffn[0]), wr_h, wr_l, b_r, tri, seq=seq, mem_len=mem_len, tm=tm_mid)

    tb = TB_EXPERT
    counts = cnt[:, 0]
    padded = (counts + tb - 1) // tb * tb
    pend = jnp.cumsum(padded)
    pstart = pend - padded
    n_slots = n * TOP_K + N_EXPERTS * tb
    nb = n_slots // tb
    blk_first = jnp.arange(nb, dtype=I32) * tb
    blk_e = jnp.minimum(jnp.sum((pend[None, :] <= blk_first[:, None]).astype(I32), axis=1),
                        N_EXPERTS - 1).astype(I32)
    nused = (pend[-1:] // tb).astype(I32)

    runs = runs_l[:, :, 0]
    run_len = jnp.concatenate([runs[1:], counts[None, :]], axis=0) - runs
    cstart = pstart[None, :] + runs
    cpieces = (run_len + COMB_PIECE - 1) // COMB_PIECE
    cloc = jnp.cumsum(cpieces * COMB_PIECE, axis=1) - cpieces * COMB_PIECE
    adj_l = jnp.broadcast_to((cloc - runs)[:, :, None], runs_l.shape).astype(I32)

    tm_comb = min(COMB_TM, tm_mid)
    lpos = _pos(idx, rank, adj_l, tm=tm_comb)
    tiled = lambda a: a.reshape(TOP_K, n // tm_comb, tm_comb).transpose(1, 0, 2).reshape(-1)
    lpos_tok = tiled(lpos)
    flat = lambda a: a.reshape(-1).astype(I32)
    xs = _dispatch((pstart + counts).astype(I32), (padded - counts).astype(I32), flat(cstart), flat(run_len),
                   flat(cloc), lpos_tok, h3r.reshape(n, PACK_TILES, LANES), n_slots=n_slots, tm=tm_comb)
    y = _expert(blk_e, nused, xs.reshape(n_slots * PACK_TILES, LANES), w_up[0],
                b_up[0].reshape(N_EXPERTS, 1, 2 * D_FF), w_down[0], b_down[0].reshape(N_EXPERTS, 1, D_MODEL), tb=tb)
    out = _combine(flat(cstart), flat(cpieces), flat(cloc), lpos_tok, tiled(gate),
                   x2, row(g_final), y, tm=tm_comb)
    return out.reshape(batch, seq, D_MODEL)
```

```python
import functools

import jax
import jax.numpy as jnp
from jax import lax
from jax.experimental import pallas as pl
from jax.experimental.pallas import tpu as pltpu

F32 = jnp.float32
BF16 = jnp.bfloat16
I32 = jnp.int32

D_MODEL = 1024
MLA_HEADS = 8
MLA_NOPE = 64
MLA_ROPE = 32
MLA_V = 64
MLA_WIDTH = MLA_HEADS * MLA_V
Q_LORA = 256
KV_LORA = 128
ROPE_THETA = 10000.0
CHUNK = 64
CONV_CH = 512
CONV_WIDTH = 31
X_HEADS = 4
X_HEAD_DIM = D_MODEL // X_HEADS
N_EXPERTS = 32
TOP_K = 4
D_FF = D_MODEL
SWIGLU_LIMIT = 7.0
SWIGLU_ALPHA = 1.702
NORM_EPS = 1e-5

LANES = 128
SUBLANES = 8
ROW_TILES = D_MODEL // LANES
PACK_TILES = ROW_TILES // 2
HI_MASK = 0xFFFF0000
VMEM_LIMIT_BYTES = 56 * 1024 * 1024

HEAD_PAD = LANES
Q_PRESCALE = (MLA_NOPE + MLA_ROPE) ** -0.5 * 1.4426950408889634
CONV_HALO = 32

TM_MIX = 512
TQ_ATTN = 1024
TM_MID = 512
TB_EXPERT = 512
MID_SPLIT = 2
COMB_TM = 512


def _rms(x, g):
    return x * lax.rsqrt(jnp.mean(x * x, axis=-1, keepdims=True) + NORM_EPS) * g


def _cparams(sem):
    return pltpu.CompilerParams(dimension_semantics=sem, vmem_limit_bytes=VMEM_LIMIT_BYTES)


def _mix_in_kernel(x_ref, pos_ref, inv_ref, sgn_ref, gmix_ref, win_ref, gq_ref, wuq_ref, wuqs_ref,
                   gkv_ref, wukk_ref, wukv_ref, wdw_ref, bdw_ref, gln_ref, bln_ref,
                   q_ref, k_ref, v_ref, yc_ref, zbuf, zsh, *, tiles_per_batch, tm):
    i = pl.program_id(0)
    x = x_ref[...]
    h = _rms(x, gmix_ref[...]).astype(BF16)
    u = jnp.dot(h, win_ref[...], preferred_element_type=F32)
    c0 = Q_LORA + KV_LORA
    c1 = c0 + 2 * HEAD_PAD
    cq = u[:, 0:Q_LORA]
    ckv = u[:, Q_LORA:c0]
    kr = u[:, c0:c0 + HEAD_PAD]
    krs = u[:, c0 + HEAD_PAD:c1]
    a = u[:, c1:c1 + CONV_CH]
    gate = u[:, c1 + CONV_CH:c1 + 2 * CONV_CH]

    r0, r1 = MLA_NOPE, MLA_NOPE + MLA_ROPE
    ang_r = inv_ref[r0:r1, :] * pos_ref[...].astype(F32)
    cos_t = jnp.concatenate([jnp.ones((r0, tm), F32), jnp.cos(ang_r), jnp.ones((HEAD_PAD - r1, tm), F32)], axis=0)
    sin_t = jnp.concatenate([jnp.zeros((r0, tm), F32), jnp.sin(ang_r) * sgn_ref[r0:r1, :],
                             jnp.zeros((HEAD_PAD - r1, tm), F32)], axis=0)
    cosb = jnp.transpose(cos_t)
    sinb = jnp.transpose(sin_t)
    cos_q = cos_t * Q_PRESCALE
    sin_q = sin_t * Q_PRESCALE

    nt = (((1,), (1,)), ((), ()))
    cqn = _rms(cq, gq_ref[...]).astype(BF16)
    qm = lax.dot_general(wuq_ref[...], cqn, nt, preferred_element_type=F32)
    qs = lax.dot_general(wuqs_ref[...], cqn, nt, preferred_element_type=F32)
    for hd in range(MLA_HEADS):
        sl = slice(hd * HEAD_PAD, (hd + 1) * HEAD_PAD)
        q_ref[sl, :] = (qm[sl, :] * cos_q + qs[sl, :] * sin_q).astype(BF16)

    ckvn = _rms(ckv, gkv_ref[...]).astype(BF16)
    kk = jnp.dot(ckvn, wukk_ref[...], preferred_element_type=F32)
    v_ref[...] = lax.dot_general(wukv_ref[...], ckvn, nt, preferred_element_type=F32).astype(BF16)
    krot = kr * cosb + krs * sinb
    for hd in range(MLA_HEADS):
        sl = slice(hd * HEAD_PAD, (hd + 1) * HEAD_PAD)
        k_ref[:, sl] = (kk[:, sl] + krot).astype(BF16)

    z = a * jax.nn.sigmoid(gate)

    @pl.when(i % tiles_per_batch == 0)
    def _():
        zbuf[0:CONV_HALO, :] = jnp.zeros((CONV_HALO, CONV_CH), F32)

    zbuf[CONV_HALO:CONV_HALO + tm, :] = z
    off = CONV_HALO - (CONV_WIDTH - 1)
    rows = 32
    span = tm + CONV_HALO - SUBLANES
    step = 64
    for r in range(1, SUBLANES):
        for c0 in range(0, span, step):
            cl = min(step, span - c0)
            zsh[r - 1, c0:c0 + cl, :] = zbuf[c0 + r:c0 + r + cl, :]
    for r0 in range(0, tm, rows):
        acc = jnp.zeros((rows, CONV_CH), F32) + bdw_ref[...]
        for j in range(CONV_WIDTH):
            a8, ph = divmod(j + off, SUBLANES)
            lo_row = r0 + a8 * SUBLANES
            tap = zbuf[lo_row:lo_row + rows, :] if ph == 0 else zsh[ph - 1, lo_row:lo_row + rows, :]
            acc = acc + wdw_ref[j:j + 1, :] * tap
        mu = jnp.mean(acc, axis=-1, keepdims=True)
        cen = acc - mu
        var = jnp.mean(cen * cen, axis=-1, keepdims=True)
        y = cen * lax.rsqrt(var + NORM_EPS) * gln_ref[...] + bln_ref[...]
        yc_ref[r0:r0 + rows, :] = (y * jax.nn.sigmoid(y)).astype(BF16)
    zbuf[0:CONV_HALO, :] = zbuf[tm:tm + CONV_HALO, :]


def _mix_in(x2d, pos_row, inv_c, sgn_c, g_mix, w_in_p, g_q, w_uq_t, w_uq_st, g_kv, w_uk_k, w_uk_vt,
            w_dw, b_dw, g_ln, b_ln, *, seq):
    n = x2d.shape[0]
    tm = min(TM_MIX, seq)
    full = lambda a: pl.BlockSpec(a.shape, lambda i: (0,) * a.ndim)
    consts = [inv_c, sgn_c, g_mix, w_in_p, g_q, w_uq_t, w_uq_st, g_kv, w_uk_k, w_uk_vt, w_dw, b_dw, g_ln, b_ln]
    return pl.pallas_call(
        functools.partial(_mix_in_kernel, tiles_per_batch=seq // tm, tm=tm),
        grid=(n // tm,),
        in_specs=[pl.BlockSpec((tm, D_MODEL), lambda i: (i, 0)),
                  pl.BlockSpec((1, tm), lambda i: (0, i))] + [full(a) for a in consts],
        out_specs=[pl.BlockSpec((MLA_HEADS * HEAD_PAD, tm), lambda i: (0, i)),
                   pl.BlockSpec((tm, MLA_HEADS * HEAD_PAD), lambda i: (i, 0)),
                   pl.BlockSpec((MLA_WIDTH, tm), lambda i: (0, i)),
                   pl.BlockSpec((tm, CONV_CH), lambda i: (i, 0))],
        out_shape=[jax.ShapeDtypeStruct((MLA_HEADS * HEAD_PAD, n), BF16),
                   jax.ShapeDtypeStruct((n, MLA_HEADS * HEAD_PAD), BF16),
                   jax.ShapeDtypeStruct((MLA_WIDTH, n), BF16),
                   jax.ShapeDtypeStruct((n, CONV_CH), BF16)],
        scratch_shapes=[pltpu.VMEM((CONV_HALO + tm + SUBLANES, CONV_CH), F32),
                        pltpu.VMEM((SUBLANES - 1, CONV_HALO + tm, CONV_CH), F32)],
        compiler_params=_cparams(("arbitrary",)),
        name="mix_in",
    )(x2d, pos_row, *consts)


ATTN_GROUP = 2


TK_ATTN = 1024
DENOM_ROWS = 16


def _attn_kernel(q_ref, k_ref, v_ref, o_ref, *s_scr, tq):
    i = pl.program_id(1)
    tk = min(TK_ATTN, tq)
    per_q = tq // tk
    key_c = lax.broadcasted_iota(I32, (tk, tq), 0) // CHUNK
    qry_c = lax.broadcasted_iota(I32, (tk, tq), 1) // CHUNK

    for g in range(MLA_HEADS // ATTN_GROUP):
        heads = list(range(g * ATTN_GROUP, (g + 1) * ATTN_GROUP))
        qts = [q_ref[hd * HEAD_PAD:(hd + 1) * HEAD_PAD, :] for hd in heads]

        def step(j, carry, diag):
            start = pl.multiple_of(j * tk, tk)
            new = []
            for n_, hd in enumerate(heads):
                m, acc = carry[n_]
                kj = k_ref[pl.ds(start, tk), hd * HEAD_PAD:(hd + 1) * HEAD_PAD]
                vj = jnp.concatenate([v_ref[hd * MLA_V:(hd + 1) * MLA_V, pl.ds(start, tk)],
                                      jnp.ones((DENOM_ROWS, tk), BF16)], axis=0)
                s = jnp.dot(kj, qts[n_], preferred_element_type=F32)
                if diag is not None:
                    s = jnp.where(key_c + diag * (tk // CHUNK) <= qry_c, s, -jnp.inf)
                m_new = jnp.maximum(m, jnp.max(s, axis=0, keepdims=True))
                s_scr[n_][...] = s
                s = s_scr[n_][...]
                alpha = jnp.exp2(m - m_new)
                p = jnp.exp2(s - m_new)
                acc = alpha * acc + jnp.dot(vj, p.astype(BF16), preferred_element_type=F32)
                new.append((m_new, acc))
            return tuple(new)

        init = tuple((jnp.full((1, tq), -jnp.inf, F32), jnp.zeros((MLA_V + DENOM_ROWS, tq), F32)) for _ in heads)
        carry = lax.fori_loop(0, i * per_q, functools.partial(step, diag=None), init)
        for d in range(per_q):
            carry = step(i * per_q + d, carry, d)
        for n_, hd in enumerate(heads):
            _, acc = carry[n_]
            out = acc[0:MLA_V, :] / acc[MLA_V:MLA_V + 1, :]
            o_ref[:, hd * MLA_V:(hd + 1) * MLA_V] = jnp.transpose(out).astype(BF16)


def _attn(q_t, k, v_t, *, batch, seq):
    tq = min(TQ_ATTN, seq)
    nq = seq // tq
    return pl.pallas_call(
        functools.partial(_attn_kernel, tq=tq),
        grid=(batch, nq),
        in_specs=[pl.BlockSpec((MLA_HEADS * HEAD_PAD, tq), lambda b, i: (0, b * nq + i)),
                  pl.BlockSpec((seq, MLA_HEADS * HEAD_PAD), lambda b, i: (b, 0)),
                  pl.BlockSpec((MLA_WIDTH, seq), lambda b, i: (0, b))],
        out_specs=pl.BlockSpec((tq, MLA_WIDTH), lambda b, i: (b * nq + i, 0)),
        out_shape=jax.ShapeDtypeStruct((batch * seq, MLA_WIDTH), BF16),
        scratch_shapes=[pltpu.VMEM((min(TK_ATTN, tq), tq), F32) for _ in range(ATTN_GROUP)],
        compiler_params=_cparams(("arbitrary", "arbitrary")),
        name="mla_attn",
    )(q_t, k, v_t)


def _mem_kv_kernel(mem_ref, g_ref, w_ref, kv_ref):
    mn = _rms(mem_ref[...], g_ref[...]).astype(BF16)
    kv_ref[...] = jnp.dot(mn, w_ref[...], preferred_element_type=F32).astype(BF16)


def _mem_kv(mem2d, g_mem, w_xkv, *, batch, mem_len):
    return pl.pallas_call(
        _mem_kv_kernel,
        grid=(batch,),
        in_specs=[pl.BlockSpec((mem_len, D_MODEL), lambda b: (b, 0)),
                  pl.BlockSpec((1, D_MODEL), lambda b: (0, 0)),
                  pl.BlockSpec((D_MODEL, 2 * D_MODEL), lambda b: (0, 0))],
        out_specs=pl.BlockSpec((mem_len, 2 * D_MODEL), lambda b: (b, 0)),
        out_shape=jax.ShapeDtypeStruct((batch * mem_len, 2 * D_MODEL), BF16),
        compiler_params=_cparams(("arbitrary",)),
        name="mem_kv",
    )(mem2d, g_mem, w_xkv)


def _mid_kernel(x_ref, ya_ref, yc_ref, kv_ref, woa_ref, wob_ref, gx_ref, wxq_ref, wxo_ref, gf_ref,
                wrh_ref, wrl_ref, br_ref, tri_ref,
                x2_ref, h3_ref, idx_ref, gate_ref, rank_ref, cnt_ref, runs_ref, run_ref, *, tm):
    i = pl.program_id(0)

    @pl.when(i == 0)
    def _():
        run_ref[...] = jnp.zeros_like(run_ref)

    nt = (((1,), (1,)), ((), ()))
    half_d = D_MODEL // 2
    rows = tm // MID_SPLIT
    lgs = []
    for part in range(MID_SPLIT):
        rs = slice(part * rows, (part + 1) * rows)
        x1 = (x_ref[rs, :]
              + jnp.dot(ya_ref[rs, :], woa_ref[...], preferred_element_type=F32)
              + jnp.dot(yc_ref[rs, :], wob_ref[...], preferred_element_type=F32))
        h = _rms(x1, gx_ref[...]).astype(BF16)
        q = jnp.dot(h, wxq_ref[...], preferred_element_type=F32).astype(BF16)
        os = []
        for hd in range(X_HEADS):
            sl = slice(hd * X_HEAD_DIM, (hd + 1) * X_HEAD_DIM)
            vsl = slice(D_MODEL + hd * X_HEAD_DIM, D_MODEL + (hd + 1) * X_HEAD_DIM)
            s = lax.dot_general(q[:, sl], kv_ref[:, sl], nt, preferred_element_type=F32) * (X_HEAD_DIM ** -0.5)
            e = jnp.exp(s - jnp.max(s, axis=-1, keepdims=True))
            p = (e / jnp.sum(e, axis=-1, keepdims=True)).astype(BF16)
            os.append(jnp.dot(p, kv_ref[:, vsl], preferred_element_type=F32).astype(BF16))
        o = jnp.concatenate(os, axis=-1)
        x2 = x1 + jnp.dot(o, wxo_ref[...], preferred_element_type=F32)
        x2_ref[rs, :] = x2
        h3 = _rms(x2, gf_ref[...])
        hh = h3.astype(BF16)
        hf = hh.astype(F32)
        word = (lax.shift_right_logical(pltpu.bitcast(hf[:, :half_d], jnp.uint32), jnp.uint32(16))
                | (pltpu.bitcast(hf[:, half_d:], jnp.uint32) & jnp.uint32(HI_MASK)))
        for t in range(PACK_TILES):
            h3_ref[pl.ds(part * rows * PACK_TILES + t, rows, stride=PACK_TILES), :] = word[:, t * LANES:(t + 1) * LANES]

        hl = (h3 - hf).astype(BF16)
        lgs.append(jnp.dot(hh, wrh_ref[...], preferred_element_type=F32)
                   + (jnp.dot(hh, wrl_ref[...], preferred_element_type=F32)
                      + jnp.dot(hl, wrh_ref[...], preferred_element_type=F32)))
    lg = jnp.concatenate(lgs, axis=0)
    lgt = jnp.transpose(lg)[0:N_EXPERTS, :] + br_ref[...]
    eid = lax.broadcasted_iota(I32, (N_EXPERTS, tm), 0)
    vals, hots = [], []
    cur = lgt
    for k in range(TOP_K):
        mx = jnp.max(cur, axis=0, keepdims=True)
        ik = jnp.min(jnp.where(cur == mx, eid, N_EXPERTS), axis=0, keepdims=True)
        hot = eid == ik
        cur = jnp.where(hot, -jnp.inf, cur)
        idx_ref[k:k + 1, :] = ik
        vals.append(mx)
        hots.append(hot)
    es = [jnp.exp(v - vals[0]) for v in vals]
    den = es[0] + es[1] + es[2] + es[3]
    for k in range(TOP_K):
        gate_ref[k:k + 1, :] = es[k] / den
    cnt = jnp.zeros((N_EXPERTS, tm), F32)
    for k in range(TOP_K):
        cnt = cnt + hots[k].astype(F32)
    run = run_ref[:, 0:1]
    ctm = min(COMB_TM, tm)
    for j in range(tm // ctm):
        before = run if j == 0 else run + jnp.sum(cnt[:, :j * ctm], axis=1, keepdims=True)
        runs_ref[j] = jnp.broadcast_to(before, (N_EXPERTS, LANES)).astype(I32)
    tot = jnp.dot(cnt.astype(BF16), tri_ref[...], preferred_element_type=F32) + run
    for k in range(TOP_K):
        rank_ref[k:k + 1, :] = jnp.sum(jnp.where(hots[k], tot, 0.0), axis=0, keepdims=True).astype(I32)
    new_run = run + jnp.sum(cnt, axis=1, keepdims=True)
    run_ref[...] = jnp.broadcast_to(new_run, run_ref.shape)
    cnt_ref[...] = jnp.broadcast_to(new_run, cnt_ref.shape).astype(I32)


def _mid(x2d, ya, yc, kv, w_oa, w_ob, g_x, w_xq, w_xo, g_f, wr_h, wr_l, b_r, tri, *, seq, mem_len, tm):
    n = x2d.shape[0]
    tpb = seq // tm
    full = lambda a: pl.BlockSpec(a.shape, lambda i: (0,) * a.ndim)
    consts_a = [w_oa, w_ob, g_x, w_xq, w_xo, g_f, wr_h, wr_l, b_r, tri]
    return pl.pallas_call(
        functools.partial(_mid_kernel, tm=tm),
        grid=(n // tm,),
        in_specs=[pl.BlockSpec((tm, D_MODEL), lambda i: (i, 0)),
                  pl.BlockSpec((tm, MLA_WIDTH), lambda i: (i, 0)),
                  pl.BlockSpec((tm, CONV_CH), lambda i: (i, 0)),
                  pl.BlockSpec((mem_len, 2 * D_MODEL), lambda i: (i // tpb, 0))] + [full(a) for a in consts_a],
        out_specs=[pl.BlockSpec((tm, D_MODEL), lambda i: (i, 0)),
                   pl.BlockSpec((tm * PACK_TILES, LANES), lambda i: (i, 0)),
                   pl.BlockSpec((TOP_K, tm), lambda i: (0, i)),
                   pl.BlockSpec((TOP_K, tm), lambda i: (0, i)),
                   pl.BlockSpec((TOP_K, tm), lambda i: (0, i)),
                   pl.BlockSpec((N_EXPERTS, LANES), lambda i: (0, 0)),
                   pl.BlockSpec((tm // min(COMB_TM, tm), N_EXPERTS, LANES), lambda i: (i, 0, 0))],
        out_shape=[jax.ShapeDtypeStruct((n, D_MODEL), F32),
                   jax.ShapeDtypeStruct((n * PACK_TILES, LANES), jnp.uint32),
                   jax.ShapeDtypeStruct((TOP_K, n), I32),
                   jax.ShapeDtypeStruct((TOP_K, n), F32),
                   jax.ShapeDtypeStruct((TOP_K, n), I32),
                   jax.ShapeDtypeStruct((N_EXPERTS, LANES), I32),
                   jax.ShapeDtypeStruct((n // min(COMB_TM, tm), N_EXPERTS, LANES), I32)],
        scratch_shapes=[pltpu.VMEM((N_EXPERTS, LANES), F32)],
        compiler_params=_cparams(("arbitrary",)),
        name="mid",
    )(x2d, ya, yc, kv, *consts_a)


def _pos_kernel(idx_ref, rank_ref, adj_ref, lpos_ref, *, tm, group):
    i = pl.program_id(0)
    eid = lax.broadcasted_iota(I32, (N_EXPERTS, tm), 0)
    for j in range(group):
        half = ((i * group + j) % 2) * _comb_cap(tm)
        cols = slice(j * tm, (j + 1) * tm)
        adj = adj_ref[j][:, 0:1] + half
        for k in range(TOP_K):
            hot = eid == idx_ref[k:k + 1, cols]
            lpos_ref[k:k + 1, cols] = (jnp.sum(jnp.where(hot, adj, 0), axis=0, keepdims=True)
                                       + rank_ref[k:k + 1, cols]) * ROW_TILES


def _pos(idx, rank, adj_l, *, tm):
    n = idx.shape[1]
    n_tiles = n // tm
    group = 4 if n_tiles % 4 == 0 else 1
    return pl.pallas_call(
        functools.partial(_pos_kernel, tm=tm, group=group),
        grid=(n_tiles // group,),
        in_specs=[pl.BlockSpec((TOP_K, tm * group), lambda i: (0, i)),
                  pl.BlockSpec((TOP_K, tm * group), lambda i: (0, i)),
                  pl.BlockSpec((group, N_EXPERTS, LANES), lambda i: (i, 0, 0))],
        out_specs=pl.BlockSpec((TOP_K, tm * group), lambda i: (0, i)),
        out_shape=jax.ShapeDtypeStruct((TOP_K, n), I32),
        compiler_params=_cparams(("arbitrary",)),
        name="slot_pos",
    )(idx, rank, adj_l)


RUN_BITS = tuple(1 << b for b in range(9, -1, -1))


def _pow2_runs(length, fn):
    off = jnp.int32(0)
    for bit in RUN_BITS:
        take = (length & bit) != 0

        @pl.when(take)
        def _(off=off, bit=bit):
            fn(off, bit)

        off = off + jnp.where(take, bit, 0)


def _dispatch_kernel(pad_start_ref, pad_cnt_ref, cstart_ref, clen_ref, cloc_ref, lpos_ref, h3_ref, xs_hbm,
                     cbuf, sems, zsem, *, tm, n_tiles):
    i = pl.program_id(0)
    slot = i % 2
    cap = _comb_cap(tm)
    rows_per_tile = TOP_K * tm

    @pl.when(i == 0)
    def _():
        zrows = RUN_BITS[0]
        cbuf[pl.ds(cap, zrows)] = jnp.zeros((zrows, PACK_TILES, LANES), jnp.uint32)

        def pad_pass(start):
            def per_expert(e, c):
                def one(off, bit):
                    cp = pltpu.make_async_copy(cbuf.at[pl.ds(cap, bit)],
                                               xs_hbm.at[pl.ds(pad_start_ref[e] + off, bit)], zsem)
                    cp.start() if start else cp.wait()

                _pow2_runs(pad_cnt_ref[e], one)
                return c

            lax.fori_loop(0, N_EXPERTS, per_expert, 0)

        pad_pass(True)
        pad_pass(False)

    def wait_tile(s):
        pltpu.make_async_copy(cbuf.at[pl.ds(s * cap, rows_per_tile)], xs_hbm.at[pl.ds(0, rows_per_tile)],
                              sems.at[s]).wait()

    @pl.when(i >= 2)
    def _():
        wait_tile(slot)

    def tok_group(tt, c):
        for u in range(COMB_UNROLL):
            t = tt * COMB_UNROLL + u
            row = h3_ref[t]
            for k in range(TOP_K):
                cbuf[lax.shift_right_logical(lpos_ref[k * tm + t], 3)] = row
        return c

    lax.fori_loop(0, tm // COMB_UNROLL, tok_group, 0)

    def per_run(e, c):
        src0 = slot * cap + cloc_ref[i * N_EXPERTS + e]
        dst0 = cstart_ref[i * N_EXPERTS + e]

        def one(off, bit):
            pltpu.make_async_copy(cbuf.at[pl.ds(src0 + off, bit)], xs_hbm.at[pl.ds(dst0 + off, bit)],
                                  sems.at[slot]).start()

        _pow2_runs(clen_ref[i * N_EXPERTS + e], one)
        return c

    lax.fori_loop(0, N_EXPERTS, per_run, 0)

    @pl.when(i == n_tiles - 1)
    def _():
        wait_tile(slot)
        if n_tiles >= 2:
            wait_tile(1 - slot)


def _dispatch(pad_start, pad_cnt, cstart, clen, cloc, lpos_tok, h3r, *, n_slots, tm):
    n = h3r.shape[0]
    n_tiles = n // tm
    gs = pltpu.PrefetchScalarGridSpec(
        num_scalar_prefetch=5,
        grid=(n_tiles,),
        in_specs=[pl.BlockSpec((TOP_K * tm,), lambda i, *_: (i,), memory_space=pltpu.SMEM),
                  pl.BlockSpec((tm, PACK_TILES, LANES), lambda i, *_: (i, 0, 0))],
        out_specs=pl.BlockSpec(memory_space=pl.ANY),
        scratch_shapes=[pltpu.VMEM((2 * _comb_cap(tm), PACK_TILES, LANES), jnp.uint32),
                        pltpu.SemaphoreType.DMA((2,)),
                        pltpu.SemaphoreType.DMA(())],
    )
    return pl.pallas_call(
        functools.partial(_dispatch_kernel, tm=tm, n_tiles=n_tiles),
        grid_spec=gs,
        out_shape=jax.ShapeDtypeStruct((n_slots, PACK_TILES, LANES), jnp.uint32),
        compiler_params=_cparams(("arbitrary",)),
        name="dispatch",
    )(pad_start, pad_cnt, cstart, clen, cloc, lpos_tok, h3r)


def _expert_kernel(blk_e_ref, nused_ref, x_ref, wu_ref, bu_ref, wd_ref, bd_ref, y_ref, wu_bf, wd_bf, *, tb):
    b = pl.program_id(0)

    @pl.when(b < nused_ref[0])
    def _():
        prev = blk_e_ref[jnp.maximum(b - 1, 0)]

        @pl.when(jnp.logical_or(b == 0, blk_e_ref[b] != prev))
        def _():
            wu_bf[...] = wu_ref[...].astype(BF16)
            wd_bf[...] = wd_ref[...].astype(BF16)

        words = [x_ref[pl.ds(t, tb, stride=PACK_TILES), :] for t in range(PACK_TILES)]
        lo = [pltpu.bitcast(lax.shift_left(w, jnp.uint32(16)), F32) for w in words]
        hi = [pltpu.bitcast(w & jnp.uint32(HI_MASK), F32) for w in words]
        x = jnp.concatenate(lo + hi, axis=-1).astype(BF16)
        up = jnp.dot(x, wu_bf[...], preferred_element_type=F32) + bu_ref[...]
        glu = jnp.minimum(up[:, :D_FF], SWIGLU_LIMIT)
        lin = jnp.clip(up[:, D_FF:], -SWIGLU_LIMIT, SWIGLU_LIMIT)
        act = (glu * jax.nn.sigmoid(SWIGLU_ALPHA * glu) * (lin + 1.0)).astype(BF16)
        y = jnp.dot(act, wd_bf[...], preferred_element_type=F32) + bd_ref[...]
        for t in range(ROW_TILES):
            y_ref[pl.ds(t, tb, stride=ROW_TILES), :] = y[:, t * LANES:(t + 1) * LANES]


def _expert(blk_e, nused, xs, w_up, b_up, w_down, b_down, *, tb):
    n_slots = xs.shape[0] // PACK_TILES
    nb = n_slots // tb
    blk = lambda b, be, nu: jnp.minimum(b, nu[0] - 1)
    gs = pltpu.PrefetchScalarGridSpec(
        num_scalar_prefetch=2,
        grid=(nb,),
        in_specs=[pl.BlockSpec((tb * PACK_TILES, LANES), lambda b, be, nu: (blk(b, be, nu), 0)),
                  pl.BlockSpec((None, D_MODEL, 2 * D_FF), lambda b, be, nu: (be[blk(b, be, nu)], 0, 0)),
                  pl.BlockSpec((None, 1, 2 * D_FF), lambda b, be, nu: (be[blk(b, be, nu)], 0, 0)),
                  pl.BlockSpec((None, D_FF, D_MODEL), lambda b, be, nu: (be[blk(b, be, nu)], 0, 0)),
                  pl.BlockSpec((None, 1, D_MODEL), lambda b, be, nu: (be[blk(b, be, nu)], 0, 0))],
        out_specs=pl.BlockSpec((tb * ROW_TILES, LANES), lambda b, be, nu: (blk(b, be, nu), 0)),
        scratch_shapes=[pltpu.VMEM((D_MODEL, 2 * D_FF), BF16),
                        pltpu.VMEM((D_FF, D_MODEL), BF16)],
    )
    return pl.pallas_call(
        functools.partial(_expert_kernel, tb=tb),
        grid_spec=gs,
        out_shape=jax.ShapeDtypeStruct(((n_slots + tb) * ROW_TILES, LANES), F32),
        compiler_params=_cparams(("arbitrary",)),
        name="expert_ffn",
    )(blk_e, nused, xs, w_up, b_up, w_down, b_down)


COMB_PIECE = 32
COMB_UNROLL = 8


def _comb_cap(tm):
    return TOP_K * tm + N_EXPERTS * (COMB_PIECE - 1)


def _combine_kernel(cstart_ref, cpieces_ref, cloc_ref, lpos_ref, gate_ref, x2_ref, gfin_ref, y_hbm, out_ref,
                    chunks, rt, sems, *, tm, n_tiles):
    i = pl.program_id(0)
    slot = i % 2
    piece_rows = COMB_PIECE * ROW_TILES

    def piece_copy(tile, e, p, to_slot):
        src = (cstart_ref[tile * N_EXPERTS + e] + p * COMB_PIECE) * ROW_TILES
        dst = (to_slot * _comb_cap(tm) + cloc_ref[tile * N_EXPERTS + e] + p * COMB_PIECE) * ROW_TILES
        return pltpu.make_async_copy(
            y_hbm.at[pl.ds(pl.multiple_of(src, ROW_TILES), piece_rows), :],
            chunks.at[pl.ds(pl.multiple_of(dst, ROW_TILES), piece_rows), :],
            sems.at[to_slot])

    def for_pieces(tile, to_slot, start):
        def per_e(e, c):
            def per_p(p, c2):
                cp = piece_copy(tile, e, p, to_slot)
                if start:
                    cp.start()
                else:
                    cp.wait()
                return c2

            lax.fori_loop(0, cpieces_ref[tile * N_EXPERTS + e], per_p, 0)
            return c

        lax.fori_loop(0, N_EXPERTS, per_e, 0)

    @pl.when(i == 0)
    def _():
        for_pieces(0, 0, True)

    @pl.when(i + 1 < n_tiles)
    def _():
        for_pieces(i + 1, 1 - slot, True)

    for_pieces(i, slot, False)

    def tok_group(tt, c):
        for u in range(COMB_UNROLL):
            t = tt * COMB_UNROLL + u
            acc = None
            for k in range(TOP_K):
                r = pl.multiple_of(lpos_ref[k * tm + t], ROW_TILES)
                term = gate_ref[k * tm + t] * chunks[pl.ds(r, ROW_TILES), :]
                acc = term if acc is None else acc + term
            rt[pl.ds(pl.multiple_of(t * ROW_TILES, ROW_TILES), ROW_TILES), :] = acc
        return c

    lax.fori_loop(0, tm // COMB_UNROLL, tok_group, 0)

    moe = jnp.concatenate([rt[pl.ds(s, tm, stride=ROW_TILES), :] for s in range(ROW_TILES)], axis=-1)
    out_ref[...] = _rms(x2_ref[...] + moe, gfin_ref[...])


def _combine(cstart, cpieces, cloc, lpos, gate, x2, g_final, y, *, tm):
    n = x2.shape[0]
    n_tiles = n // tm
    gs = pltpu.PrefetchScalarGridSpec(
        num_scalar_prefetch=3,
        grid=(n_tiles,),
        in_specs=[pl.BlockSpec((TOP_K * tm,), lambda i, a, b, c: (i,), memory_space=pltpu.SMEM),
                  pl.BlockSpec((TOP_K * tm,), lambda i, a, b, c: (i,), memory_space=pltpu.SMEM),
                  pl.BlockSpec((tm, D_MODEL), lambda i, a, b, c: (i, 0)),
                  pl.BlockSpec((1, D_MODEL), lambda i, a, b, c: (0, 0)),
                  pl.BlockSpec(memory_space=pl.ANY)],
        out_specs=pl.BlockSpec((tm, D_MODEL), lambda i, a, b, c: (i, 0)),
        scratch_shapes=[pltpu.VMEM((2 * _comb_cap(tm) * ROW_TILES, LANES), F32),
                        pltpu.VMEM((tm * ROW_TILES, LANES), F32),
                        pltpu.SemaphoreType.DMA((2,))],
    )
    return pl.pallas_call(
        functools.partial(_combine_kernel, tm=tm, n_tiles=n_tiles),
        grid_spec=gs,
        out_shape=jax.ShapeDtypeStruct((n, D_MODEL), F32),
        compiler_params=_cparams(("arbitrary",)),
        name="combine",
    )(cstart, cpieces, cloc, lpos, gate, x2, g_final, y)


def _head_pad_cols(w, per_head_in, take, place):
    kdim = w.shape[0]
    w3 = w.reshape(kdim, MLA_HEADS, per_head_in)
    out = jnp.zeros((kdim, MLA_HEADS, HEAD_PAD), w.dtype)
    for (t0, t1), p0 in zip(take, place):
        out = out.at[:, :, p0:p0 + (t1 - t0)].set(w3[:, :, t0:t1])
    return out.reshape(kdim, MLA_HEADS * HEAD_PAD)


def kernel(x, mem, positions, g_mix, w_in, g_q, w_uq, g_kv, w_ukv, w_dw, b_dw, g_conv_ln, b_conv_ln, w_out,
           g_xattn, g_mem, w_xq, w_xkv, w_xo, g_ffn, w_router, b_router, w_up, b_up, w_down, b_down, g_final):
    batch, seq, _ = x.shape
    mem_len = mem.shape[1]
    n = batch * seq
    half = MLA_ROPE // 2
    r0 = MLA_NOPE
    assert w_in.shape[0] == 1, "one trunk layer"

    row = lambda v: v.reshape(1, -1).astype(F32)

    wi = w_in[0]
    o1, o2, o3 = Q_LORA, Q_LORA + KV_LORA, Q_LORA + KV_LORA + MLA_ROPE
    kr_blk = jnp.zeros((D_MODEL, HEAD_PAD), F32).at[:, r0:r0 + MLA_ROPE].set(wi[:, o2:o3])
    kr_swp = (jnp.zeros((D_MODEL, HEAD_PAD), F32)
              .at[:, r0:r0 + half].set(wi[:, o2 + half:o3])
              .at[:, r0 + half:r0 + MLA_ROPE].set(wi[:, o2:o2 + half]))
    w_in_p = jnp.concatenate([wi[:, :o2], kr_blk, kr_swp, wi[:, o3:]], axis=1).astype(BF16)
    per_q = MLA_NOPE + MLA_ROPE
    w_uq_p = _head_pad_cols(w_uq[0], per_q, [(0, per_q)], [0]).astype(BF16)
    w_uq_s = _head_pad_cols(w_uq[0], per_q, [(MLA_NOPE + half, per_q), (MLA_NOPE, MLA_NOPE + half)],
                            [r0, r0 + half]).astype(BF16)
    per_kv = MLA_NOPE + MLA_V
    w_uk_k = _head_pad_cols(w_ukv[0], per_kv, [(0, MLA_NOPE)], [0]).astype(BF16)
    w_uk_v = (w_ukv[0].reshape(KV_LORA, MLA_HEADS, per_kv)[:, :, MLA_NOPE:]
              .reshape(KV_LORA, MLA_WIDTH).astype(BF16))
    inv = ROPE_THETA ** (-jnp.arange(0, MLA_ROPE, 2, dtype=F32) / MLA_ROPE)
    inv_c = (jnp.zeros((LANES, 1), F32).at[r0:r0 + half, 0].set(inv).at[r0 + half:r0 + MLA_ROPE, 0].set(inv))
    sgn_c = (jnp.zeros((LANES, 1), F32).at[r0:r0 + half, 0].set(-1.0).at[r0 + half:r0 + MLA_ROPE, 0].set(1.0))

    x2d = x.reshape(n, D_MODEL)
    pos_row = positions.reshape(1, n).astype(I32)

    q_t, k, v_t, y_conv = _mix_in(x2d, pos_row, inv_c, sgn_c, row(g_mix[0]), w_in_p, row(g_q[0]),
                                  jnp.transpose(w_uq_p), jnp.transpose(w_uq_s), row(g_kv[0]), w_uk_k,
                                  jnp.transpose(w_uk_v), w_dw[0].astype(F32), row(b_dw[0]),
                                  row(g_conv_ln[0]), row(b_conv_ln[0]), seq=seq)
    y_mla = _attn(q_t, k, v_t, batch=batch, seq=seq)
    kv = _mem_kv(mem.reshape(batch * mem_len, D_MODEL), row(g_mem[0]), w_xkv[0].astype(BF16),
                 batch=batch, mem_len=mem_len)

    tm_mid = min(TM_MID, seq)
    wr = jnp.zeros((D_MODEL, LANES), F32).at[:, :N_EXPERTS].set(w_router[0])
    wr_h = wr.astype(BF16)
    wr_l = (wr - wr_h.astype(F32)).astype(BF16)
    b_r = b_router[0].reshape(N_EXPERTS, 1).astype(F32)
    tri = (lax.broadcasted_iota(I32, (tm_mid, tm_mid), 0)
           < lax.broadcasted_iota(I32, (tm_mid, tm_mid), 1)).astype(BF16)
    wo = w_out[0].astype(BF16)
    x2, h3r, idx, gate, rank, cnt, runs_l = _mid(
        x2d, y_mla, y_conv, kv, wo[:MLA_WIDTH], wo[MLA_WIDTH:], row(g_xattn[0]), w_xq[0].astype(BF16),
        w_xo[0].astype(BF16), row(g_ffn[0]), wr_h, wr_l, b_r, tri, seq=seq, mem_len=mem_len, tm=tm_mid)

    tb = TB_EXPERT
    counts = cnt[:, 0]
    padded = (counts + tb - 1) // tb * tb
    pend = jnp.cumsum(padded)
    pstart = pend - padded
    n_slots = n * TOP_K + N_EXPERTS * tb
    nb = n_slots // tb
    blk_first = jnp.arange(nb, dtype=I32) * tb
    blk_e = jnp.minimum(jnp.sum((pend[None, :] <= blk_first[:, None]).astype(I32), axis=1),
                        N_EXPERTS - 1).astype(I32)
    nused = (pend[-1:] // tb).astype(I32)

    runs = runs_l[:, :, 0]
    run_len = jnp.concatenate([runs[1:], counts[None, :]], axis=0) - runs
    cstart = pstart[None, :] + runs
    cpieces = (run_len + COMB_PIECE - 1) // COMB_PIECE
    cloc = jnp.cumsum(cpieces * COMB_PIECE, axis=1) - cpieces * COMB_PIECE
    adj_l = jnp.broadcast_to((cloc - runs)[:, :, None], runs_l.shape).astype(I32)

    tm_comb = min(COMB_TM, tm_mid)
    lpos = _pos(idx, rank, adj_l, tm=tm_comb)
    tiled = lambda a: a.reshape(TOP_K, n // tm_comb, tm_comb).transpose(1, 0, 2).reshape(-1)
    lpos_tok = tiled(lpos)
    flat = lambda a: a.reshape(-1).astype(I32)
    xs = _dispatch((pstart + counts).astype(I32), (padded - counts).astype(I32), flat(cstart), flat(run_len),
                   flat(cloc), lpos_tok, h3r.reshape(n, PACK_TILES, LANES), n_slots=n_slots, tm=tm_comb)
    y = _expert(blk_e, nused, xs.reshape(n_slots * PACK_TILES, LANES), w_up[0],
                b_up[0].reshape(N_EXPERTS, 1, 2 * D_FF), w_down[0], b_down[0].reshape(N_EXPERTS, 1, D_MODEL), tb=tb)
    out = _combine(flat(cstart), flat(cpieces), flat(cloc), lpos_tok, tiled(gate),
                   x2, row(g_final), y, tm=tm_comb)
    return out.reshape(batch, seq, D_MODEL)
```

```python
import functools

import jax
import jax.numpy as jnp
from jax import lax
from jax.experimental import pallas as pl
from jax.experimental.pallas import tpu as pltpu

F32 = jnp.float32
BF16 = jnp.bfloat16
I32 = jnp.int32

D_MODEL = 1024
MLA_HEADS = 8
MLA_NOPE = 64
MLA_ROPE = 32
MLA_V = 64
MLA_WIDTH = MLA_HEADS * MLA_V
Q_LORA = 256
KV_LORA = 128
ROPE_THETA = 10000.0
CHUNK = 64
CONV_CH = 512
CONV_WIDTH = 31
X_HEADS = 4
X_HEAD_DIM = D_MODEL // X_HEADS
N_EXPERTS = 32
TOP_K = 4
D_FF = D_MODEL
SWIGLU_LIMIT = 7.0
SWIGLU_ALPHA = 1.702
NORM_EPS = 1e-5

LANES = 128
SUBLANES = 8
ROW_TILES = D_MODEL // LANES
PACK_TILES = ROW_TILES // 2
VMEM_LIMIT_BYTES = 56 * 1024 * 1024

HEAD_PAD = LANES
Q_PRESCALE = (MLA_NOPE + MLA_ROPE) ** -0.5 * 1.4426950408889634
CONV_HALO = 32

TM_MIX = 512
TQ_ATTN = 1024
TM_MID = 512
TB_EXPERT = 512
MID_SPLIT = 2
COMB_TM = 512


def _rms(x, g):
    return x * lax.rsqrt(jnp.mean(x * x, axis=-1, keepdims=True) + NORM_EPS) * g


def _cparams(sem):
    return pltpu.CompilerParams(dimension_semantics=sem, vmem_limit_bytes=VMEM_LIMIT_BYTES)


def _mix_in_kernel(x_ref, pos_ref, inv_ref, sgn_ref, gmix_ref, win_ref, gq_ref, wuq_ref, wuqs_ref,
                   gkv_ref, wukk_ref, wukv_ref, wdw_ref, bdw_ref, gln_ref, bln_ref,
                   q_ref, k_ref, v_ref, yc_ref, zbuf, zsh, *, tiles_per_batch, tm):
    i = pl.program_id(0)
    x = x_ref[...]
    h = _rms(x, gmix_ref[...]).astype(BF16)
    u = jnp.dot(h, win_ref[...], preferred_element_type=F32)
    c0 = Q_LORA + KV_LORA
    c1 = c0 + 2 * HEAD_PAD
    cq = u[:, 0:Q_LORA]
    ckv = u[:, Q_LORA:c0]
    kr = u[:, c0:c0 + HEAD_PAD]
    krs = u[:, c0 + HEAD_PAD:c1]
    a = u[:, c1:c1 + CONV_CH]
    gate = u[:, c1 + CONV_CH:c1 + 2 * CONV_CH]

    r0, r1 = MLA_NOPE, MLA_NOPE + MLA_ROPE
    ang_r = inv_ref[r0:r1, :] * pos_ref[...].astype(F32)
    cos_t = jnp.concatenate([jnp.ones((r0, tm), F32), jnp.cos(ang_r), jnp.ones((HEAD_PAD - r1, tm), F32)], axis=0)
    sin_t = jnp.concatenate([jnp.zeros((r0, tm), F32), jnp.sin(ang_r) * sgn_ref[r0:r1, :],
                             jnp.zeros((HEAD_PAD - r1, tm), F32)], axis=0)
    cosb = jnp.transpose(cos_t)
    sinb = jnp.transpose(sin_t)
    cos_q = cos_t * Q_PRESCALE
    sin_q = sin_t * Q_PRESCALE

    nt = (((1,), (1,)), ((), ()))
    cqn = _rms(cq, gq_ref[...]).astype(BF16)
    qm = lax.dot_general(wuq_ref[...], cqn, nt, preferred_element_type=F32)
    qs = lax.dot_general(wuqs_ref[...], cqn, nt, preferred_element_type=F32)
    for hd in range(MLA_HEADS):
        sl = slice(hd * HEAD_PAD, (hd + 1) * HEAD_PAD)
        q_ref[sl, :] = (qm[sl, :] * cos_q + qs[sl, :] * sin_q).astype(BF16)

    ckvn = _rms(ckv, gkv_ref[...]).astype(BF16)
    kk = jnp.dot(ckvn, wukk_ref[...], preferred_element_type=F32)
    v_ref[...] = lax.dot_general(wukv_ref[...], ckvn, nt, preferred_element_type=F32).astype(BF16)
    krot = kr * cosb + krs * sinb
    for hd in range(MLA_HEADS):
        sl = slice(hd * HEAD_PAD, (hd + 1) * HEAD_PAD)
        k_ref[:, sl] = (kk[:, sl] + krot).astype(BF16)

    z = a * jax.nn.sigmoid(gate)

    @pl.when(i % tiles_per_batch == 0)
    def _():
        zbuf[0:CONV_HALO, :] = jnp.zeros((CONV_HALO, CONV_CH), F32)

    zbuf[CONV_HALO:CONV_HALO + tm, :] = z
    off = CONV_HALO - (CONV_WIDTH - 1)
    rows = 32
    span = tm + CONV_HALO - SUBLANES
    step = 64
    for r in range(1, SUBLANES):
        for c0 in range(0, span, step):
            cl = min(step, span - c0)
            zsh[r - 1, c0:c0 + cl, :] = zbuf[c0 + r:c0 + r + cl, :]
    for r0 in range(0, tm, rows):
        acc = jnp.zeros((rows, CONV_CH), F32) + bdw_ref[...]
        for j in range(CONV_WIDTH):
            a8, ph = divmod(j + off, SUBLANES)
            lo_row = r0 + a8 * SUBLANES
            tap = zbuf[lo_row:lo_row + rows, :] if ph == 0 else zsh[ph - 1, lo_row:lo_row + rows, :]
            acc = acc + wdw_ref[j:j + 1, :] * tap
        mu = jnp.mean(acc, axis=-1, keepdims=True)
        cen = acc - mu
        var = jnp.mean(cen * cen, axis=-1, keepdims=True)
        y = cen * lax.rsqrt(var + NORM_EPS) * gln_ref[...] + bln_ref[...]
        yc_ref[r0:r0 + rows, :] = (y * jax.nn.sigmoid(y)).astype(BF16)
    zbuf[0:CONV_HALO, :] = zbuf[tm:tm + CONV_HALO, :]


def _mix_in(x2d, pos_row, inv_c, sgn_c, g_mix, w_in_p, g_q, w_uq_t, w_uq_st, g_kv, w_uk_k, w_uk_vt,
            w_dw, b_dw, g_ln, b_ln, *, seq):
    n = x2d.shape[0]
    tm = min(TM_MIX, seq)
    full = lambda a: pl.BlockSpec(a.shape, lambda i: (0,) * a.ndim)
    consts = [inv_c, sgn_c, g_mix, w_in_p, g_q, w_uq_t, w_uq_st, g_kv, w_uk_k, w_uk_vt, w_dw, b_dw, g_ln, b_ln]
    return pl.pallas_call(
        functools.partial(_mix_in_kernel, tiles_per_batch=seq // tm, tm=tm),
        grid=(n // tm,),
        in_specs=[pl.BlockSpec((tm, D_MODEL), lambda i: (i, 0)),
                  pl.BlockSpec((1, tm), lambda i: (0, i))] + [full(a) for a in consts],
        out_specs=[pl.BlockSpec((MLA_HEADS * HEAD_PAD, tm), lambda i: (0, i)),
                   pl.BlockSpec((tm, MLA_HEADS * HEAD_PAD), lambda i: (i, 0)),
                   pl.BlockSpec((MLA_WIDTH, tm), lambda i: (0, i)),
                   pl.BlockSpec((tm, CONV_CH), lambda i: (i, 0))],
        out_shape=[jax.ShapeDtypeStruct((MLA_HEADS * HEAD_PAD, n), BF16),
                   jax.ShapeDtypeStruct((n, MLA_HEADS * HEAD_PAD), BF16),
                   jax.ShapeDtypeStruct((MLA_WIDTH, n), BF16),
                   jax.ShapeDtypeStruct((n, CONV_CH), BF16)],
        scratch_shapes=[pltpu.VMEM((CONV_HALO + tm + SUBLANES, CONV_CH), F32),
                        pltpu.VMEM((SUBLANES - 1, CONV_HALO + tm, CONV_CH), F32)],
        compiler_params=_cparams(("arbitrary",)),
        name="mix_in",
    )(x2d, pos_row, *consts)


ATTN_GROUP = 2


TK_ATTN = 1024
DENOM_ROWS = 16


def _attn_kernel(q_ref, k_ref, v_ref, o_ref, *s_scr, tq):
    i = pl.program_id(1)
    tk = min(TK_ATTN, tq)
    per_q = tq // tk
    key_c = lax.broadcasted_iota(I32, (tk, tq), 0) // CHUNK
    qry_c = lax.broadcasted_iota(I32, (tk, tq), 1) // CHUNK

    for g in range(MLA_HEADS // ATTN_GROUP):
        heads = list(range(g * ATTN_GROUP, (g + 1) * ATTN_GROUP))
        qts = [q_ref[hd * HEAD_PAD:(hd + 1) * HEAD_PAD, :] for hd in heads]

        def step(j, carry, diag):
            start = pl.multiple_of(j * tk, tk)
            new = []
            for n_, hd in enumerate(heads):
                m, acc = carry[n_]
                kj = k_ref[pl.ds(start, tk), hd * HEAD_PAD:(hd + 1) * HEAD_PAD]
                vj = jnp.concatenate([v_ref[hd * MLA_V:(hd + 1) * MLA_V, pl.ds(start, tk)],
                                      jnp.ones((DENOM_ROWS, tk), BF16)], axis=0)
                s = jnp.dot(kj, qts[n_], preferred_element_type=F32)
                if diag is not None:
                    s = jnp.where(key_c + diag * (tk // CHUNK) <= qry_c, s, -jnp.inf)
                m_new = jnp.maximum(m, jnp.max(s, axis=0, keepdims=True))
                s_scr[n_][...] = s
                s = s_scr[n_][...]
                alpha = jnp.exp2(m - m_new)
                p = jnp.exp2(s - m_new)
                acc = alpha * acc + jnp.dot(vj, p.astype(BF16), preferred_element_type=F32)
                new.append((m_new, acc))
            return tuple(new)

        init = tuple((jnp.full((1, tq), -jnp.inf, F32), jnp.zeros((MLA_V + DENOM_ROWS, tq), F32)) for _ in heads)
        carry = lax.fori_loop(0, i * per_q, functools.partial(step, diag=None), init)
        for d in range(per_q):
            carry = step(i * per_q + d, carry, d)
        for n_, hd in enumerate(heads):
            _, acc = carry[n_]
            out = acc[0:MLA_V, :] / acc[MLA_V:MLA_V + 1, :]
            o_ref[:, hd * MLA_V:(hd + 1) * MLA_V] = jnp.transpose(out).astype(BF16)


def _attn(q_t, k, v_t, *, batch, seq):
    tq = min(TQ_ATTN, seq)
    nq = seq // tq
    return pl.pallas_call(
        functools.partial(_attn_kernel, tq=tq),
        grid=(batch, nq),
        in_specs=[pl.BlockSpec((MLA_HEADS * HEAD_PAD, tq), lambda b, i: (0, b * nq + i)),
                  pl.BlockSpec((seq, MLA_HEADS * HEAD_PAD), lambda b, i: (b, 0)),
                  pl.BlockSpec((MLA_WIDTH, seq), lambda b, i: (0, b))],
        out_specs=pl.BlockSpec((tq, MLA_WIDTH), lambda b, i: (b * nq + i, 0)),
        out_shape=jax.ShapeDtypeStruct((batch * seq, MLA_WIDTH), BF16),
        scratch_shapes=[pltpu.VMEM((min(TK_ATTN, tq), tq), F32) for _ in range(ATTN_GROUP)],
        compiler_params=_cparams(("arbitrary", "arbitrary")),
        name="mla_attn",
    )(q_t, k, v_t)


def _mem_kv_kernel(mem_ref, g_ref, w_ref, kv_ref):
    mn = _rms(mem_ref[...], g_ref[...]).astype(BF16)
    kv_ref[...] = jnp.dot(mn, w_ref[...], preferred_element_type=F32).astype(BF16)


def _mem_kv(mem2d, g_mem, w_xkv, *, batch, mem_len):
    return pl.pallas_call(
        _mem_kv_kernel,
        grid=(batch,),
        in_specs=[pl.BlockSpec((mem_len, D_MODEL), lambda b: (b, 0)),
                  pl.BlockSpec((1, D_MODEL), lambda b: (0, 0)),
                  pl.BlockSpec((D_MODEL, 2 * D_MODEL), lambda b: (0, 0))],
        out_specs=pl.BlockSpec((mem_len, 2 * D_MODEL), lambda b: (b, 0)),
        out_shape=jax.ShapeDtypeStruct((batch * mem_len, 2 * D_MODEL), BF16),
        compiler_params=_cparams(("arbitrary",)),
        name="mem_kv",
    )(mem2d, g_mem, w_xkv)


def _mid_kernel(x_ref, ya_ref, yc_ref, kv_ref, woa_ref, wob_ref, gx_ref, wxq_ref, wxo_ref, gf_ref,
                wrh_ref, wrl_ref, br_ref, tri_ref,
                x2_ref, h3_ref, idx_ref, gate_ref, rank_ref, cnt_ref, runs_ref, run_ref, *, tm):
    i = pl.program_id(0)

    @pl.when(i == 0)
    def _():
        run_ref[...] = jnp.zeros_like(run_ref)

    nt = (((1,), (1,)), ((), ()))
    half_d = D_MODEL // 2
    rows = tm // MID_SPLIT
    lgs = []
    for part in range(MID_SPLIT):
        rs = slice(part * rows, (part + 1) * rows)
        x1 = (x_ref[rs, :]
              + jnp.dot(ya_ref[rs, :], woa_ref[...], preferred_element_type=F32)
              + jnp.dot(yc_ref[rs, :], wob_ref[...], preferred_element_type=F32))
        h = _rms(x1, gx_ref[...]).astype(BF16)
        q = jnp.dot(h, wxq_ref[...], preferred_element_type=F32).astype(BF16)
        os = []
        for hd in range(X_HEADS):
            sl = slice(hd * X_HEAD_DIM, (hd + 1) * X_HEAD_DIM)
            vsl = slice(D_MODEL + hd * X_HEAD_DIM, D_MODEL + (hd + 1) * X_HEAD_DIM)
            s = lax.dot_general(q[:, sl], kv_ref[:, sl], nt, preferred_element_type=F32) * (X_HEAD_DIM ** -0.5)
            e = jnp.exp(s - jnp.max(s, axis=-1, keepdims=True))
            p = (e / jnp.sum(e, axis=-1, keepdims=True)).astype(BF16)
            os.append(jnp.dot(p, kv_ref[:, vsl], preferred_element_type=F32).astype(BF16))
        o = jnp.concatenate(os, axis=-1)
        x2 = x1 + jnp.dot(o, wxo_ref[...], preferred_element_type=F32)
        x2_ref[rs, :] = x2
        h3 = _rms(x2, gf_ref[...])
        hh = h3.astype(BF16)
        hf = hh.astype(F32)
        word = pltpu.pack_elementwise([h3[:, :half_d], h3[:, half_d:]], packed_dtype=BF16)
        for t in range(PACK_TILES):
            h3_ref[pl.ds(part * rows * PACK_TILES + t, rows, stride=PACK_TILES), :] = word[:, t * LANES:(t + 1) * LANES]

        hl = (h3 - hf).astype(BF16)
        lgs.append(jnp.dot(hh, wrh_ref[...], preferred_element_type=F32)
                   + (jnp.dot(hh, wrl_ref[...], preferred_element_type=F32)
                      + jnp.dot(hl, wrh_ref[...], preferred_element_type=F32)))
    lg = jnp.concatenate(lgs, axis=0)
    lgt = jnp.transpose(lg)[0:N_EXPERTS, :] + br_ref[...]
    eid = lax.broadcasted_iota(I32, (N_EXPERTS, tm), 0)
    vals, hots = [], []
    cur = lgt
    for k in range(TOP_K):
        mx = jnp.max(cur, axis=0, keepdims=True)
        ik = jnp.min(jnp.where(cur == mx, eid, N_EXPERTS), axis=0, keepdims=True)
        hot = eid == ik
        cur = jnp.where(hot, -jnp.inf, cur)
        idx_ref[k:k + 1, :] = ik
        vals.append(mx)
        hots.append(hot)
    es = [jnp.exp(v - vals[0]) for v in vals]
    den = es[0] + es[1] + es[2] + es[3]
    for k in range(TOP_K):
        gate_ref[k:k + 1, :] = es[k] / den
    cnt = jnp.zeros((N_EXPERTS, tm), F32)
    for k in range(TOP_K):
        cnt = cnt + hots[k].astype(F32)
    run = run_ref[:, 0:1]
    ctm = min(COMB_TM, tm)
    for j in range(tm // ctm):
        before = run if j == 0 else run + jnp.sum(cnt[:, :j * ctm], axis=1, keepdims=True)
        runs_ref[j] = jnp.broadcast_to(before, (N_EXPERTS, LANES)).astype(I32)
    tot = jnp.dot(cnt.astype(BF16), tri_ref[...], preferred_element_type=F32) + run
    for k in range(TOP_K):
        rank_ref[k:k + 1, :] = jnp.sum(jnp.where(hots[k], tot, 0.0), axis=0, keepdims=True).astype(I32)
    new_run = run + jnp.sum(cnt, axis=1, keepdims=True)
    run_ref[...] = jnp.broadcast_to(new_run, run_ref.shape)
    cnt_ref[...] = jnp.broadcast_to(new_run, cnt_ref.shape).astype(I32)


def _mid(x2d, ya, yc, kv, w_oa, w_ob, g_x, w_xq, w_xo, g_f, wr_h, wr_l, b_r, tri, *, seq, mem_len, tm):
    n = x2d.shape[0]
    tpb = seq // tm
    full = lambda a: pl.BlockSpec(a.shape, lambda i: (0,) * a.ndim)
    consts_a = [w_oa, w_ob, g_x, w_xq, w_xo, g_f, wr_h, wr_l, b_r, tri]
    return pl.pallas_call(
        functools.partial(_mid_kernel, tm=tm),
        grid=(n // tm,),
        in_specs=[pl.BlockSpec((tm, D_MODEL), lambda i: (i, 0)),
                  pl.BlockSpec((tm, MLA_WIDTH), lambda i: (i, 0)),
                  pl.BlockSpec((tm, CONV_CH), lambda i: (i, 0)),
                  pl.BlockSpec((mem_len, 2 * D_MODEL), lambda i: (i // tpb, 0))] + [full(a) for a in consts_a],
        out_specs=[pl.BlockSpec((tm, D_MODEL), lambda i: (i, 0)),
                   pl.BlockSpec((tm * PACK_TILES, LANES), lambda i: (i, 0)),
                   pl.BlockSpec((TOP_K, tm), lambda i: (0, i)),
                   pl.BlockSpec((TOP_K, tm), lambda i: (0, i)),
                   pl.BlockSpec((TOP_K, tm), lambda i: (0, i)),
                   pl.BlockSpec((N_EXPERTS, LANES), lambda i: (0, 0)),
                   pl.BlockSpec((tm // min(COMB_TM, tm), N_EXPERTS, LANES), lambda i: (i, 0, 0))],
        out_shape=[jax.ShapeDtypeStruct((n, D_MODEL), F32),
                   jax.ShapeDtypeStruct((n * PACK_TILES, LANES), jnp.uint32),
                   jax.ShapeDtypeStruct((TOP_K, n), I32),
                   jax.ShapeDtypeStruct((TOP_K, n), F32),
                   jax.ShapeDtypeStruct((TOP_K, n), I32),
                   jax.ShapeDtypeStruct((N_EXPERTS, LANES), I32),
                   jax.ShapeDtypeStruct((n // min(COMB_TM, tm), N_EXPERTS, LANES), I32)],
        scratch_shapes=[pltpu.VMEM((N_EXPERTS, LANES), F32)],
        compiler_params=_cparams(("arbitrary",)),
        name="mid",
    )(x2d, ya, yc, kv, *consts_a)


def _pos_kernel(idx_ref, rank_ref, adj_ref, lpos_ref, *, tm, group):
    i = pl.program_id(0)
    eid = lax.broadcasted_iota(I32, (N_EXPERTS, tm), 0)
    for j in range(group):
        half = ((i * group + j) % 2) * _comb_cap(tm)
        cols = slice(j * tm, (j + 1) * tm)
        adj = adj_ref[j][:, 0:1] + half
        for k in range(TOP_K):
            hot = eid == idx_ref[k:k + 1, cols]
            lpos_ref[k:k + 1, cols] = (jnp.sum(jnp.where(hot, adj, 0), axis=0, keepdims=True)
                                       + rank_ref[k:k + 1, cols]) * ROW_TILES


def _pos(idx, rank, adj_l, *, tm):
    n = idx.shape[1]
    n_tiles = n // tm
    group = 4 if n_tiles % 4 == 0 else 1
    return pl.pallas_call(
        functools.partial(_pos_kernel, tm=tm, group=group),
        grid=(n_tiles // group,),
        in_specs=[pl.BlockSpec((TOP_K, tm * group), lambda i: (0, i)),
                  pl.BlockSpec((TOP_K, tm * group), lambda i: (0, i)),
                  pl.BlockSpec((group, N_EXPERTS, LANES), lambda i: (i, 0, 0))],
        out_specs=pl.BlockSpec((TOP_K, tm * group), lambda i: (0, i)),
        out_shape=jax.ShapeDtypeStruct((TOP_K, n), I32),
        compiler_params=_cparams(("arbitrary",)),
        name="slot_pos",
    )(idx, rank, adj_l)


RUN_BITS = tuple(1 << b for b in range(9, -1, -1))


def _pow2_runs(length, fn):
    off = jnp.int32(0)
    for bit in RUN_BITS:
        take = (length & bit) != 0

        @pl.when(take)
        def _(off=off, bit=bit):
            fn(off, bit)

        off = off + jnp.where(take, bit, 0)


def _dispatch_kernel(pad_start_ref, pad_cnt_ref, cstart_ref, clen_ref, cloc_ref, lpos_ref, h3_ref, xs_hbm,
                     cbuf, sems, zsem, *, tm, n_tiles):
    i = pl.program_id(0)
    slot = i % 2
    cap = _comb_cap(tm)
    rows_per_tile = TOP_K * tm

    @pl.when(i == 0)
    def _():
        zrows = RUN_BITS[0]
        cbuf[pl.ds(cap, zrows)] = jnp.zeros((zrows, PACK_TILES, LANES), jnp.uint32)

        def pad_pass(start):
            def per_expert(e, c):
                def one(off, bit):
                    cp = pltpu.make_async_copy(cbuf.at[pl.ds(cap, bit)],
                                               xs_hbm.at[pl.ds(pad_start_ref[e] + off, bit)], zsem)
                    cp.start() if start else cp.wait()

                _pow2_runs(pad_cnt_ref[e], one)
                return c

            lax.fori_loop(0, N_EXPERTS, per_expert, 0)

        pad_pass(True)
        pad_pass(False)

    def wait_tile(s):
        pltpu.make_async_copy(cbuf.at[pl.ds(s * cap, rows_per_tile)], xs_hbm.at[pl.ds(0, rows_per_tile)],
                              sems.at[s]).wait()

    @pl.when(i >= 2)
    def _():
        wait_tile(slot)

    def tok_group(tt, c):
        for u in range(COMB_UNROLL):
            t = tt * COMB_UNROLL + u
            row = h3_ref[t]
            for k in range(TOP_K):
                cbuf[lax.shift_right_logical(lpos_ref[k * tm + t], 3)] = row
        return c

    lax.fori_loop(0, tm // COMB_UNROLL, tok_group, 0)

    def per_run(e, c):
        src0 = slot * cap + cloc_ref[i * N_EXPERTS + e]
        dst0 = cstart_ref[i * N_EXPERTS + e]

        def one(off, bit):
            pltpu.make_async_copy(cbuf.at[pl.ds(src0 + off, bit)], xs_hbm.at[pl.ds(dst0 + off, bit)],
                                  sems.at[slot]).start()

        _pow2_runs(clen_ref[i * N_EXPERTS + e], one)
        return c

    lax.fori_loop(0, N_EXPERTS, per_run, 0)

    @pl.when(i == n_tiles - 1)
    def _():
        wait_tile(slot)
        if n_tiles >= 2:
            wait_tile(1 - slot)


def _dispatch(pad_start, pad_cnt, cstart, clen, cloc, lpos_tok, h3r, *, n_slots, tm):
    n = h3r.shape[0]
    n_tiles = n // tm
    gs = pltpu.PrefetchScalarGridSpec(
        num_scalar_prefetch=5,
        grid=(n_tiles,),
        in_specs=[pl.BlockSpec((TOP_K * tm,), lambda i, *_: (i,), memory_space=pltpu.SMEM),
                  pl.BlockSpec((tm, PACK_TILES, LANES), lambda i, *_: (i, 0, 0))],
        out_specs=pl.BlockSpec(memory_space=pl.ANY),
        scratch_shapes=[pltpu.VMEM((2 * _comb_cap(tm), PACK_TILES, LANES), jnp.uint32),
                        pltpu.SemaphoreType.DMA((2,)),
                        pltpu.SemaphoreType.DMA(())],
    )
    return pl.pallas_call(
        functools.partial(_dispatch_kernel, tm=tm, n_tiles=n_tiles),
        grid_spec=gs,
        out_shape=jax.ShapeDtypeStruct((n_slots, PACK_TILES, LANES), jnp.uint32),
        compiler_params=_cparams(("arbitrary",)),
        name="dispatch",
    )(pad_start, pad_cnt, cstart, clen, cloc, lpos_tok, h3r)


def _expert_kernel(blk_e_ref, nused_ref, x_ref, wu_ref, bu_ref, wd_ref, bd_ref, y_ref, wu_bf, wd_bf, *, tb):
    b = pl.program_id(0)

    @pl.when(b < nused_ref[0])
    def _():
        prev = blk_e_ref[jnp.maximum(b - 1, 0)]

        @pl.when(jnp.logical_or(b == 0, blk_e_ref[b] != prev))
        def _():
            wu_bf[...] = wu_ref[...].astype(BF16)
            wd_bf[...] = wd_ref[...].astype(BF16)

        words = [x_ref[pl.ds(t, tb, stride=PACK_TILES), :] for t in range(PACK_TILES)]
        lo = [pltpu.unpack_elementwise(w, index=0, packed_dtype=BF16, unpacked_dtype=F32) for w in words]
        hi = [pltpu.unpack_elementwise(w, index=1, packed_dtype=BF16, unpacked_dtype=F32) for w in words]
        x = jnp.concatenate(lo + hi, axis=-1).astype(BF16)
        up = jnp.dot(x, wu_bf[...], preferred_element_type=F32) + bu_ref[...]
        glu = jnp.minimum(up[:, :D_FF], SWIGLU_LIMIT)
        lin = jnp.clip(up[:, D_FF:], -SWIGLU_LIMIT, SWIGLU_LIMIT)
        act = (glu * jax.nn.sigmoid(SWIGLU_ALPHA * glu) * (lin + 1.0)).astype(BF16)
        y = jnp.dot(act, wd_bf[...], preferred_element_type=F32) + bd_ref[...]
        for t in range(ROW_TILES):
            y_ref[pl.ds(t, tb, stride=ROW_TILES), :] = y[:, t * LANES:(t + 1) * LANES]


def _expert(blk_e, nused, xs, w_up, b_up, w_down, b_down, *, tb):
    n_slots = xs.shape[0] // PACK_TILES
    nb = n_slots // tb
    blk = lambda b, be, nu: jnp.minimum(b, nu[0] - 1)
    gs = pltpu.PrefetchScalarGridSpec(
        num_scalar_prefetch=2,
        grid=(nb,),
        in_specs=[pl.BlockSpec((tb * PACK_TILES, LANES), lambda b, be, nu: (blk(b, be, nu), 0)),
                  pl.BlockSpec((None, D_MODEL, 2 * D_FF), lambda b, be, nu: (be[blk(b, be, nu)], 0, 0)),
                  pl.BlockSpec((None, 1, 2 * D_FF), lambda b, be, nu: (be[blk(b, be, nu)], 0, 0)),
                  pl.BlockSpec((None, D_FF, D_MODEL), lambda b, be, nu: (be[blk(b, be, nu)], 0, 0)),
                  pl.BlockSpec((None, 1, D_MODEL), lambda b, be, nu: (be[blk(b, be, nu)], 0, 0))],
        out_specs=pl.BlockSpec((tb * ROW_TILES, LANES), lambda b, be, nu: (blk(b, be, nu), 0)),
        scratch_shapes=[pltpu.VMEM((D_MODEL, 2 * D_FF), BF16),
                        pltpu.VMEM((D_FF, D_MODEL), BF16)],
    )
    return pl.pallas_call(
        functools.partial(_expert_kernel, tb=tb),
        grid_spec=gs,
        out_shape=jax.ShapeDtypeStruct(((n_slots + tb) * ROW_TILES, LANES), F32),
        compiler_params=_cparams(("arbitrary",)),
        name="expert_ffn",
    )(blk_e, nused, xs, w_up, b_up, w_down, b_down)


COMB_PIECE = 32
COMB_UNROLL = 8


def _comb_cap(tm):
    return TOP_K * tm + N_EXPERTS * (COMB_PIECE - 1)


def _combine_kernel(cstart_ref, cpieces_ref, cloc_ref, lpos_ref, gate_ref, x2_ref, gfin_ref, y_hbm, out_ref,
                    chunks, rt, sems, *, tm, n_tiles):
    i = pl.program_id(0)
    slot = i % 2
    piece_rows = COMB_PIECE * ROW_TILES

    def piece_copy(tile, e, p, to_slot):
        src = (cstart_ref[tile * N_EXPERTS + e] + p * COMB_PIECE) * ROW_TILES
        dst = (to_slot * _comb_cap(tm) + cloc_ref[tile * N_EXPERTS + e] + p * COMB_PIECE) * ROW_TILES
        return pltpu.make_async_copy(
            y_hbm.at[pl.ds(pl.multiple_of(src, ROW_TILES), piece_rows), :],
            chunks.at[pl.ds(pl.multiple_of(dst, ROW_TILES), piece_rows), :],
            sems.at[to_slot])

    def for_pieces(tile, to_slot, start):
        def per_e(e, c):
            def per_p(p, c2):
                cp = piece_copy(tile, e, p, to_slot)
                if start:
                    cp.start()
                else:
                    cp.wait()
                return c2

            lax.fori_loop(0, cpieces_ref[tile * N_EXPERTS + e], per_p, 0)
            return c

        lax.fori_loop(0, N_EXPERTS, per_e, 0)

    @pl.when(i == 0)
    def _():
        for_pieces(0, 0, True)

    @pl.when(i + 1 < n_tiles)
    def _():
        for_pieces(i + 1, 1 - slot, True)

    for_pieces(i, slot, False)

    def tok_group(tt, c):
        for u in range(COMB_UNROLL):
            t = tt * COMB_UNROLL + u
            acc = None
            for k in range(TOP_K):
                r = pl.multiple_of(lpos_ref[k * tm + t], ROW_TILES)
                term = gate_ref[k * tm + t] * chunks[pl.ds(r, ROW_TILES), :]
                acc = term if acc is None else acc + term
            rt[pl.ds(pl.multiple_of(t * ROW_TILES, ROW_TILES), ROW_TILES), :] = acc
        return c

    lax.fori_loop(0, tm // COMB_UNROLL, tok_group, 0)

    moe = jnp.concatenate([rt[pl.ds(s, tm, stride=ROW_TILES), :] for s in range(ROW_TILES)], axis=-1)
    out_ref[...] = _rms(x2_ref[...] + moe, gfin_ref[...])


def _combine(cstart, cpieces, cloc, lpos, gate, x2, g_final, y, *, tm):
    n = x2.shape[0]
    n_tiles = n // tm
    gs = pltpu.PrefetchScalarGridSpec(
        num_scalar_prefetch=3,
        grid=(n_tiles,),
        in_specs=[pl.BlockSpec((TOP_K * tm,), lambda i, a, b, c: (i,), memory_space=pltpu.SMEM),
                  pl.BlockSpec((TOP_K * tm,), lambda i, a, b, c: (i,), memory_space=pltpu.SMEM),
                  pl.BlockSpec((tm, D_MODEL), lambda i, a, b, c: (i, 0)),
                  pl.BlockSpec((1, D_MODEL), lambda i, a, b, c: (0, 0)),
                  pl.BlockSpec(memory_space=pl.ANY)],
        out_specs=pl.BlockSpec((tm, D_MODEL), lambda i, a, b, c: (i, 0)),
        scratch_shapes=[pltpu.VMEM((2 * _comb_cap(tm) * ROW_TILES, LANES), F32),
                        pltpu.VMEM((tm * ROW_TILES, LANES), F32),
                        pltpu.SemaphoreType.DMA((2,))],
    )
    return pl.pallas_call(
        functools.partial(_combine_kernel, tm=tm, n_tiles=n_tiles),
        grid_spec=gs,
        out_shape=jax.ShapeDtypeStruct((n, D_MODEL), F32),
        compiler_params=_cparams(("arbitrary",)),
        name="combine",
    )(cstart, cpieces, cloc, lpos, gate, x2, g_final, y)


def _head_pad_cols(w, per_head_in, take, place):
    kdim = w.shape[0]
    w3 = w.reshape(kdim, MLA_HEADS, per_head_in)
    out = jnp.zeros((kdim, MLA_HEADS, HEAD_PAD), w.dtype)
    for (t0, t1), p0 in zip(take, place):
        out = out.at[:, :, p0:p0 + (t1 - t0)].set(w3[:, :, t0:t1])
    return out.reshape(kdim, MLA_HEADS * HEAD_PAD)


def kernel(x, mem, positions, g_mix, w_in, g_q, w_uq, g_kv, w_ukv, w_dw, b_dw, g_conv_ln, b_conv_ln, w_out,
           g_xattn, g_mem, w_xq, w_xkv, w_xo, g_ffn, w_router, b_router, w_up, b_up, w_down, b_down, g_final):
    batch, seq, _ = x.shape
    mem_len = mem.shape[1]
    n = batch * seq
    half = MLA_ROPE // 2
    r0 = MLA_NOPE
    assert w_in.shape[0] == 1, "one trunk layer"

    row = lambda v: v.reshape(1, -1).astype(F32)

    wi = w_in[0]
    o1, o2, o3 = Q_LORA, Q_LORA + KV_LORA, Q_LORA + KV_LORA + MLA_ROPE
    kr_blk = jnp.zeros((D_MODEL, HEAD_PAD), F32).at[:, r0:r0 + MLA_ROPE].set(wi[:, o2:o3])
    kr_swp = (jnp.zeros((D_MODEL, HEAD_PAD), F32)
              .at[:, r0:r0 + half].set(wi[:, o2 + half:o3])
              .at[:, r0 + half:r0 + MLA_ROPE].set(wi[:, o2:o2 + half]))
    w_in_p = jnp.concatenate([wi[:, :o2], kr_blk, kr_swp, wi[:, o3:]], axis=1).astype(BF16)
    per_q = MLA_NOPE + MLA_ROPE
    w_uq_p = _head_pad_cols(w_uq[0], per_q, [(0, per_q)], [0]).astype(BF16)
    w_uq_s = _head_pad_cols(w_uq[0], per_q, [(MLA_NOPE + half, per_q), (MLA_NOPE, MLA_NOPE + half)],
                            [r0, r0 + half]).astype(BF16)
    per_kv = MLA_NOPE + MLA_V
    w_uk_k = _head_pad_cols(w_ukv[0], per_kv, [(0, MLA_NOPE)], [0]).astype(BF16)
    w_uk_v = (w_ukv[0].reshape(KV_LORA, MLA_HEADS, per_kv)[:, :, MLA_NOPE:]
              .reshape(KV_LORA, MLA_WIDTH).astype(BF16))
    inv = ROPE_THETA ** (-jnp.arange(0, MLA_ROPE, 2, dtype=F32) / MLA_ROPE)
    inv_c = (jnp.zeros((LANES, 1), F32).at[r0:r0 + half, 0].set(inv).at[r0 + half:r0 + MLA_ROPE, 0].set(inv))
    sgn_c = (jnp.zeros((LANES, 1), F32).at[r0:r0 + half, 0].set(-1.0).at[r0 + half:r0 + MLA_ROPE, 0].set(1.0))

    x2d = x.reshape(n, D_MODEL)
    pos_row = positions.reshape(1, n).astype(I32)

    q_t, k, v_t, y_conv = _mix_in(x2d, pos_row, inv_c, sgn_c, row(g_mix[0]), w_in_p, row(g_q[0]),
                                  jnp.transpose(w_uq_p), jnp.transpose(w_uq_s), row(g_kv[0]), w_uk_k,
                                  jnp.transpose(w_uk_v), w_dw[0].astype(F32), row(b_dw[0]),
                                  row(g_conv_ln[0]), row(b_conv_ln[0]), seq=seq)
    y_mla = _attn(q_t, k, v_t, batch=batch, seq=seq)
    kv = _mem_kv(mem.reshape(batch * mem_len, D_MODEL), row(g_mem[0]), w_xkv[0].astype(BF16),
                 batch=batch, mem_len=mem_len)

    tm_mid = min(TM_MID, seq)
    wr = jnp.zeros((D_MODEL, LANES), F32).at[:, :N_EXPERTS].set(w_router[0])
    wr_h = wr.astype(BF16)
    wr_l = (wr - wr_h.astype(F32)).astype(BF16)
    b_r = b_router[0].reshape(N_EXPERTS, 1).astype(F32)
    tri = (lax.broadcasted_iota(I32, (tm_mid, tm_mid), 0)
           < lax.broadcasted_iota(I32, (tm_mid, tm_mid), 1)).astype(BF16)
    wo = w_out[0].astype(BF16)
    x2, h3r, idx, gate, rank, cnt, runs_l = _mid(
        x2d, y_mla, y_conv, kv, wo[:MLA_WIDTH], wo[MLA_WIDTH:], row(g_xattn[0]), w_xq[0].astype(BF16),
        w_xo[0].astype(BF16), row(g_ffn[0]), wr_h, wr_l, b_r, tri, seq=seq, mem_len=mem_len, tm=tm_mid)

    tb = TB_EXPERT
    counts = cnt[:, 0]
    padded = (counts + tb - 1) // tb * tb
    pend = jnp.cumsum(padded)
    pstart = pend - padded
    n_slots = n * TOP_K + N_EXPERTS * tb
    nb = n_slots // tb
    blk_first = jnp.arange(nb, dtype=I32) * tb
    blk_e = jnp.minimum(jnp.sum((pend[None, :] <= blk_first[:, None]).astype(I32), axis=1),
                        N_EXPERTS - 1).astype(I32)
    nused = (pend[-1:] // tb).astype(I32)

    runs = runs_l[:, :, 0]
    run_len = jnp.concatenate([runs[1:], counts[None, :]], axis=0) - runs
    cstart = pstart[None, :] + runs
    cpieces = (run_len + COMB_PIECE - 1) // COMB_PIECE
    cloc = jnp.cumsum(cpieces * COMB_PIECE, axis=1) - cpieces * COMB_PIECE
    adj_l = jnp.broadcast_to((cloc - runs)[:, :, None], runs_l.shape).astype(I32)

    tm_comb = min(COMB_TM, tm_mid)
    lpos = _pos(idx, rank, adj_l, tm=tm_comb)
    tiled = lambda a: a.reshape(TOP_K, n // tm_comb, tm_comb).transpose(1, 0, 2).reshape(-1)
    lpos_tok = tiled(lpos)
    flat = lambda a: a.reshape(-1).astype(I32)
    xs = _dispatch((pstart + counts).astype(I32), (padded - counts).astype(I32), flat(cstart), flat(run_len),
                   flat(cloc), lpos_tok, h3r.reshape(n, PACK_TILES, LANES), n_slots=n_slots, tm=tm_comb)
    y = _expert(blk_e, nused, xs.reshape(n_slots * PACK_TILES, LANES), w_up[0],
                b_up[0].reshape(N_EXPERTS, 1, 2 * D_FF), w_down[0], b_down[0].reshape(N_EXPERTS, 1, D_MODEL), tb=tb)
    out = _combine(flat(cstart), flat(cpieces), flat(cloc), lpos_tok, tiled(gate),
                   x2, row(g_final), y, tm=tm_comb)
    return out.reshape(batch, seq, D_MODEL)
```

```python
import functools

import jax
import jax.numpy as jnp
from jax import lax
from jax.experimental import pallas as pl
from jax.experimental.pallas import tpu as pltpu

F32 = jnp.float32
BF16 = jnp.bfloat16
I32 = jnp.int32

D_MODEL = 1024
MLA_HEADS = 8
MLA_NOPE = 64
MLA_ROPE = 32
MLA_V = 64
MLA_WIDTH = MLA_HEADS * MLA_V
Q_LORA = 256
KV_LORA = 128
ROPE_THETA = 10000.0
CHUNK = 64
CONV_CH = 512
CONV_WIDTH = 31
X_HEADS = 4
X_HEAD_DIM = D_MODEL // X_HEADS
N_EXPERTS = 32
TOP_K = 4
D_FF = D_MODEL
SWIGLU_LIMIT = 7.0
SWIGLU_ALPHA = 1.702
NORM_EPS = 1e-5

LANES = 128
SUBLANES = 8
ROW_TILES = D_MODEL // LANES
PACK_TILES = ROW_TILES // 2
VMEM_LIMIT_BYTES = 56 * 1024 * 1024

HEAD_PAD = LANES
Q_PRESCALE = (MLA_NOPE + MLA_ROPE) ** -0.5 * 1.4426950408889634
CONV_HALO = 32

TM_MIX = 512
TQ_ATTN = 1024
TM_MID = 512
TB_EXPERT = 512
MID_SPLIT = 2
COMB_TM = 512


def _rms(x, g):
    return x * lax.rsqrt(jnp.mean(x * x, axis=-1, keepdims=True) + NORM_EPS) * g


def _cparams(sem):
    return pltpu.CompilerParams(dimension_semantics=sem, vmem_limit_bytes=VMEM_LIMIT_BYTES)


def _mix_in_kernel(x_ref, pos_ref, inv_ref, sgn_ref, gmix_ref, win_ref, gq_ref, wuq_ref, wuqs_ref,
                   gkv_ref, wukk_ref, wukv_ref, wdw_ref, bdw_ref, gln_ref, bln_ref,
                   q_ref, k_ref, v_ref, yc_ref, zbuf, zsh, *, tiles_per_batch, tm):
    i = pl.program_id(0)
    x = x_ref[...]
    h = _rms(x, gmix_ref[...]).astype(BF16)
    u = jnp.dot(h, win_ref[...], preferred_element_type=F32)
    c0 = Q_LORA + KV_LORA
    c1 = c0 + 2 * HEAD_PAD
    cq = u[:, 0:Q_LORA]
    ckv = u[:, Q_LORA:c0]
    kr = u[:, c0:c0 + HEAD_PAD]
    krs = u[:, c0 + HEAD_PAD:c1]
    a = u[:, c1:c1 + CONV_CH]
    gate = u[:, c1 + CONV_CH:c1 + 2 * CONV_CH]

    r0, r1 = MLA_NOPE, MLA_NOPE + MLA_ROPE
    ang_r = inv_ref[r0:r1, :] * pos_ref[...].astype(F32)
    cos_t = jnp.concatenate([jnp.ones((r0, tm), F32), jnp.cos(ang_r), jnp.ones((HEAD_PAD - r1, tm), F32)], axis=0)
    sin_t = jnp.concatenate([jnp.zeros((r0, tm), F32), jnp.sin(ang_r) * sgn_ref[r0:r1, :],
                             jnp.zeros((HEAD_PAD - r1, tm), F32)], axis=0)
    cosb = jnp.transpose(cos_t)
    sinb = jnp.transpose(sin_t)
    cos_q = cos_t * Q_PRESCALE
    sin_q = sin_t * Q_PRESCALE

    nt = (((1,), (1,)), ((), ()))
    cqn = _rms(cq, gq_ref[...]).astype(BF16)
    qm = lax.dot_general(wuq_ref[...], cqn, nt, preferred_element_type=F32)
    qs = lax.dot_general(wuqs_ref[...], cqn, nt, preferred_element_type=F32)
    for hd in range(MLA_HEADS):
        sl = slice(hd * HEAD_PAD, (hd + 1) * HEAD_PAD)
        q_ref[sl, :] = (qm[sl, :] * cos_q + qs[sl, :] * sin_q).astype(BF16)

    ckvn = _rms(ckv, gkv_ref[...]).astype(BF16)
    kk = jnp.dot(ckvn, wukk_ref[...], preferred_element_type=F32)
    v_ref[...] = lax.dot_general(wukv_ref[...], ckvn, nt, preferred_element_type=F32).astype(BF16)
    krot = kr * cosb + krs * sinb
    for hd in range(MLA_HEADS):
        sl = slice(hd * HEAD_PAD, (hd + 1) * HEAD_PAD)
        k_ref[:, sl] = (kk[:, sl] + krot).astype(BF16)

    z = a * jax.nn.sigmoid(gate)

    @pl.when(i % tiles_per_batch == 0)
    def _():
        zbuf[0:CONV_HALO, :] = jnp.zeros((CONV_HALO, CONV_CH), F32)

    zbuf[CONV_HALO:CONV_HALO + tm, :] = z
    off = CONV_HALO - (CONV_WIDTH - 1)
    rows = 32
    span = tm + CONV_HALO - SUBLANES
    step = 64
    for r in range(1, SUBLANES):
        for c0 in range(0, span, step):
            cl = min(step, span - c0)
            zsh[r - 1, c0:c0 + cl, :] = zbuf[c0 + r:c0 + r + cl, :]
    for r0 in range(0, tm, rows):
        acc = jnp.zeros((rows, CONV_CH), F32) + bdw_ref[...]
        for j in range(CONV_WIDTH):
            a8, ph = divmod(j + off, SUBLANES)
            lo_row = r0 + a8 * SUBLANES
            tap = zbuf[lo_row:lo_row + rows, :] if ph == 0 else zsh[ph - 1, lo_row:lo_row + rows, :]
            acc = acc + wdw_ref[j:j + 1, :] * tap
        mu = jnp.mean(acc, axis=-1, keepdims=True)
        cen = acc - mu
        var = jnp.mean(cen * cen, axis=-1, keepdims=True)
        y = cen * lax.rsqrt(var + NORM_EPS) * gln_ref[...] + bln_ref[...]
        yc_ref[r0:r0 + rows, :] = (y * jax.nn.sigmoid(y)).astype(BF16)
    zbuf[0:CONV_HALO, :] = zbuf[tm:tm + CONV_HALO, :]


def _mix_in(x2d, pos_row, inv_c, sgn_c, g_mix, w_in_p, g_q, w_uq_t, w_uq_st, g_kv, w_uk_k, w_uk_vt,
            w_dw, b_dw, g_ln, b_ln, *, seq):
    n = x2d.shape[0]
    tm = min(TM_MIX, seq)
    full = lambda a: pl.BlockSpec(a.shape, lambda i: (0,) * a.ndim)
    consts = [inv_c, sgn_c, g_mix, w_in_p, g_q, w_uq_t, w_uq_st, g_kv, w_uk_k, w_uk_vt, w_dw, b_dw, g_ln, b_ln]
    return pl.pallas_call(
        functools.partial(_mix_in_kernel, tiles_per_batch=seq // tm, tm=tm),
        grid=(n // tm,),
        in_specs=[pl.BlockSpec((tm, D_MODEL), lambda i: (i, 0)),
                  pl.BlockSpec((1, tm), lambda i: (0, i))] + [full(a) for a in consts],
        out_specs=[pl.BlockSpec((MLA_HEADS * HEAD_PAD, tm), lambda i: (0, i)),
                   pl.BlockSpec((tm, MLA_HEADS * HEAD_PAD), lambda i: (i, 0)),
                   pl.BlockSpec((MLA_WIDTH, tm), lambda i: (0, i)),
                   pl.BlockSpec((tm, CONV_CH), lambda i: (i, 0))],
        out_shape=[jax.ShapeDtypeStruct((MLA_HEADS * HEAD_PAD, n), BF16),
                   jax.ShapeDtypeStruct((n, MLA_HEADS * HEAD_PAD), BF16),
                   jax.ShapeDtypeStruct((MLA_WIDTH, n), BF16),
                   jax.ShapeDtypeStruct((n, CONV_CH), BF16)],
        scratch_shapes=[pltpu.VMEM((CONV_HALO + tm + SUBLANES, CONV_CH), F32),
                        pltpu.VMEM((SUBLANES - 1, CONV_HALO + tm, CONV_CH), F32)],
        compiler_params=_cparams(("arbitrary",)),
        name="mix_in",
    )(x2d, pos_row, *consts)


ATTN_GROUP = 2


TK_ATTN = 1024
DENOM_ROWS = 16


def _attn_kernel(q_ref, k_ref, v_ref, o_ref, *s_scr, tq):
    i = pl.program_id(1)
    tk = min(TK_ATTN, tq)
    per_q = tq // tk
    key_c = lax.broadcasted_iota(I32, (tk, tq), 0) // CHUNK
    qry_c = lax.broadcasted_iota(I32, (tk, tq), 1) // CHUNK

    for g in range(MLA_HEADS // ATTN_GROUP):
        heads = list(range(g * ATTN_GROUP, (g + 1) * ATTN_GROUP))
        qts = [q_ref[hd * HEAD_PAD:(hd + 1) * HEAD_PAD, :] for hd in heads]

        def step(j, carry, diag):
            start = pl.multiple_of(j * tk, tk)
            new = []
            for n_, hd in enumerate(heads):
                m, acc = carry[n_]
                kj = k_ref[pl.ds(start, tk), hd * HEAD_PAD:(hd + 1) * HEAD_PAD]
                vj = jnp.concatenate([v_ref[hd * MLA_V:(hd + 1) * MLA_V, pl.ds(start, tk)],
                                      jnp.ones((DENOM_ROWS, tk), BF16)], axis=0)
                s = jnp.dot(kj, qts[n_], preferred_element_type=F32)
                if diag is not None:
                    s = jnp.where(key_c + diag * (tk // CHUNK) <= qry_c, s, -jnp.inf)
                m_new = jnp.maximum(m, jnp.max(s, axis=0, keepdims=True))
                s_scr[n_][...] = s
                s = s_scr[n_][...]
                alpha = jnp.exp2(m - m_new)
                p = jnp.exp2(s - m_new)
                acc = alpha * acc + jnp.dot(vj, p.astype(BF16), preferred_element_type=F32)
                new.append((m_new, acc))
            return tuple(new)

        init = tuple((jnp.full((1, tq), -jnp.inf, F32), jnp.zeros((MLA_V + DENOM_ROWS, tq), F32)) for _ in heads)
        carry = lax.fori_loop(0, i * per_q, functools.partial(step, diag=None), init)
        for d in range(per_q):
            carry = step(i * per_q + d, carry, d)
        for n_, hd in enumerate(heads):
            _, acc = carry[n_]
            out = acc[0:MLA_V, :] / acc[MLA_V:MLA_V + 1, :]
            o_ref[:, hd * MLA_V:(hd + 1) * MLA_V] = jnp.transpose(out).astype(BF16)


def _attn(q_t, k, v_t, *, batch, seq):
    tq = min(TQ_ATTN, seq)
    nq = seq // tq
    return pl.pallas_call(
        functools.partial(_attn_kernel, tq=tq),
        grid=(batch, nq),
        in_specs=[pl.BlockSpec((MLA_HEADS * HEAD_PAD, tq), lambda b, i: (0, b * nq + i)),
                  pl.BlockSpec((seq, MLA_HEADS * HEAD_PAD), lambda b, i: (b, 0)),
                  pl.BlockSpec((MLA_WIDTH, seq), lambda b, i: (0, b))],
        out_specs=pl.BlockSpec((tq, MLA_WIDTH), lambda b, i: (b * nq + i, 0)),
        out_shape=jax.ShapeDtypeStruct((batch * seq, MLA_WIDTH), BF16),
        scratch_shapes=[pltpu.VMEM((min(TK_ATTN, tq), tq), F32) for _ in range(ATTN_GROUP)],
        compiler_params=_cparams(("arbitrary", "arbitrary")),
        name="mla_attn",
    )(q_t, k, v_t)


def _mem_kv_kernel(mem_ref, g_ref, w_ref, kv_ref):
    mn = _rms(mem_ref[...], g_ref[...]).astype(BF16)
    kv_ref[...] = jnp.dot(mn, w_ref[...], preferred_element_type=F32).astype(BF16)


def _mem_kv(mem2d, g_mem, w_xkv, *, batch, mem_len):
    return pl.pallas_call(
        _mem_kv_kernel,
        grid=(batch,),
        in_specs=[pl.BlockSpec((mem_len, D_MODEL), lambda b: (b, 0)),
                  pl.BlockSpec((1, D_MODEL), lambda b: (0, 0)),
                  pl.BlockSpec((D_MODEL, 2 * D_MODEL), lambda b: (0, 0))],
        out_specs=pl.BlockSpec((mem_len, 2 * D_MODEL), lambda b: (b, 0)),
        out_shape=jax.ShapeDtypeStruct((batch * mem_len, 2 * D_MODEL), BF16),
        compiler_params=_cparams(("arbitrary",)),
        name="mem_kv",
    )(mem2d, g_mem, w_xkv)


def _mid_kernel(x_ref, ya_ref, yc_ref, kv_ref, woa_ref, wob_ref, gx_ref, wxq_ref, wxo_ref, gf_ref,
                wrh_ref, wrl_ref, br_ref, tri_ref,
                x2_ref, h3_ref, idx_ref, gate_ref, rank_ref, cnt_ref, runs_ref, run_ref, *, tm):
    i = pl.program_id(0)

    @pl.when(i == 0)
    def _():
        run_ref[...] = jnp.zeros_like(run_ref)

    nt = (((1,), (1,)), ((), ()))
    half_d = D_MODEL // 2
    rows = tm // MID_SPLIT
    lgs = []
    for part in range(MID_SPLIT):
        rs = slice(part * rows, (part + 1) * rows)
        x1 = (x_ref[rs, :]
              + jnp.dot(ya_ref[rs, :], woa_ref[...], preferred_element_type=F32)
              + jnp.dot(yc_ref[rs, :], wob_ref[...], preferred_element_type=F32))
        h = _rms(x1, gx_ref[...]).astype(BF16)
        q = jnp.dot(h, wxq_ref[...], preferred_element_type=F32).astype(BF16)
        os = []
        for hd in range(X_HEADS):
            sl = slice(hd * X_HEAD_DIM, (hd + 1) * X_HEAD_DIM)
            vsl = slice(D_MODEL + hd * X_HEAD_DIM, D_MODEL + (hd + 1) * X_HEAD_DIM)
            s = lax.dot_general(q[:, sl], kv_ref[:, sl], nt, preferred_element_type=F32) * (X_HEAD_DIM ** -0.5)
            e = jnp.exp(s - jnp.max(s, axis=-1, keepdims=True))
            p = (e / jnp.sum(e, axis=-1, keepdims=True)).astype(BF16)
            os.append(jnp.dot(p, kv_ref[:, vsl], preferred_element_type=F32).astype(BF16))
        o = jnp.concatenate(os, axis=-1)
        x2 = x1 + jnp.dot(o, wxo_ref[...], preferred_element_type=F32)
        x2_ref[rs, :] = x2
        h3 = _rms(x2, gf_ref[...])
        hh = h3.astype(BF16)
        hf = hh.astype(F32)
        word = pltpu.pack_elementwise([h3[:, :half_d], h3[:, half_d:]], packed_dtype=BF16)
        for t in range(PACK_TILES):
            h3_ref[pl.ds(part * rows * PACK_TILES + t, rows, stride=PACK_TILES), :] = word[:, t * LANES:(t + 1) * LANES]

        hl = (h3 - hf).astype(BF16)
        lgs.append(jnp.dot(hh, wrh_ref[...], preferred_element_type=F32)
                   + (jnp.dot(hh, wrl_ref[...], preferred_element_type=F32)
                      + jnp.dot(hl, wrh_ref[...], preferred_element_type=F32)))
    lg = jnp.concatenate(lgs, axis=0)
    lgt = jnp.transpose(lg)[0:N_EXPERTS, :] + br_ref[...]
    eid = lax.broadcasted_iota(I32, (N_EXPERTS, tm), 0)
    vals, hots = [], []
    cur = lgt
    for k in range(TOP_K):
        mx = jnp.max(cur, axis=0, keepdims=True)
        ik = jnp.min(jnp.where(cur == mx, eid, N_EXPERTS), axis=0, keepdims=True)
        hot = eid == ik
        cur = jnp.where(hot, -jnp.inf, cur)
        idx_ref[k:k + 1, :] = ik
        vals.append(mx)
        hots.append(hot)
    es = [jnp.exp(v - vals[0]) for v in vals]
    den = es[0] + es[1] + es[2] + es[3]
    for k in range(TOP_K):
        gate_ref[k:k + 1, :] = es[k] / den
    cnt = jnp.zeros((N_EXPERTS, tm), F32)
    for k in range(TOP_K):
        cnt = cnt + hots[k].astype(F32)
    run = run_ref[:, 0:1]
    ctm = min(COMB_TM, tm)
    for j in range(tm // ctm):
        before = run if j == 0 else run + jnp.sum(cnt[:, :j * ctm], axis=1, keepdims=True)
        runs_ref[j] = jnp.broadcast_to(before, (N_EXPERTS, LANES)).astype(I32)
    tot = jnp.dot(cnt.astype(BF16), tri_ref[...], preferred_element_type=F32) + run
    for k in range(TOP_K):
        rank_ref[k:k + 1, :] = jnp.sum(jnp.where(hots[k], tot, 0.0), axis=0, keepdims=True).astype(I32)
    new_run = run + jnp.sum(cnt, axis=1, keepdims=True)
    run_ref[...] = jnp.broadcast_to(new_run, run_ref.shape)
    cnt_ref[...] = jnp.broadcast_to(new_run, cnt_ref.shape).astype(I32)


def _mid(x2d, ya, yc, kv, w_oa, w_ob, g_x, w_xq, w_xo, g_f, wr_h, wr_l, b_r, tri, *, seq, mem_len, tm):
    n = x2d.shape[0]
    tpb = seq // tm
    full = lambda a: pl.BlockSpec(a.shape, lambda i: (0,) * a.ndim)
    consts_a = [w_oa, w_ob, g_x, w_xq, w_xo, g_f, wr_h, wr_l, b_r, tri]
    return pl.pallas_call(
        functools.partial(_mid_kernel, tm=tm),
        grid=(n // tm,),
        in_specs=[pl.BlockSpec((tm, D_MODEL), lambda i: (i, 0)),
                  pl.BlockSpec((tm, MLA_WIDTH), lambda i: (i, 0)),
                  pl.BlockSpec((tm, CONV_CH), lambda i: (i, 0)),
                  pl.BlockSpec((mem_len, 2 * D_MODEL), lambda i: (i // tpb, 0))] + [full(a) for a in consts_a],
        out_specs=[pl.BlockSpec((tm, D_MODEL), lambda i: (i, 0)),
                   pl.BlockSpec((tm * PACK_TILES, LANES), lambda i: (i, 0)),
                   pl.BlockSpec((TOP_K, tm), lambda i: (0, i)),
                   pl.BlockSpec((TOP_K, tm), lambda i: (0, i)),
                   pl.BlockSpec((TOP_K, tm), lambda i: (0, i)),
                   pl.BlockSpec((N_EXPERTS, LANES), lambda i: (0, 0)),
                   pl.BlockSpec((tm // min(COMB_TM, tm), N_EXPERTS, LANES), lambda i: (i, 0, 0))],
        out_shape=[jax.ShapeDtypeStruct((n, D_MODEL), F32),
                   jax.ShapeDtypeStruct((n * PACK_TILES, LANES), jnp.uint32),
                   jax.ShapeDtypeStruct((TOP_K, n), I32),
                   jax.ShapeDtypeStruct((TOP_K, n), F32),
                   jax.ShapeDtypeStruct((TOP_K, n), I32),
                   jax.ShapeDtypeStruct((N_EXPERTS, LANES), I32),
                   jax.ShapeDtypeStruct((n // min(COMB_TM, tm), N_EXPERTS, LANES), I32)],
        scratch_shapes=[pltpu.VMEM((N_EXPERTS, LANES), F32)],
        compiler_params=_cparams(("arbitrary",)),
        name="mid",
    )(x2d, ya, yc, kv, *consts_a)


def _pos_kernel(idx_ref, rank_ref, adj_ref, lpos_ref, *, tm, group):
    i = pl.program_id(0)
    eid = lax.broadcasted_iota(I32, (N_EXPERTS, tm), 0)
    for j in range(group):
        half = ((i * group + j) % 2) * _comb_cap(tm)
        cols = slice(j * tm, (j + 1) * tm)
        adj = adj_ref[j][:, 0:1] + half
        for k in range(TOP_K):
            hot = eid == idx_ref[k:k + 1, cols]
            lpos_ref[k:k + 1, cols] = (jnp.sum(jnp.where(hot, adj, 0), axis=0, keepdims=True)
                                       + rank_ref[k:k + 1, cols]) * ROW_TILES


def _pos(idx, rank, adj_l, *, tm):
    n = idx.shape[1]
    n_tiles = n // tm
    group = 4 if n_tiles % 4 == 0 else 1
    return pl.pallas_call(
        functools.partial(_pos_kernel, tm=tm, group=group),
        grid=(n_tiles // group,),
        in_specs=[pl.BlockSpec((TOP_K, tm * group), lambda i: (0, i)),
                  pl.BlockSpec((TOP_K, tm * group), lambda i: (0, i)),
                  pl.BlockSpec((group, N_EXPERTS, LANES), lambda i: (i, 0, 0))],
        out_specs=pl.BlockSpec((TOP_K, tm * group), lambda i: (0, i)),
        out_shape=jax.ShapeDtypeStruct((TOP_K, n), I32),
        compiler_params=_cparams(("arbitrary",)),
        name="slot_pos",
    )(idx, rank, adj_l)


RUN_BITS = tuple(1 << b for b in range(9, -1, -1))


def _pow2_runs(length, fn):
    off = jnp.int32(0)
    for bit in RUN_BITS:
        take = (length & bit) != 0

        @pl.when(take)
        def _(off=off, bit=bit):
            fn(off, bit)

        off = off + jnp.where(take, bit, 0)


def _dispatch_kernel(pad_start_ref, pad_cnt_ref, cstart_ref, clen_ref, cloc_ref, lpos_ref, h3_ref, xs_hbm,
                     cbuf, sems, zsem, *, tm, n_tiles):
    i = pl.program_id(0)
    slot = i % 2
    cap = _comb_cap(tm)
    rows_per_tile = TOP_K * tm

    @pl.when(i == 0)
    def _():
        zrows = RUN_BITS[0]
        cbuf[pl.ds(cap, zrows)] = jnp.zeros((zrows, PACK_TILES, LANES), jnp.uint32)

        def pad_pass(start):
            def per_expert(e, c):
                def one(off, bit):
                    cp = pltpu.make_async_copy(cbuf.at[pl.ds(cap, bit)],
                                               xs_hbm.at[pl.ds(pad_start_ref[e] + off, bit)], zsem)
                    cp.start() if start else cp.wait()

                _pow2_runs(pad_cnt_ref[e], one)
                return c

            lax.fori_loop(0, N_EXPERTS, per_expert, 0)

        pad_pass(True)
        pad_pass(False)

        def tail_pass(start):
            def one(j, c):
                cp = pltpu.make_async_copy(cbuf.at[pl.ds(cap, zrows)],
                                           xs_hbm.at[pl.ds(pad_start_ref[N_EXPERTS] + j * zrows, zrows)], zsem)
                cp.start() if start else cp.wait()
                return c

            lax.fori_loop(0, pad_cnt_ref[N_EXPERTS], one, 0)

        tail_pass(True)
        tail_pass(False)

    def wait_tile(s):
        pltpu.make_async_copy(cbuf.at[pl.ds(s * cap, rows_per_tile)], xs_hbm.at[pl.ds(0, rows_per_tile)],
                              sems.at[s]).wait()

    @pl.when(i >= 2)
    def _():
        wait_tile(slot)

    def tok_group(tt, c):
        for u in range(COMB_UNROLL):
            t = tt * COMB_UNROLL + u
            row = h3_ref[t]
            for k in range(TOP_K):
                cbuf[lax.shift_right_logical(lpos_ref[k * tm + t], 3)] = row
        return c

    lax.fori_loop(0, tm // COMB_UNROLL, tok_group, 0)

    def per_run(e, c):
        src0 = slot * cap + cloc_ref[i * N_EXPERTS + e]
        dst0 = cstart_ref[i * N_EXPERTS + e]

        def one(off, bit):
            pltpu.make_async_copy(cbuf.at[pl.ds(src0 + off, bit)], xs_hbm.at[pl.ds(dst0 + off, bit)],
                                  sems.at[slot]).start()

        _pow2_runs(clen_ref[i * N_EXPERTS + e], one)
        return c

    lax.fori_loop(0, N_EXPERTS, per_run, 0)

    @pl.when(i == n_tiles - 1)
    def _():
        wait_tile(slot)
        if n_tiles >= 2:
            wait_tile(1 - slot)


def _dispatch(pad_start, pad_cnt, cstart, clen, cloc, lpos_tok, h3r, *, n_slots, tm):
    n = h3r.shape[0]
    n_tiles = n // tm
    gs = pltpu.PrefetchScalarGridSpec(
        num_scalar_prefetch=5,
        grid=(n_tiles,),
        in_specs=[pl.BlockSpec((TOP_K * tm,), lambda i, *_: (i,), memory_space=pltpu.SMEM),
                  pl.BlockSpec((tm, PACK_TILES, LANES), lambda i, *_: (i, 0, 0))],
        out_specs=pl.BlockSpec(memory_space=pl.ANY),
        scratch_shapes=[pltpu.VMEM((2 * _comb_cap(tm), PACK_TILES, LANES), jnp.uint32),
                        pltpu.SemaphoreType.DMA((2,)),
                        pltpu.SemaphoreType.DMA(())],
    )
    return pl.pallas_call(
        functools.partial(_dispatch_kernel, tm=tm, n_tiles=n_tiles),
        grid_spec=gs,
        out_shape=jax.ShapeDtypeStruct((n_slots, PACK_TILES, LANES), jnp.uint32),
        compiler_params=_cparams(("arbitrary",)),
        name="dispatch",
    )(pad_start, pad_cnt, cstart, clen, cloc, lpos_tok, h3r)


def _expert_kernel(blk_e_ref, nused_ref, x_ref, wu_ref, bu_ref, wd_ref, bd_ref, y_ref, wu_bf, wd_bf, *, tb):
    b = pl.program_id(0)

    @pl.when(b >= nused_ref[0])
    def _():
        y_ref[...] = jnp.zeros_like(y_ref)

    @pl.when(b < nused_ref[0])
    def _():
        prev = blk_e_ref[jnp.maximum(b - 1, 0)]

        @pl.when(jnp.logical_or(b == 0, blk_e_ref[b] != prev))
        def _():
            wu_bf[...] = wu_ref[...].astype(BF16)
            wd_bf[...] = wd_ref[...].astype(BF16)

        words = [x_ref[pl.ds(t, tb, stride=PACK_TILES), :] for t in range(PACK_TILES)]
        lo = [pltpu.unpack_elementwise(w, index=0, packed_dtype=BF16, unpacked_dtype=F32) for w in words]
        hi = [pltpu.unpack_elementwise(w, index=1, packed_dtype=BF16, unpacked_dtype=F32) for w in words]
        x = jnp.concatenate(lo + hi, axis=-1).astype(BF16)
        up = jnp.dot(x, wu_bf[...], preferred_element_type=F32) + bu_ref[...]
        glu = jnp.minimum(up[:, :D_FF], SWIGLU_LIMIT)
        lin = jnp.clip(up[:, D_FF:], -SWIGLU_LIMIT, SWIGLU_LIMIT)
        act = (glu * jax.nn.sigmoid(SWIGLU_ALPHA * glu) * (lin + 1.0)).astype(BF16)
        y = jnp.dot(act, wd_bf[...], preferred_element_type=F32) + bd_ref[...]
        for t in range(ROW_TILES):
            y_ref[pl.ds(t, tb, stride=ROW_TILES), :] = y[:, t * LANES:(t + 1) * LANES]


def _expert(blk_e, nused, xs, w_up, b_up, w_down, b_down, *, tb):
    n_slots = xs.shape[0] // PACK_TILES
    nb = n_slots // tb
    blk = lambda b, be, nu: jnp.minimum(b, nu[0] - 1)
    gs = pltpu.PrefetchScalarGridSpec(
        num_scalar_prefetch=2,
        grid=(nb + 1,),
        in_specs=[pl.BlockSpec((tb * PACK_TILES, LANES), lambda b, be, nu: (blk(b, be, nu), 0)),
                  pl.BlockSpec((None, D_MODEL, 2 * D_FF), lambda b, be, nu: (be[blk(b, be, nu)], 0, 0)),
                  pl.BlockSpec((None, 1, 2 * D_FF), lambda b, be, nu: (be[blk(b, be, nu)], 0, 0)),
                  pl.BlockSpec((None, D_FF, D_MODEL), lambda b, be, nu: (be[blk(b, be, nu)], 0, 0)),
                  pl.BlockSpec((None, 1, D_MODEL), lambda b, be, nu: (be[blk(b, be, nu)], 0, 0))],
        out_specs=pl.BlockSpec((tb * ROW_TILES, LANES), lambda b, be, nu: (b, 0)),
        scratch_shapes=[pltpu.VMEM((D_MODEL, 2 * D_FF), BF16),
                        pltpu.VMEM((D_FF, D_MODEL), BF16)],
    )
    return pl.pallas_call(
        functools.partial(_expert_kernel, tb=tb),
        grid_spec=gs,
        out_shape=jax.ShapeDtypeStruct(((n_slots + tb) * ROW_TILES, LANES), F32),
        compiler_params=_cparams(("arbitrary",)),
        name="expert_ffn",
    )(blk_e, nused, xs, w_up, b_up, w_down, b_down)


COMB_PIECE = 32
COMB_UNROLL = 8


def _comb_cap(tm):
    return TOP_K * tm + N_EXPERTS * (COMB_PIECE - 1)


def _combine_kernel(cstart_ref, cpieces_ref, cloc_ref, lpos_ref, gate_ref, x2_ref, gfin_ref, y_hbm, out_ref,
                    chunks, rt, sems, *, tm, n_tiles):
    i = pl.program_id(0)
    slot = i % 2
    piece_rows = COMB_PIECE * ROW_TILES

    def piece_copy(tile, e, p, to_slot):
        src = (cstart_ref[tile * N_EXPERTS + e] + p * COMB_PIECE) * ROW_TILES
        dst = (to_slot * _comb_cap(tm) + cloc_ref[tile * N_EXPERTS + e] + p * COMB_PIECE) * ROW_TILES
        return pltpu.make_async_copy(
            y_hbm.at[pl.ds(pl.multiple_of(src, ROW_TILES), piece_rows), :],
            chunks.at[pl.ds(pl.multiple_of(dst, ROW_TILES), piece_rows), :],
            sems.at[to_slot])

    def for_pieces(tile, to_slot, start):
        def per_e(e, c):
            def per_p(p, c2):
                cp = piece_copy(tile, e, p, to_slot)
                if start:
                    cp.start()
                else:
                    cp.wait()
                return c2

            lax.fori_loop(0, cpieces_ref[tile * N_EXPERTS + e], per_p, 0)
            return c

        lax.fori_loop(0, N_EXPERTS, per_e, 0)

    @pl.when(i == 0)
    def _():
        for_pieces(0, 0, True)

    @pl.when(i + 1 < n_tiles)
    def _():
        for_pieces(i + 1, 1 - slot, True)

    for_pieces(i, slot, False)

    def tok_group(tt, c):
        for u in range(COMB_UNROLL):
            t = tt * COMB_UNROLL + u
            acc = None
            for k in range(TOP_K):
                r = pl.multiple_of(lpos_ref[k * tm + t], ROW_TILES)
                term = gate_ref[k * tm + t] * chunks[pl.ds(r, ROW_TILES), :]
                acc = term if acc is None else acc + term
            rt[pl.ds(pl.multiple_of(t * ROW_TILES, ROW_TILES), ROW_TILES), :] = acc
        return c

    lax.fori_loop(0, tm // COMB_UNROLL, tok_group, 0)

    moe = jnp.concatenate([rt[pl.ds(s, tm, stride=ROW_TILES), :] for s in range(ROW_TILES)], axis=-1)
    out_ref[...] = _rms(x2_ref[...] + moe, gfin_ref[...])


def _combine(cstart, cpieces, cloc, lpos, gate, x2, g_final, y, *, tm):
    n = x2.shape[0]
    n_tiles = n // tm
    gs = pltpu.PrefetchScalarGridSpec(
        num_scalar_prefetch=3,
        grid=(n_tiles,),
        in_specs=[pl.BlockSpec((TOP_K * tm,), lambda i, a, b, c: (i,), memory_space=pltpu.SMEM),
                  pl.BlockSpec((TOP_K * tm,), lambda i, a, b, c: (i,), memory_space=pltpu.SMEM),
                  pl.BlockSpec((tm, D_MODEL), lambda i, a, b, c: (i, 0)),
                  pl.BlockSpec((1, D_MODEL), lambda i, a, b, c: (0, 0)),
                  pl.BlockSpec(memory_space=pl.ANY)],
        out_specs=pl.BlockSpec((tm, D_MODEL), lambda i, a, b, c: (i, 0)),
        scratch_shapes=[pltpu.VMEM((2 * _comb_cap(tm) * ROW_TILES, LANES), F32),
                        pltpu.VMEM((tm * ROW_TILES, LANES), F32),
                        pltpu.SemaphoreType.DMA((2,))],
    )
    return pl.pallas_call(
        functools.partial(_combine_kernel, tm=tm, n_tiles=n_tiles),
        grid_spec=gs,
        out_shape=jax.ShapeDtypeStruct((n, D_MODEL), F32),
        compiler_params=_cparams(("arbitrary",)),
        name="combine",
    )(cstart, cpieces, cloc, lpos, gate, x2, g_final, y)


def _head_pad_cols(w, per_head_in, take, place):
    kdim = w.shape[0]
    w3 = w.reshape(kdim, MLA_HEADS, per_head_in)
    out = jnp.zeros((kdim, MLA_HEADS, HEAD_PAD), w.dtype)
    for (t0, t1), p0 in zip(take, place):
        out = out.at[:, :, p0:p0 + (t1 - t0)].set(w3[:, :, t0:t1])
    return out.reshape(kdim, MLA_HEADS * HEAD_PAD)


def kernel(x, mem, positions, g_mix, w_in, g_q, w_uq, g_kv, w_ukv, w_dw, b_dw, g_conv_ln, b_conv_ln, w_out,
           g_xattn, g_mem, w_xq, w_xkv, w_xo, g_ffn, w_router, b_router, w_up, b_up, w_down, b_down, g_final):
    batch, seq, _ = x.shape
    mem_len = mem.shape[1]
    n = batch * seq
    half = MLA_ROPE // 2
    r0 = MLA_NOPE
    assert w_in.shape[0] == 1, "one trunk layer"

    row = lambda v: v.reshape(1, -1).astype(F32)

    wi = w_in[0]
    o1, o2, o3 = Q_LORA, Q_LORA + KV_LORA, Q_LORA + KV_LORA + MLA_ROPE
    kr_blk = jnp.zeros((D_MODEL, HEAD_PAD), F32).at[:, r0:r0 + MLA_ROPE].set(wi[:, o2:o3])
    kr_swp = (jnp.zeros((D_MODEL, HEAD_PAD), F32)
              .at[:, r0:r0 + half].set(wi[:, o2 + half:o3])
              .at[:, r0 + half:r0 + MLA_ROPE].set(wi[:, o2:o2 + half]))
    w_in_p = jnp.concatenate([wi[:, :o2], kr_blk, kr_swp, wi[:, o3:]], axis=1).astype(BF16)
    per_q = MLA_NOPE + MLA_ROPE
    w_uq_p = _head_pad_cols(w_uq[0], per_q, [(0, per_q)], [0]).astype(BF16)
    w_uq_s = _head_pad_cols(w_uq[0], per_q, [(MLA_NOPE + half, per_q), (MLA_NOPE, MLA_NOPE + half)],
                            [r0, r0 + half]).astype(BF16)
    per_kv = MLA_NOPE + MLA_V
    w_uk_k = _head_pad_cols(w_ukv[0], per_kv, [(0, MLA_NOPE)], [0]).astype(BF16)
    w_uk_v = (w_ukv[0].reshape(KV_LORA, MLA_HEADS, per_kv)[:, :, MLA_NOPE:]
              .reshape(KV_LORA, MLA_WIDTH).astype(BF16))
    inv = ROPE_THETA ** (-jnp.arange(0, MLA_ROPE, 2, dtype=F32) / MLA_ROPE)
    inv_c = (jnp.zeros((LANES, 1), F32).at[r0:r0 + half, 0].set(inv).at[r0 + half:r0 + MLA_ROPE, 0].set(inv))
    sgn_c = (jnp.zeros((LANES, 1), F32).at[r0:r0 + half, 0].set(-1.0).at[r0 + half:r0 + MLA_ROPE, 0].set(1.0))

    x2d = x.reshape(n, D_MODEL)
    pos_row = positions.reshape(1, n).astype(I32)

    q_t, k, v_t, y_conv = _mix_in(x2d, pos_row, inv_c, sgn_c, row(g_mix[0]), w_in_p, row(g_q[0]),
                                  jnp.transpose(w_uq_p), jnp.transpose(w_uq_s), row(g_kv[0]), w_uk_k,
                                  jnp.transpose(w_uk_v), w_dw[0].astype(F32), row(b_dw[0]),
                                  row(g_conv_ln[0]), row(b_conv_ln[0]), seq=seq)
    y_mla = _attn(q_t, k, v_t, batch=batch, seq=seq)
    kv = _mem_kv(mem.reshape(batch * mem_len, D_MODEL), row(g_mem[0]), w_xkv[0].astype(BF16),
                 batch=batch, mem_len=mem_len)

    tm_mid = min(TM_MID, seq)
    wr = jnp.zeros((D_MODEL, LANES), F32).at[:, :N_EXPERTS].set(w_router[0])
    wr_h = wr.astype(BF16)
    wr_l = (wr - wr_h.astype(F32)).astype(BF16)
    b_r = b_router[0].reshape(N_EXPERTS, 1).astype(F32)
    tri = (lax.broadcasted_iota(I32, (tm_mid, tm_mid), 0)
           < lax.broadcasted_iota(I32, (tm_mid, tm_mid), 1)).astype(BF16)
    wo = w_out[0].astype(BF16)
    x2, h3r, idx, gate, rank, cnt, runs_l = _mid(
        x2d, y_mla, y_conv, kv, wo[:MLA_WIDTH], wo[MLA_WIDTH:], row(g_xattn[0]), w_xq[0].astype(BF16),
        w_xo[0].astype(BF16), row(g_ffn[0]), wr_h, wr_l, b_r, tri, seq=seq, mem_len=mem_len, tm=tm_mid)

    tb = TB_EXPERT
    counts = cnt[:, 0]
    padded = (counts + tb - 1) // tb * tb
    pend = jnp.cumsum(padded)
    pstart = pend - padded
    n_slots = n * TOP_K + N_EXPERTS * tb
    nb = n_slots // tb
    blk_first = jnp.arange(nb, dtype=I32) * tb
    blk_e = jnp.minimum(jnp.sum((pend[None, :] <= blk_first[:, None]).astype(I32), axis=1),
                        N_EXPERTS - 1).astype(I32)
    nused = (pend[-1:] // tb).astype(I32)

    runs = runs_l[:, :, 0]
    run_len = jnp.concatenate([runs[1:], counts[None, :]], axis=0) - runs
    cstart = pstart[None, :] + runs
    cpieces = (run_len + COMB_PIECE - 1) // COMB_PIECE
    cloc = jnp.cumsum(cpieces * COMB_PIECE, axis=1) - cpieces * COMB_PIECE
    adj_l = jnp.broadcast_to((cloc - runs)[:, :, None], runs_l.shape).astype(I32)

    tm_comb = min(COMB_TM, tm_mid)
    lpos = _pos(idx, rank, adj_l, tm=tm_comb)
    tiled = lambda a: a.reshape(TOP_K, n // tm_comb, tm_comb).transpose(1, 0, 2).reshape(-1)
    lpos_tok = tiled(lpos)
    flat = lambda a: a.reshape(-1).astype(I32)
    tail_blocks = (n_slots - pend[-1:]) // RUN_BITS[0]
    fill_start = jnp.concatenate([pstart + counts, pend[-1:]]).astype(I32)
    fill_count = jnp.concatenate([padded - counts, tail_blocks]).astype(I32)
    xs = _dispatch(fill_start, fill_count, flat(cstart), flat(run_len),
                   flat(cloc), lpos_tok, h3r.reshape(n, PACK_TILES, LANES), n_slots=n_slots, tm=tm_comb)
    y = _expert(blk_e, nused, xs.reshape(n_slots * PACK_TILES, LANES), w_up[0],
                b_up[0].reshape(N_EXPERTS, 1, 2 * D_FF), w_down[0], b_down[0].reshape(N_EXPERTS, 1, D_MODEL), tb=tb)
    out = _combine(flat(cstart), flat(cpieces), flat(cloc), lpos_tok, tiled(gate),
                   x2, row(g_final), y, tm=tm_comb)
    return out.reshape(batch, seq, D_MODEL)
```

```python
import functools

import jax
import jax.numpy as jnp
from jax import lax
from jax.experimental import pallas as pl
from jax.experimental.pallas import tpu as pltpu

F32 = jnp.float32
BF16 = jnp.bfloat16
I32 = jnp.int32

D_MODEL = 1024
MLA_HEADS = 8
MLA_NOPE = 64
MLA_ROPE = 32
MLA_V = 64
MLA_WIDTH = MLA_HEADS * MLA_V
Q_LORA = 256
KV_LORA = 128
ROPE_THETA = 10000.0
CHUNK = 64
CONV_CH = 512
CONV_WIDTH = 31
X_HEADS = 4
X_HEAD_DIM = D_MODEL // X_HEADS
N_EXPERTS = 32
TOP_K = 4
D_FF = D_MODEL
SWIGLU_LIMIT = 7.0
SWIGLU_ALPHA = 1.702
NORM_EPS = 1e-5

LANES = 128
SUBLANES = 8
ROW_TILES = D_MODEL // LANES
PACK_TILES = ROW_TILES // 2
VMEM_LIMIT_BYTES = 56 * 1024 * 1024

HEAD_PAD = LANES
Q_PRESCALE = (MLA_NOPE + MLA_ROPE) ** -0.5 * 1.4426950408889634
CONV_HALO = 32

TM_MIX = 512
TQ_ATTN = 1024
TM_MID = 512
TB_EXPERT = 512
MID_SPLIT = 2
COMB_TM = 512


def _rms(x, g):
    return x * lax.rsqrt(jnp.mean(x * x, axis=-1, keepdims=True) + NORM_EPS) * g


def _cparams(sem):
    return pltpu.CompilerParams(dimension_semantics=sem, vmem_limit_bytes=VMEM_LIMIT_BYTES)


def _mix_in_kernel(x_ref, pos_ref, inv_ref, sgn_ref, gmix_ref, win_ref, gq_ref, wuq_ref, wuqs_ref,
                   gkv_ref, wukk_ref, wukv_ref, wdw_ref, bdw_ref, gln_ref, bln_ref,
                   q_ref, k_ref, v_ref, yc_ref, zbuf, zsh, *, tiles_per_batch, tm):
    i = pl.program_id(0)
    x = x_ref[...]
    h = _rms(x, gmix_ref[...]).astype(BF16)
    u = jnp.dot(h, win_ref[...], preferred_element_type=F32)
    c0 = Q_LORA + KV_LORA
    c1 = c0 + 2 * HEAD_PAD
    cq = u[:, 0:Q_LORA]
    ckv = u[:, Q_LORA:c0]
    kr = u[:, c0:c0 + HEAD_PAD]
    krs = u[:, c0 + HEAD_PAD:c1]
    a = u[:, c1:c1 + CONV_CH]
    gate = u[:, c1 + CONV_CH:c1 + 2 * CONV_CH]

    r0, r1 = MLA_NOPE, MLA_NOPE + MLA_ROPE
    ang_r = inv_ref[r0:r1, :] * pos_ref[...].astype(F32)
    cos_t = jnp.concatenate([jnp.ones((r0, tm), F32), jnp.cos(ang_r), jnp.ones((HEAD_PAD - r1, tm), F32)], axis=0)
    sin_t = jnp.concatenate([jnp.zeros((r0, tm), F32), jnp.sin(ang_r) * sgn_ref[r0:r1, :],
                             jnp.zeros((HEAD_PAD - r1, tm), F32)], axis=0)
    cosb = jnp.transpose(cos_t)
    sinb = jnp.transpose(sin_t)
    cos_q = cos_t * Q_PRESCALE
    sin_q = sin_t * Q_PRESCALE

    nt = (((1,), (1,)), ((), ()))
    cqn = _rms(cq, gq_ref[...]).astype(BF16)
    qm = lax.dot_general(wuq_ref[...], cqn, nt, preferred_element_type=F32)
    qs = lax.dot_general(wuqs_ref[...], cqn, nt, preferred_element_type=F32)
    for hd in range(MLA_HEADS):
        sl = slice(hd * HEAD_PAD, (hd + 1) * HEAD_PAD)
        q_ref[sl, :] = (qm[sl, :] * cos_q + qs[sl, :] * sin_q).astype(BF16)

    ckvn = _rms(ckv, gkv_ref[...]).astype(BF16)
    kk = jnp.dot(ckvn, wukk_ref[...], preferred_element_type=F32)
    v_ref[...] = lax.dot_general(wukv_ref[...], ckvn, nt, preferred_element_type=F32).astype(BF16)
    krot = kr * cosb + krs * sinb
    for hd in range(MLA_HEADS):
        sl = slice(hd * HEAD_PAD, (hd + 1) * HEAD_PAD)
        k_ref[:, sl] = (kk[:, sl] + krot).astype(BF16)

    z = a * jax.nn.sigmoid(gate)

    @pl.when(i % tiles_per_batch == 0)
    def _():
        zbuf[0:CONV_HALO, :] = jnp.zeros((CONV_HALO, CONV_CH), F32)

    zbuf[CONV_HALO:CONV_HALO + tm, :] = z
    off = CONV_HALO - (CONV_WIDTH - 1)
    rows = 32
    span = tm + CONV_HALO - SUBLANES
    step = 64
    for r in range(1, SUBLANES):
        for c0 in range(0, span, step):
            cl = min(step, span - c0)
            zsh[r - 1, c0:c0 + cl, :] = zbuf[c0 + r:c0 + r + cl, :]
    for r0 in range(0, tm, rows):
        acc = jnp.zeros((rows, CONV_CH), F32) + bdw_ref[...]
        for j in range(CONV_WIDTH):
            a8, ph = divmod(j + off, SUBLANES)
            lo_row = r0 + a8 * SUBLANES
            tap = zbuf[lo_row:lo_row + rows, :] if ph == 0 else zsh[ph - 1, lo_row:lo_row + rows, :]
            acc = acc + wdw_ref[j:j + 1, :] * tap
        mu = jnp.mean(acc, axis=-1, keepdims=True)
        cen = acc - mu
        var = jnp.mean(cen * cen, axis=-1, keepdims=True)
        y = cen * lax.rsqrt(var + NORM_EPS) * gln_ref[...] + bln_ref[...]
        yc_ref[r0:r0 + rows, :] = (y * jax.nn.sigmoid(y)).astype(BF16)
    zbuf[0:CONV_HALO, :] = zbuf[tm:tm + CONV_HALO, :]


def _mix_in(x2d, pos_row, inv_c, sgn_c, g_mix, w_in_p, g_q, w_uq_t, w_uq_st, g_kv, w_uk_k, w_uk_vt,
            w_dw, b_dw, g_ln, b_ln, *, seq):
    n = x2d.shape[0]
    tm = min(TM_MIX, seq)
    full = lambda a: pl.BlockSpec(a.shape, lambda i: (0,) * a.ndim)
    consts = [inv_c, sgn_c, g_mix, w_in_p, g_q, w_uq_t, w_uq_st, g_kv, w_uk_k, w_uk_vt, w_dw, b_dw, g_ln, b_ln]
    return pl.pallas_call(
        functools.partial(_mix_in_kernel, tiles_per_batch=seq // tm, tm=tm),
        grid=(n // tm,),
        in_specs=[pl.BlockSpec((tm, D_MODEL), lambda i: (i, 0)),
                  pl.BlockSpec((1, tm), lambda i: (0, i))] + [full(a) for a in consts],
        out_specs=[pl.BlockSpec((MLA_HEADS * HEAD_PAD, tm), lambda i: (0, i)),
                   pl.BlockSpec((tm, MLA_HEADS * HEAD_PAD), lambda i: (i, 0)),
                   pl.BlockSpec((MLA_WIDTH, tm), lambda i: (0, i)),
                   pl.BlockSpec((tm, CONV_CH), lambda i: (i, 0))],
        out_shape=[jax.ShapeDtypeStruct((MLA_HEADS * HEAD_PAD, n), BF16),
                   jax.ShapeDtypeStruct((n, MLA_HEADS * HEAD_PAD), BF16),
                   jax.ShapeDtypeStruct((MLA_WIDTH, n), BF16),
                   jax.ShapeDtypeStruct((n, CONV_CH), BF16)],
        scratch_shapes=[pltpu.VMEM((CONV_HALO + tm + SUBLANES, CONV_CH), F32),
                        pltpu.VMEM((SUBLANES - 1, CONV_HALO + tm, CONV_CH), F32)],
        compiler_params=_cparams(("arbitrary",)),
        name="mix_in",
    )(x2d, pos_row, *consts)


ATTN_GROUP = 4


TK_ATTN = 1024
DENOM_ROWS = 16


def _attn_kernel(q_ref, k_ref, v_ref, o_ref, *s_scr, tq):
    i = pl.program_id(1)
    tk = min(TK_ATTN, tq)
    per_q = tq // tk
    key_c = lax.broadcasted_iota(I32, (tk, tq), 0) // CHUNK
    qry_c = lax.broadcasted_iota(I32, (tk, tq), 1) // CHUNK

    for g in range(MLA_HEADS // ATTN_GROUP):
        heads = list(range(g * ATTN_GROUP, (g + 1) * ATTN_GROUP))
        qts = [q_ref[hd * HEAD_PAD:(hd + 1) * HEAD_PAD, :] for hd in heads]

        def step(j, carry, diag):
            start = pl.multiple_of(j * tk, tk)
            new = []
            for n_, hd in enumerate(heads):
                m, acc = carry[n_]
                kj = k_ref[pl.ds(start, tk), hd * HEAD_PAD:(hd + 1) * HEAD_PAD]
                vj = jnp.concatenate([v_ref[hd * MLA_V:(hd + 1) * MLA_V, pl.ds(start, tk)],
                                      jnp.ones((DENOM_ROWS, tk), BF16)], axis=0)
                s = jnp.dot(kj, qts[n_], preferred_element_type=F32)
                if diag is not None:
                    s = jnp.where(key_c + diag * (tk // CHUNK) <= qry_c, s, -jnp.inf)
                m_new = jnp.maximum(m, jnp.max(s, axis=0, keepdims=True))
                s_scr[n_][...] = s
                s = s_scr[n_][...]
                alpha = jnp.exp2(m - m_new)
                p = jnp.exp2(s - m_new)
                acc = alpha * acc + jnp.dot(vj, p.astype(BF16), preferred_element_type=F32)
                new.append((m_new, acc))
            return tuple(new)

        init = tuple((jnp.full((1, tq), -jnp.inf, F32), jnp.zeros((MLA_V + DENOM_ROWS, tq), F32)) for _ in heads)
        carry = lax.fori_loop(0, i * per_q, functools.partial(step, diag=None), init)
        for d in range(per_q):
            carry = step(i * per_q + d, carry, d)
        for n_, hd in enumerate(heads):
            _, acc = carry[n_]
            out = acc[0:MLA_V, :] / acc[MLA_V:MLA_V + 1, :]
            o_ref[:, hd * MLA_V:(hd + 1) * MLA_V] = jnp.transpose(out).astype(BF16)


def _attn(q_t, k, v_t, *, batch, seq):
    tq = min(TQ_ATTN, seq)
    nq = seq // tq
    return pl.pallas_call(
        functools.partial(_attn_kernel, tq=tq),
        grid=(batch, nq),
        in_specs=[pl.BlockSpec((MLA_HEADS * HEAD_PAD, tq), lambda b, i: (0, b * nq + i)),
                  pl.BlockSpec((seq, MLA_HEADS * HEAD_PAD), lambda b, i: (b, 0)),
                  pl.BlockSpec((MLA_WIDTH, seq), lambda b, i: (0, b))],
        out_specs=pl.BlockSpec((tq, MLA_WIDTH), lambda b, i: (b * nq + i, 0)),
        out_shape=jax.ShapeDtypeStruct((batch * seq, MLA_WIDTH), BF16),
        scratch_shapes=[pltpu.VMEM((min(TK_ATTN, tq), tq), F32) for _ in range(ATTN_GROUP)],
        compiler_params=_cparams(("arbitrary", "arbitrary")),
        name="mla_attn",
    )(q_t, k, v_t)


def _mem_kv_kernel(mem_ref, g_ref, w_ref, kv_ref):
    mn = _rms(mem_ref[...], g_ref[...]).astype(BF16)
    kv_ref[...] = jnp.dot(mn, w_ref[...], preferred_element_type=F32).astype(BF16)


def _mem_kv(mem2d, g_mem, w_xkv, *, batch, mem_len):
    return pl.pallas_call(
        _mem_kv_kernel,
        grid=(batch,),
        in_specs=[pl.BlockSpec((mem_len, D_MODEL), lambda b: (b, 0)),
                  pl.BlockSpec((1, D_MODEL), lambda b: (0, 0)),
                  pl.BlockSpec((D_MODEL, 2 * D_MODEL), lambda b: (0, 0))],
        out_specs=pl.BlockSpec((mem_len, 2 * D_MODEL), lambda b: (b, 0)),
        out_shape=jax.ShapeDtypeStruct((batch * mem_len, 2 * D_MODEL), BF16),
        compiler_params=_cparams(("arbitrary",)),
        name="mem_kv",
    )(mem2d, g_mem, w_xkv)


def _mid_kernel(x_ref, ya_ref, yc_ref, kv_ref, woa_ref, wob_ref, gx_ref, wxq_ref, wxo_ref, gf_ref,
                wrh_ref, wrl_ref, br_ref, tri_ref,
                x2_ref, h3_ref, idx_ref, gate_ref, rank_ref, cnt_ref, runs_ref, run_ref, *, tm):
    i = pl.program_id(0)

    @pl.when(i == 0)
    def _():
        run_ref[...] = jnp.zeros_like(run_ref)

    nt = (((1,), (1,)), ((), ()))
    half_d = D_MODEL // 2
    rows = tm // MID_SPLIT
    lgs = []
    for part in range(MID_SPLIT):
        rs = slice(part * rows, (part + 1) * rows)
        x1 = (x_ref[rs, :]
              + jnp.dot(ya_ref[rs, :], woa_ref[...], preferred_element_type=F32)
              + jnp.dot(yc_ref[rs, :], wob_ref[...], preferred_element_type=F32))
        h = _rms(x1, gx_ref[...]).astype(BF16)
        q = jnp.dot(h, wxq_ref[...], preferred_element_type=F32).astype(BF16)
        os = []
        for hd in range(X_HEADS):
            sl = slice(hd * X_HEAD_DIM, (hd + 1) * X_HEAD_DIM)
            vsl = slice(D_MODEL + hd * X_HEAD_DIM, D_MODEL + (hd + 1) * X_HEAD_DIM)
            s = lax.dot_general(q[:, sl], kv_ref[:, sl], nt, preferred_element_type=F32) * (X_HEAD_DIM ** -0.5)
            e = jnp.exp(s - jnp.max(s, axis=-1, keepdims=True))
            p = (e / jnp.sum(e, axis=-1, keepdims=True)).astype(BF16)
            os.append(jnp.dot(p, kv_ref[:, vsl], preferred_element_type=F32).astype(BF16))
        o = jnp.concatenate(os, axis=-1)
        x2 = x1 + jnp.dot(o, wxo_ref[...], preferred_element_type=F32)
        x2_ref[rs, :] = x2
        h3 = _rms(x2, gf_ref[...])
        hh = h3.astype(BF16)
        hf = hh.astype(F32)
        word = pltpu.pack_elementwise([h3[:, :half_d], h3[:, half_d:]], packed_dtype=BF16)
        for t in range(PACK_TILES):
            h3_ref[pl.ds(part * rows * PACK_TILES + t, rows, stride=PACK_TILES), :] = word[:, t * LANES:(t + 1) * LANES]

        hl = (h3 - hf).astype(BF16)
        lgs.append(jnp.dot(hh, wrh_ref[...], preferred_element_type=F32)
                   + (jnp.dot(hh, wrl_ref[...], preferred_element_type=F32)
                      + jnp.dot(hl, wrh_ref[...], preferred_element_type=F32)))
    lg = jnp.concatenate(lgs, axis=0)
    lgt = jnp.transpose(lg)[0:N_EXPERTS, :] + br_ref[...]
    eid = lax.broadcasted_iota(I32, (N_EXPERTS, tm), 0)
    vals, hots = [], []
    cur = lgt
    for k in range(TOP_K):
        mx = jnp.max(cur, axis=0, keepdims=True)
        ik = jnp.min(jnp.where(cur == mx, eid, N_EXPERTS), axis=0, keepdims=True)
        hot = eid == ik
        cur = jnp.where(hot, -jnp.inf, cur)
        idx_ref[k:k + 1, :] = ik
        vals.append(mx)
        hots.append(hot)
    es = [jnp.exp(v - vals[0]) for v in vals]
    den = es[0] + es[1] + es[2] + es[3]
    for k in range(TOP_K):
        gate_ref[k:k + 1, :] = es[k] / den
    cnt = jnp.zeros((N_EXPERTS, tm), F32)
    for k in range(TOP_K):
        cnt = cnt + hots[k].astype(F32)
    run = run_ref[:, 0:1]
    ctm = min(COMB_TM, tm)
    for j in range(tm // ctm):
        before = run if j == 0 else run + jnp.sum(cnt[:, :j * ctm], axis=1, keepdims=True)
        runs_ref[j] = jnp.broadcast_to(before, (N_EXPERTS, LANES)).astype(I32)
    tot = jnp.dot(cnt.astype(BF16), tri_ref[...], preferred_element_type=F32) + run
    for k in range(TOP_K):
        rank_ref[k:k + 1, :] = jnp.sum(jnp.where(hots[k], tot, 0.0), axis=0, keepdims=True).astype(I32)
    new_run = run + jnp.sum(cnt, axis=1, keepdims=True)
    run_ref[...] = jnp.broadcast_to(new_run, run_ref.shape)
    cnt_ref[...] = jnp.broadcast_to(new_run, cnt_ref.shape).astype(I32)


def _mid(x2d, ya, yc, kv, w_oa, w_ob, g_x, w_xq, w_xo, g_f, wr_h, wr_l, b_r, tri, *, seq, mem_len, tm):
    n = x2d.shape[0]
    tpb = seq // tm
    full = lambda a: pl.BlockSpec(a.shape, lambda i: (0,) * a.ndim)
    consts_a = [w_oa, w_ob, g_x, w_xq, w_xo, g_f, wr_h, wr_l, b_r, tri]
    return pl.pallas_call(
        functools.partial(_mid_kernel, tm=tm),
        grid=(n // tm,),
        in_specs=[pl.BlockSpec((tm, D_MODEL), lambda i: (i, 0)),
                  pl.BlockSpec((tm, MLA_WIDTH), lambda i: (i, 0)),
                  pl.BlockSpec((tm, CONV_CH), lambda i: (i, 0)),
                  pl.BlockSpec((mem_len, 2 * D_MODEL), lambda i: (i // tpb, 0))] + [full(a) for a in consts_a],
        out_specs=[pl.BlockSpec((tm, D_MODEL), lambda i: (i, 0)),
                   pl.BlockSpec((tm * PACK_TILES, LANES), lambda i: (i, 0)),
                   pl.BlockSpec((TOP_K, tm), lambda i: (0, i)),
                   pl.BlockSpec((TOP_K, tm), lambda i: (0, i)),
                   pl.BlockSpec((TOP_K, tm), lambda i: (0, i)),
                   pl.BlockSpec((N_EXPERTS, LANES), lambda i: (0, 0)),
                   pl.BlockSpec((tm // min(COMB_TM, tm), N_EXPERTS, LANES), lambda i: (i, 0, 0))],
        out_shape=[jax.ShapeDtypeStruct((n, D_MODEL), F32),
                   jax.ShapeDtypeStruct((n * PACK_TILES, LANES), jnp.uint32),
                   jax.ShapeDtypeStruct((TOP_K, n), I32),
                   jax.ShapeDtypeStruct((TOP_K, n), F32),
                   jax.ShapeDtypeStruct((TOP_K, n), I32),
                   jax.ShapeDtypeStruct((N_EXPERTS, LANES), I32),
                   jax.ShapeDtypeStruct((n // min(COMB_TM, tm), N_EXPERTS, LANES), I32)],
        scratch_shapes=[pltpu.VMEM((N_EXPERTS, LANES), F32)],
        compiler_params=_cparams(("arbitrary",)),
        name="mid",
    )(x2d, ya, yc, kv, *consts_a)


def _pos_kernel(idx_ref, rank_ref, adj_ref, lpos_ref, *, tm, group):
    i = pl.program_id(0)
    eid = lax.broadcasted_iota(I32, (N_EXPERTS, tm), 0)
    for j in range(group):
        half = ((i * group + j) % 2) * _comb_cap(tm)
        cols = slice(j * tm, (j + 1) * tm)
        adj = adj_ref[j][:, 0:1] + half
        for k in range(TOP_K):
            hot = eid == idx_ref[k:k + 1, cols]
            lpos_ref[k:k + 1, cols] = (jnp.sum(jnp.where(hot, adj, 0), axis=0, keepdims=True)
                                       + rank_ref[k:k + 1, cols]) * ROW_TILES


def _pos(idx, rank, adj_l, *, tm):
    n = idx.shape[1]
    n_tiles = n // tm
    group = 4 if n_tiles % 4 == 0 else 1
    return pl.pallas_call(
        functools.partial(_pos_kernel, tm=tm, group=group),
        grid=(n_tiles // group,),
        in_specs=[pl.BlockSpec((TOP_K, tm * group), lambda i: (0, i)),
                  pl.BlockSpec((TOP_K, tm * group), lambda i: (0, i)),
                  pl.BlockSpec((group, N_EXPERTS, LANES), lambda i: (i, 0, 0))],
        out_specs=pl.BlockSpec((TOP_K, tm * group), lambda i: (0, i)),
        out_shape=jax.ShapeDtypeStruct((TOP_K, n), I32),
        compiler_params=_cparams(("arbitrary",)),
        name="slot_pos",
    )(idx, rank, adj_l)


RUN_BITS = tuple(1 << b for b in range(9, -1, -1))


def _pow2_runs(length, fn):
    off = jnp.int32(0)
    for bit in RUN_BITS:
        take = (length & bit) != 0

        @pl.when(take)
        def _(off=off, bit=bit):
            fn(off, bit)

        off = off + jnp.where(take, bit, 0)


def _dispatch_kernel(pad_start_ref, pad_cnt_ref, cstart_ref, clen_ref, cloc_ref, lpos_ref, h3_ref, xs_hbm,
                     cbuf, sems, zsem, *, tm, n_tiles):
    i = pl.program_id(0)
    slot = i % 2
    cap = _comb_cap(tm)
    rows_per_tile = TOP_K * tm

    @pl.when(i == 0)
    def _():
        zrows = RUN_BITS[0]
        cbuf[pl.ds(cap, zrows)] = jnp.zeros((zrows, PACK_TILES, LANES), jnp.uint32)

        def pad_pass(start):
            def per_expert(e, c):
                def one(off, bit):
                    cp = pltpu.make_async_copy(cbuf.at[pl.ds(cap, bit)],
                                               xs_hbm.at[pl.ds(pad_start_ref[e] + off, bit)], zsem)
                    cp.start() if start else cp.wait()

                _pow2_runs(pad_cnt_ref[e], one)
                return c

            lax.fori_loop(0, N_EXPERTS, per_expert, 0)

        pad_pass(True)
        pad_pass(False)

        def tail_pass(start):
            def one(j, c):
                cp = pltpu.make_async_copy(cbuf.at[pl.ds(cap, zrows)],
                                           xs_hbm.at[pl.ds(pad_start_ref[N_EXPERTS] + j * zrows, zrows)], zsem)
                cp.start() if start else cp.wait()
                return c

            lax.fori_loop(0, pad_cnt_ref[N_EXPERTS], one, 0)

        tail_pass(True)
        tail_pass(False)

    def wait_tile(s):
        pltpu.make_async_copy(cbuf.at[pl.ds(s * cap, rows_per_tile)], xs_hbm.at[pl.ds(0, rows_per_tile)],
                              sems.at[s]).wait()

    @pl.when(i >= 2)
    def _():
        wait_tile(slot)

    def tok_group(tt, c):
        for u in range(COMB_UNROLL):
            t = tt * COMB_UNROLL + u
            row = h3_ref[t]
            for k in range(TOP_K):
                cbuf[lax.shift_right_logical(lpos_ref[k * tm + t], 3)] = row
        return c

    lax.fori_loop(0, tm // COMB_UNROLL, tok_group, 0)

    def per_run(e, c):
        src0 = slot * cap + cloc_ref[i * N_EXPERTS + e]
        dst0 = cstart_ref[i * N_EXPERTS + e]

        def one(off, bit):
            pltpu.make_async_copy(cbuf.at[pl.ds(src0 + off, bit)], xs_hbm.at[pl.ds(dst0 + off, bit)],
                                  sems.at[slot]).start()

        _pow2_runs(clen_ref[i * N_EXPERTS + e], one)
        return c

    lax.fori_loop(0, N_EXPERTS, per_run, 0)

    @pl.when(i == n_tiles - 1)
    def _():
        wait_tile(slot)
        if n_tiles >= 2:
            wait_tile(1 - slot)


def _dispatch(pad_start, pad_cnt, cstart, clen, cloc, lpos_tok, h3r, *, n_slots, tm):
    n = h3r.shape[0]
    n_tiles = n // tm
    gs = pltpu.PrefetchScalarGridSpec(
        num_scalar_prefetch=5,
        grid=(n_tiles,),
        in_specs=[pl.BlockSpec((TOP_K * tm,), lambda i, *_: (i,), memory_space=pltpu.SMEM),
                  pl.BlockSpec((tm, PACK_TILES, LANES), lambda i, *_: (i, 0, 0))],
        out_specs=pl.BlockSpec(memory_space=pl.ANY),
        scratch_shapes=[pltpu.VMEM((2 * _comb_cap(tm), PACK_TILES, LANES), jnp.uint32),
                        pltpu.SemaphoreType.DMA((2,)),
                        pltpu.SemaphoreType.DMA(())],
    )
    return pl.pallas_call(
        functools.partial(_dispatch_kernel, tm=tm, n_tiles=n_tiles),
        grid_spec=gs,
        out_shape=jax.ShapeDtypeStruct((n_slots, PACK_TILES, LANES), jnp.uint32),
        compiler_params=_cparams(("arbitrary",)),
        name="dispatch",
    )(pad_start, pad_cnt, cstart, clen, cloc, lpos_tok, h3r)


def _expert_kernel(blk_e_ref, nused_ref, x_ref, wu_ref, bu_ref, wd_ref, bd_ref, y_ref, wu_bf, wd_bf, *, tb):
    b = pl.program_id(0)

    @pl.when(b >= nused_ref[0])
    def _():
        y_ref[...] = jnp.zeros_like(y_ref)

    @pl.when(b < nused_ref[0])
    def _():
        prev = blk_e_ref[jnp.maximum(b - 1, 0)]

        @pl.when(jnp.logical_or(b == 0, blk_e_ref[b] != prev))
        def _():
            wu_bf[...] = wu_ref[...].astype(BF16)
            wd_bf[...] = wd_ref[...].astype(BF16)

        words = [x_ref[pl.ds(t, tb, stride=PACK_TILES), :] for t in range(PACK_TILES)]
        lo = [pltpu.unpack_elementwise(w, index=0, packed_dtype=BF16, unpacked_dtype=F32) for w in words]
        hi = [pltpu.unpack_elementwise(w, index=1, packed_dtype=BF16, unpacked_dtype=F32) for w in words]
        x = jnp.concatenate(lo + hi, axis=-1).astype(BF16)
        up = jnp.dot(x, wu_bf[...], preferred_element_type=F32) + bu_ref[...]
        glu = jnp.minimum(up[:, :D_FF], SWIGLU_LIMIT)
        lin = jnp.clip(up[:, D_FF:], -SWIGLU_LIMIT, SWIGLU_LIMIT)
        act = (glu * jax.nn.sigmoid(SWIGLU_ALPHA * glu) * (lin + 1.0)).astype(BF16)
        y = jnp.dot(act, wd_bf[...], preferred_element_type=F32) + bd_ref[...]
        for t in range(ROW_TILES):
            y_ref[pl.ds(t, tb, stride=ROW_TILES), :] = y[:, t * LANES:(t + 1) * LANES]


def _expert(blk_e, nused, xs, w_up, b_up, w_down, b_down, *, tb):
    n_slots = xs.shape[0] // PACK_TILES
    nb = n_slots // tb
    blk = lambda b, be, nu: jnp.minimum(b, nu[0] - 1)
    gs = pltpu.PrefetchScalarGridSpec(
        num_scalar_prefetch=2,
        grid=(nb + 1,),
        in_specs=[pl.BlockSpec((tb * PACK_TILES, LANES), lambda b, be, nu: (blk(b, be, nu), 0)),
                  pl.BlockSpec((None, D_MODEL, 2 * D_FF), lambda b, be, nu: (be[blk(b, be, nu)], 0, 0)),
                  pl.BlockSpec((None, 1, 2 * D_FF), lambda b, be, nu: (be[blk(b, be, nu)], 0, 0)),
                  pl.BlockSpec((None, D_FF, D_MODEL), lambda b, be, nu: (be[blk(b, be, nu)], 0, 0)),
                  pl.BlockSpec((None, 1, D_MODEL), lambda b, be, nu: (be[blk(b, be, nu)], 0, 0))],
        out_specs=pl.BlockSpec((tb * ROW_TILES, LANES), lambda b, be, nu: (b, 0)),
        scratch_shapes=[pltpu.VMEM((D_MODEL, 2 * D_FF), BF16),
                        pltpu.VMEM((D_FF, D_MODEL), BF16)],
    )
    return pl.pallas_call(
        functools.partial(_expert_kernel, tb=tb),
        grid_spec=gs,
        out_shape=jax.ShapeDtypeStruct(((n_slots + tb) * ROW_TILES, LANES), F32),
        compiler_params=_cparams(("arbitrary",)),
        name="expert_ffn",
    )(blk_e, nused, xs, w_up, b_up, w_down, b_down)


COMB_PIECE = 32
COMB_UNROLL = 16


def _comb_cap(tm):
    return TOP_K * tm + N_EXPERTS * (COMB_PIECE - 1)


def _combine_kernel(cstart_ref, cpieces_ref, cloc_ref, lpos_ref, gate_ref, x2_ref, gfin_ref, y_hbm, out_ref,
                    chunks, rt, sems, *, tm, n_tiles):
    i = pl.program_id(0)
    slot = i % 2
    piece_rows = COMB_PIECE * ROW_TILES

    def piece_copy(tile, e, p, to_slot):
        src = (cstart_ref[tile * N_EXPERTS + e] + p * COMB_PIECE) * ROW_TILES
        dst = (to_slot * _comb_cap(tm) + cloc_ref[tile * N_EXPERTS + e] + p * COMB_PIECE) * ROW_TILES
        return pltpu.make_async_copy(
            y_hbm.at[pl.ds(pl.multiple_of(src, ROW_TILES), piece_rows), :],
            chunks.at[pl.ds(pl.multiple_of(dst, ROW_TILES), piece_rows), :],
            sems.at[to_slot])

    def for_pieces(tile, to_slot, start):
        def per_e(e, c):
            def per_p(p, c2):
                cp = piece_copy(tile, e, p, to_slot)
                if start:
                    cp.start()
                else:
                    cp.wait()
                return c2

            lax.fori_loop(0, cpieces_ref[tile * N_EXPERTS + e], per_p, 0)
            return c

        lax.fori_loop(0, N_EXPERTS, per_e, 0)

    @pl.when(i == 0)
    def _():
        for_pieces(0, 0, True)

    @pl.when(i + 1 < n_tiles)
    def _():
        for_pieces(i + 1, 1 - slot, True)

    for_pieces(i, slot, False)

    def tok_group(tt, c):
        for u in range(COMB_UNROLL):
            t = tt * COMB_UNROLL + u
            acc = None
            for k in range(TOP_K):
                r = pl.multiple_of(lpos_ref[k * tm + t], ROW_TILES)
                term = gate_ref[k * tm + t] * chunks[pl.ds(r, ROW_TILES), :]
                acc = term if acc is None else acc + term
            rt[pl.ds(pl.multiple_of(t * ROW_TILES, ROW_TILES), ROW_TILES), :] = acc
        return c

    lax.fori_loop(0, tm // COMB_UNROLL, tok_group, 0)

    moe = jnp.concatenate([rt[pl.ds(s, tm, stride=ROW_TILES), :] for s in range(ROW_TILES)], axis=-1)
    out_ref[...] = _rms(x2_ref[...] + moe, gfin_ref[...])


def _combine(cstart, cpieces, cloc, lpos, gate, x2, g_final, y, *, tm):
    n = x2.shape[0]
    n_tiles = n // tm
    gs = pltpu.PrefetchScalarGridSpec(
        num_scalar_prefetch=3,
        grid=(n_tiles,),
        in_specs=[pl.BlockSpec((TOP_K * tm,), lambda i, a, b, c: (i,), memory_space=pltpu.SMEM),
                  pl.BlockSpec((TOP_K * tm,), lambda i, a, b, c: (i,), memory_space=pltpu.SMEM),
                  pl.BlockSpec((tm, D_MODEL), lambda i, a, b, c: (i, 0)),
                  pl.BlockSpec((1, D_MODEL), lambda i, a, b, c: (0, 0)),
                  pl.BlockSpec(memory_space=pl.ANY)],
        out_specs=pl.BlockSpec((tm, D_MODEL), lambda i, a, b, c: (i, 0)),
        scratch_shapes=[pltpu.VMEM((2 * _comb_cap(tm) * ROW_TILES, LANES), F32),
                        pltpu.VMEM((tm * ROW_TILES, LANES), F32),
                        pltpu.SemaphoreType.DMA((2,))],
    )
    return pl.pallas_call(
        functools.partial(_combine_kernel, tm=tm, n_tiles=n_tiles),
        grid_spec=gs,
        out_shape=jax.ShapeDtypeStruct((n, D_MODEL), F32),
        compiler_params=_cparams(("arbitrary",)),
        name="combine",
    )(cstart, cpieces, cloc, lpos, gate, x2, g_final, y)


def _head_pad_cols(w, per_head_in, take, place):
    kdim = w.shape[0]
    w3 = w.reshape(kdim, MLA_HEADS, per_head_in)
    out = jnp.zeros((kdim, MLA_HEADS, HEAD_PAD), w.dtype)
    for (t0, t1), p0 in zip(take, place):
        out = out.at[:, :, p0:p0 + (t1 - t0)].set(w3[:, :, t0:t1])
    return out.reshape(kdim, MLA_HEADS * HEAD_PAD)


def kernel(x, mem, positions, g_mix, w_in, g_q, w_uq, g_kv, w_ukv, w_dw, b_dw, g_conv_ln, b_conv_ln, w_out,
           g_xattn, g_mem, w_xq, w_xkv, w_xo, g_ffn, w_router, b_router, w_up, b_up, w_down, b_down, g_final):
    batch, seq, _ = x.shape
    mem_len = mem.shape[1]
    n = batch * seq
    half = MLA_ROPE // 2
    r0 = MLA_NOPE
    assert w_in.shape[0] == 1, "one trunk layer"

    row = lambda v: v.reshape(1, -1).astype(F32)

    wi = w_in[0]
    o1, o2, o3 = Q_LORA, Q_LORA + KV_LORA, Q_LORA + KV_LORA + MLA_ROPE
    kr_blk = jnp.zeros((D_MODEL, HEAD_PAD), F32).at[:, r0:r0 + MLA_ROPE].set(wi[:, o2:o3])
    kr_swp = (jnp.zeros((D_MODEL, HEAD_PAD), F32)
              .at[:, r0:r0 + half].set(wi[:, o2 + half:o3])
              .at[:, r0 + half:r0 + MLA_ROPE].set(wi[:, o2:o2 + half]))
    w_in_p = jnp.concatenate([wi[:, :o2], kr_blk, kr_swp, wi[:, o3:]], axis=1).astype(BF16)
    per_q = MLA_NOPE + MLA_ROPE
    w_uq_p = _head_pad_cols(w_uq[0], per_q, [(0, per_q)], [0]).astype(BF16)
    w_uq_s = _head_pad_cols(w_uq[0], per_q, [(MLA_NOPE + half, per_q), (MLA_NOPE, MLA_NOPE + half)],
                            [r0, r0 + half]).astype(BF16)
    per_kv = MLA_NOPE + MLA_V
    w_uk_k = _head_pad_cols(w_ukv[0], per_kv, [(0, MLA_NOPE)], [0]).astype(BF16)
    w_uk_v = (w_ukv[0].reshape(KV_LORA, MLA_HEADS, per_kv)[:, :, MLA_NOPE:]
              .reshape(KV_LORA, MLA_WIDTH).astype(BF16))
    inv = ROPE_THETA ** (-jnp.arange(0, MLA_ROPE, 2, dtype=F32) / MLA_ROPE)
    inv_c = (jnp.zeros((LANES, 1), F32).at[r0:r0 + half, 0].set(inv).at[r0 + half:r0 + MLA_ROPE, 0].set(inv))
    sgn_c = (jnp.zeros((LANES, 1), F32).at[r0:r0 + half, 0].set(-1.0).at[r0 + half:r0 + MLA_ROPE, 0].set(1.0))

    x2d = x.reshape(n, D_MODEL)
    pos_row = positions.reshape(1, n).astype(I32)

    q_t, k, v_t, y_conv = _mix_in(x2d, pos_row, inv_c, sgn_c, row(g_mix[0]), w_in_p, row(g_q[0]),
                                  jnp.transpose(w_uq_p), jnp.transpose(w_uq_s), row(g_kv[0]), w_uk_k,
                                  jnp.transpose(w_uk_v), w_dw[0].astype(F32), row(b_dw[0]),
                                  row(g_conv_ln[0]), row(b_conv_ln[0]), seq=seq)
    y_mla = _attn(q_t, k, v_t, batch=batch, seq=seq)
    kv = _mem_kv(mem.reshape(batch * mem_len, D_MODEL), row(g_mem[0]), w_xkv[0].astype(BF16),
                 batch=batch, mem_len=mem_len)

    tm_mid = min(TM_MID, seq)
    wr = jnp.zeros((D_MODEL, LANES), F32).at[:, :N_EXPERTS].set(w_router[0])
    wr_h = wr.astype(BF16)
    wr_l = (wr - wr_h.astype(F32)).astype(BF16)
    b_r = b_router[0].reshape(N_EXPERTS, 1).astype(F32)
    tri = (lax.broadcasted_iota(I32, (tm_mid, tm_mid), 0)
           < lax.broadcasted_iota(I32, (tm_mid, tm_mid), 1)).astype(BF16)
    wo = w_out[0].astype(BF16)
    x2, h3r, idx, gate, rank, cnt, runs_l = _mid(
        x2d, y_mla, y_conv, kv, wo[:MLA_WIDTH], wo[MLA_WIDTH:], row(g_xattn[0]), w_xq[0].astype(BF16),
        w_xo[0].astype(BF16), row(g_ffn[0]), wr_h, wr_l, b_r, tri, seq=seq, mem_len=mem_len, tm=tm_mid)

    tb = TB_EXPERT
    counts = cnt[:, 0]
    padded = (counts + tb - 1) // tb * tb
    pend = jnp.cumsum(padded)
    pstart = pend - padded
    n_slots = n * TOP_K + N_EXPERTS * tb
    nb = n_slots // tb
    blk_first = jnp.arange(nb, dtype=I32) * tb
    blk_e = jnp.minimum(jnp.sum((pend[None, :] <= blk_first[:, None]).astype(I32), axis=1),
                        N_EXPERTS - 1).astype(I32)
    nused = (pend[-1:] // tb).astype(I32)

    runs = runs_l[:, :, 0]
    run_len = jnp.concatenate([runs[1:], counts[None, :]], axis=0) - runs
    cstart = pstart[None, :] + runs
    cpieces = (run_len + COMB_PIECE - 1) // COMB_PIECE
    cloc = jnp.cumsum(cpieces * COMB_PIECE, axis=1) - cpieces * COMB_PIECE
    adj_l = jnp.broadcast_to((cloc - runs)[:, :, None], runs_l.shape).astype(I32)

    tm_comb = min(COMB_TM, tm_mid)
    lpos = _pos(idx, rank, adj_l, tm=tm_comb)
    tiled = lambda a: a.reshape(TOP_K, n // tm_comb, tm_comb).transpose(1, 0, 2).reshape(-1)
    lpos_tok = tiled(lpos)
    flat = lambda a: a.reshape(-1).astype(I32)
    tail_blocks = (n_slots - pend[-1:]) // RUN_BITS[0]
    fill_start = jnp.concatenate([pstart + counts, pend[-1:]]).astype(I32)
    fill_count = jnp.concatenate([padded - counts, tail_blocks]).astype(I32)
    xs = _dispatch(fill_start, fill_count, flat(cstart), flat(run_len),
                   flat(cloc), lpos_tok, h3r.reshape(n, PACK_TILES, LANES), n_slots=n_slots, tm=tm_comb)
    y = _expert(blk_e, nused, xs.reshape(n_slots * PACK_TILES, LANES), w_up[0],
                b_up[0].reshape(N_EXPERTS, 1, 2 * D_FF), w_down[0], b_down[0].reshape(N_EXPERTS, 1, D_MODEL), tb=tb)
    out = _combine(flat(cstart), flat(cpieces), flat(cloc), lpos_tok, tiled(gate),
                   x2, row(g_final), y, tm=tm_comb)
    return out.reshape(batch, seq, D_MODEL)
```

```python
import functools

import jax
import jax.numpy as jnp
from jax import lax
from jax.experimental import pallas as pl
from jax.experimental.pallas import tpu as pltpu

F32 = jnp.float32
BF16 = jnp.bfloat16
I32 = jnp.int32

D_MODEL = 1024
MLA_HEADS = 8
MLA_NOPE = 64
MLA_ROPE = 32
MLA_V = 64
MLA_WIDTH = MLA_HEADS * MLA_V
Q_LORA = 256
KV_LORA = 128
ROPE_THETA = 10000.0
CHUNK = 64
CONV_CH = 512
CONV_WIDTH = 31
X_HEADS = 4
X_HEAD_DIM = D_MODEL // X_HEADS
N_EXPERTS = 32
TOP_K = 4
D_FF = D_MODEL
SWIGLU_LIMIT = 7.0
SWIGLU_ALPHA = 1.702
NORM_EPS = 1e-5

LANES = 128
SUBLANES = 8
ROW_TILES = D_MODEL // LANES
PACK_TILES = ROW_TILES // 2
VMEM_LIMIT_BYTES = 56 * 1024 * 1024

HEAD_PAD = LANES
Q_PRESCALE = (MLA_NOPE + MLA_ROPE) ** -0.5 * 1.4426950408889634
CONV_HALO = 32

TM_MIX = 512
TQ_ATTN = 1024
TM_MID = 512
TB_EXPERT = 512
MID_SPLIT = 2
COMB_TM = 512


def _rms(x, g):
    return x * lax.rsqrt(jnp.mean(x * x, axis=-1, keepdims=True) + NORM_EPS) * g


def _cparams(sem):
    return pltpu.CompilerParams(dimension_semantics=sem, vmem_limit_bytes=VMEM_LIMIT_BYTES)


def _mix_in_kernel(x_ref, pos_ref, inv_ref, sgn_ref, gmix_ref, win_ref, gq_ref, wuq_ref, wuqs_ref,
                   gkv_ref, wukk_ref, wukv_ref, wdw_ref, bdw_ref, gln_ref, bln_ref,
                   q_ref, k_ref, v_ref, yc_ref, zbuf, zsh, *, tiles_per_batch, tm):
    i = pl.program_id(0)
    x = x_ref[...]
    h = _rms(x, gmix_ref[...]).astype(BF16)
    u = jnp.dot(h, win_ref[...], preferred_element_type=F32)
    c0 = Q_LORA + KV_LORA
    c1 = c0 + 2 * HEAD_PAD
    cq = u[:, 0:Q_LORA]
    ckv = u[:, Q_LORA:c0]
    kr = u[:, c0:c0 + HEAD_PAD]
    krs = u[:, c0 + HEAD_PAD:c1]
    a = u[:, c1:c1 + CONV_CH]
    gate = u[:, c1 + CONV_CH:c1 + 2 * CONV_CH]

    r0, r1 = MLA_NOPE, MLA_NOPE + MLA_ROPE
    ang_r = inv_ref[r0:r1, :] * pos_ref[...].astype(F32)
    cos_t = jnp.concatenate([jnp.ones((r0, tm), F32), jnp.cos(ang_r), jnp.ones((HEAD_PAD - r1, tm), F32)], axis=0)
    sin_t = jnp.concatenate([jnp.zeros((r0, tm), F32), jnp.sin(ang_r) * sgn_ref[r0:r1, :],
                             jnp.zeros((HEAD_PAD - r1, tm), F32)], axis=0)
    cosb = jnp.transpose(cos_t)
    sinb = jnp.transpose(sin_t)
    cos_q = cos_t * Q_PRESCALE
    sin_q = sin_t * Q_PRESCALE

    nt = (((1,), (1,)), ((), ()))
    cqn = _rms(cq, gq_ref[...]).astype(BF16)
    qm = lax.dot_general(wuq_ref[...], cqn, nt, preferred_element_type=F32)
    qs = lax.dot_general(wuqs_ref[...], cqn, nt, preferred_element_type=F32)
    for hd in range(MLA_HEADS):
        sl = slice(hd * HEAD_PAD, (hd + 1) * HEAD_PAD)
        q_ref[sl, :] = (qm[sl, :] * cos_q + qs[sl, :] * sin_q).astype(BF16)

    ckvn = _rms(ckv, gkv_ref[...]).astype(BF16)
    kk = jnp.dot(ckvn, wukk_ref[...], preferred_element_type=F32)
    v_ref[...] = lax.dot_general(wukv_ref[...], ckvn, nt, preferred_element_type=F32).astype(BF16)
    krot = kr * cosb + krs * sinb
    for hd in range(MLA_HEADS):
        sl = slice(hd * HEAD_PAD, (hd + 1) * HEAD_PAD)
        k_ref[:, sl] = (kk[:, sl] + krot).astype(BF16)

    z = a * jax.nn.sigmoid(gate)

    @pl.when(i % tiles_per_batch == 0)
    def _():
        zbuf[0:CONV_HALO, :] = jnp.zeros((CONV_HALO, CONV_CH), F32)

    zbuf[CONV_HALO:CONV_HALO + tm, :] = z
    off = CONV_HALO - (CONV_WIDTH - 1)
    rows = 32
    span = tm + CONV_HALO - SUBLANES
    step = 64
    for r in range(1, SUBLANES):
        for c0 in range(0, span, step):
            cl = min(step, span - c0)
            zsh[r - 1, c0:c0 + cl, :] = zbuf[c0 + r:c0 + r + cl, :]
    for r0 in range(0, tm, rows):
        acc = jnp.zeros((rows, CONV_CH), F32) + bdw_ref[...]
        for j in range(CONV_WIDTH):
            a8, ph = divmod(j + off, SUBLANES)
            lo_row = r0 + a8 * SUBLANES
            tap = zbuf[lo_row:lo_row + rows, :] if ph == 0 else zsh[ph - 1, lo_row:lo_row + rows, :]
            acc = acc + wdw_ref[j:j + 1, :] * tap
        mu = jnp.mean(acc, axis=-1, keepdims=True)
        cen = acc - mu
        var = jnp.mean(cen * cen, axis=-1, keepdims=True)
        y = cen * lax.rsqrt(var + NORM_EPS) * gln_ref[...] + bln_ref[...]
        yc_ref[r0:r0 + rows, :] = (y * jax.nn.sigmoid(y)).astype(BF16)
    zbuf[0:CONV_HALO, :] = zbuf[tm:tm + CONV_HALO, :]


def _mix_in(x2d, pos_row, inv_c, sgn_c, g_mix, w_in_p, g_q, w_uq_t, w_uq_st, g_kv, w_uk_k, w_uk_vt,
            w_dw, b_dw, g_ln, b_ln, *, seq):
    n = x2d.shape[0]
    tm = min(TM_MIX, seq)
    full = lambda a: pl.BlockSpec(a.shape, lambda i: (0,) * a.ndim)
    consts = [inv_c, sgn_c, g_mix, w_in_p, g_q, w_uq_t, w_uq_st, g_kv, w_uk_k, w_uk_vt, w_dw, b_dw, g_ln, b_ln]
    return pl.pallas_call(
        functools.partial(_mix_in_kernel, tiles_per_batch=seq // tm, tm=tm),
        grid=(n // tm,),
        in_specs=[pl.BlockSpec((tm, D_MODEL), lambda i: (i, 0)),
                  pl.BlockSpec((1, tm), lambda i: (0, i))] + [full(a) for a in consts],
        out_specs=[pl.BlockSpec((MLA_HEADS * HEAD_PAD, tm), lambda i: (0, i)),
                   pl.BlockSpec((tm, MLA_HEADS * HEAD_PAD), lambda i: (i, 0)),
                   pl.BlockSpec((MLA_WIDTH, tm), lambda i: (0, i)),
                   pl.BlockSpec((tm, CONV_CH), lambda i: (i, 0))],
        out_shape=[jax.ShapeDtypeStruct((MLA_HEADS * HEAD_PAD, n), BF16),
                   jax.ShapeDtypeStruct((n, MLA_HEADS * HEAD_PAD), BF16),
                   jax.ShapeDtypeStruct((MLA_WIDTH, n), BF16),
                   jax.ShapeDtypeStruct((n, CONV_CH), BF16)],
        scratch_shapes=[pltpu.VMEM((CONV_HALO + tm + SUBLANES, CONV_CH), F32),
                        pltpu.VMEM((SUBLANES - 1, CONV_HALO + tm, CONV_CH), F32)],
        compiler_params=_cparams(("arbitrary",)),
        name="mix_in",
    )(x2d, pos_row, *consts)


ATTN_GROUP = 4


TK_ATTN = 1024
DENOM_ROWS = 16


def _attn_kernel(q_ref, k_ref, v_ref, o_ref, *s_scr, tq):
    i = pl.program_id(1)
    tk = min(TK_ATTN, tq)
    per_q = tq // tk
    key_c = lax.broadcasted_iota(I32, (tk, tq), 0) // CHUNK
    qry_c = lax.broadcasted_iota(I32, (tk, tq), 1) // CHUNK

    for g in range(MLA_HEADS // ATTN_GROUP):
        heads = list(range(g * ATTN_GROUP, (g + 1) * ATTN_GROUP))
        qts = [q_ref[hd * HEAD_PAD:(hd + 1) * HEAD_PAD, :] for hd in heads]

        def step(j, carry, diag):
            start = pl.multiple_of(j * tk, tk)
            new = []
            for n_, hd in enumerate(heads):
                m, acc = carry[n_]
                kj = k_ref[pl.ds(start, tk), hd * HEAD_PAD:(hd + 1) * HEAD_PAD]
                vj = jnp.concatenate([v_ref[hd * MLA_V:(hd + 1) * MLA_V, pl.ds(start, tk)],
                                      jnp.ones((DENOM_ROWS, tk), BF16)], axis=0)
                s = jnp.dot(kj, qts[n_], preferred_element_type=F32)
                if diag is not None:
                    s = jnp.where(key_c + diag * (tk // CHUNK) <= qry_c, s, -jnp.inf)
                m_new = jnp.maximum(m, jnp.max(s, axis=0, keepdims=True))
                s_scr[n_][...] = s
                s = s_scr[n_][...]
                alpha = jnp.exp2(m - m_new)
                p = jnp.exp2(s - m_new)
                acc = alpha * acc + jnp.dot(vj, p.astype(BF16), preferred_element_type=F32)
                new.append((m_new, acc))
            return tuple(new)

        init = tuple((jnp.full((1, tq), -jnp.inf, F32), jnp.zeros((MLA_V + DENOM_ROWS, tq), F32)) for _ in heads)
        carry = lax.fori_loop(0, i * per_q, functools.partial(step, diag=None), init)
        for d in range(per_q):
            carry = step(i * per_q + d, carry, d)
        for n_, hd in enumerate(heads):
            _, acc = carry[n_]
            out = acc[0:MLA_V, :] / acc[MLA_V:MLA_V + 1, :]
            o_ref[:, hd * MLA_V:(hd + 1) * MLA_V] = jnp.transpose(out).astype(BF16)


def _attn(q_t, k, v_t, *, batch, seq):
    tq = min(TQ_ATTN, seq)
    nq = seq // tq
    return pl.pallas_call(
        functools.partial(_attn_kernel, tq=tq),
        grid=(batch, nq),
        in_specs=[pl.BlockSpec((MLA_HEADS * HEAD_PAD, tq), lambda b, i: (0, b * nq + i)),
                  pl.BlockSpec((seq, MLA_HEADS * HEAD_PAD), lambda b, i: (b, 0)),
                  pl.BlockSpec((MLA_WIDTH, seq), lambda b, i: (0, b))],
        out_specs=pl.BlockSpec((tq, MLA_WIDTH), lambda b, i: (b * nq + i, 0)),
        out_shape=jax.ShapeDtypeStruct((batch * seq, MLA_WIDTH), BF16),
        scratch_shapes=[pltpu.VMEM((min(TK_ATTN, tq), tq), F32) for _ in range(ATTN_GROUP)],
        compiler_params=_cparams(("arbitrary", "arbitrary")),
        name="mla_attn",
    )(q_t, k, v_t)


def _mem_kv_kernel(mem_ref, g_ref, w_ref, kv_ref):
    mn = _rms(mem_ref[...], g_ref[...]).astype(BF16)
    kv_ref[...] = jnp.dot(mn, w_ref[...], preferred_element_type=F32).astype(BF16)


def _mem_kv(mem2d, g_mem, w_xkv, *, batch, mem_len):
    return pl.pallas_call(
        _mem_kv_kernel,
        grid=(batch,),
        in_specs=[pl.BlockSpec((mem_len, D_MODEL), lambda b: (b, 0)),
                  pl.BlockSpec((1, D_MODEL), lambda b: (0, 0)),
                  pl.BlockSpec((D_MODEL, 2 * D_MODEL), lambda b: (0, 0))],
        out_specs=pl.BlockSpec((mem_len, 2 * D_MODEL), lambda b: (b, 0)),
        out_shape=jax.ShapeDtypeStruct((batch * mem_len, 2 * D_MODEL), BF16),
        compiler_params=_cparams(("arbitrary",)),
        name="mem_kv",
    )(mem2d, g_mem, w_xkv)


def _mid_kernel(x_ref, ya_ref, yc_ref, kv_ref, woa_ref, wob_ref, gx_ref, wxq_ref, wxo_ref, gf_ref,
                wrh_ref, wrl_ref, br_ref, tri_ref,
                x2_ref, h3_ref, idx_ref, gate_ref, rank_ref, cnt_ref, runs_ref, run_ref, *, tm):
    i = pl.program_id(0)

    @pl.when(i == 0)
    def _():
        run_ref[...] = jnp.zeros_like(run_ref)

    nt = (((1,), (1,)), ((), ()))
    half_d = D_MODEL // 2
    rows = tm // MID_SPLIT
    lgs = []
    for part in range(MID_SPLIT):
        rs = slice(part * rows, (part + 1) * rows)
        x1 = (x_ref[rs, :]
              + jnp.dot(ya_ref[rs, :], woa_ref[...], preferred_element_type=F32)
              + jnp.dot(yc_ref[rs, :], wob_ref[...], preferred_element_type=F32))
        h = _rms(x1, gx_ref[...]).astype(BF16)
        q = jnp.dot(h, wxq_ref[...], preferred_element_type=F32).astype(BF16)
        os = []
        for hd in range(X_HEADS):
            sl = slice(hd * X_HEAD_DIM, (hd + 1) * X_HEAD_DIM)
            vsl = slice(D_MODEL + hd * X_HEAD_DIM, D_MODEL + (hd + 1) * X_HEAD_DIM)
            s = lax.dot_general(q[:, sl], kv_ref[:, sl], nt, preferred_element_type=F32) * (X_HEAD_DIM ** -0.5)
            e = jnp.exp(s - jnp.max(s, axis=-1, keepdims=True))
            p = (e / jnp.sum(e, axis=-1, keepdims=True)).astype(BF16)
            os.append(jnp.dot(p, kv_ref[:, vsl], preferred_element_type=F32).astype(BF16))
        o = jnp.concatenate(os, axis=-1)
        x2 = x1 + jnp.dot(o, wxo_ref[...], preferred_element_type=F32)
        x2_ref[rs, :] = x2
        h3 = _rms(x2, gf_ref[...])
        hh = h3.astype(BF16)
        hf = hh.astype(F32)
        word = pltpu.pack_elementwise([h3[:, :half_d], h3[:, half_d:]], packed_dtype=BF16)
        for t in range(PACK_TILES):
            h3_ref[pl.ds(part * rows * PACK_TILES + t, rows, stride=PACK_TILES), :] = word[:, t * LANES:(t + 1) * LANES]

        hl = (h3 - hf).astype(BF16)
        lgs.append(jnp.dot(hh, wrh_ref[...], preferred_element_type=F32)
                   + (jnp.dot(hh, wrl_ref[...], preferred_element_type=F32)
                      + jnp.dot(hl, wrh_ref[...], preferred_element_type=F32)))
    lg = jnp.concatenate(lgs, axis=0)
    lgt = jnp.transpose(lg)[0:N_EXPERTS, :] + br_ref[...]
    eid = lax.broadcasted_iota(I32, (N_EXPERTS, tm), 0)
    vals, hots = [], []
    cur = lgt
    for k in range(TOP_K):
        mx = jnp.max(cur, axis=0, keepdims=True)
        ik = jnp.min(jnp.where(cur == mx, eid, N_EXPERTS), axis=0, keepdims=True)
        hot = eid == ik
        cur = jnp.where(hot, -jnp.inf, cur)
        idx_ref[k:k + 1, :] = ik
        vals.append(mx)
        hots.append(hot)
    es = [jnp.exp(v - vals[0]) for v in vals]
    den = es[0] + es[1] + es[2] + es[3]
    for k in range(TOP_K):
        gate_ref[k:k + 1, :] = es[k] / den
    cnt = jnp.zeros((N_EXPERTS, tm), F32)
    for k in range(TOP_K):
        cnt = cnt + hots[k].astype(F32)
    run = run_ref[:, 0:1]
    ctm = min(COMB_TM, tm)
    for j in range(tm // ctm):
        before = run if j == 0 else run + jnp.sum(cnt[:, :j * ctm], axis=1, keepdims=True)
        runs_ref[j] = jnp.broadcast_to(before, (N_EXPERTS, LANES)).astype(I32)
    tot = jnp.dot(cnt.astype(BF16), tri_ref[...], preferred_element_type=F32) + run
    for k in range(TOP_K):
        rank_ref[k:k + 1, :] = jnp.sum(jnp.where(hots[k], tot, 0.0), axis=0, keepdims=True).astype(I32)
    new_run = run + jnp.sum(cnt, axis=1, keepdims=True)
    run_ref[...] = jnp.broadcast_to(new_run, run_ref.shape)
    cnt_ref[...] = jnp.broadcast_to(new_run, cnt_ref.shape).astype(I32)


def _mid(x2d, ya, yc, kv, w_oa, w_ob, g_x, w_xq, w_xo, g_f, wr_h, wr_l, b_r, tri, *, seq, mem_len, tm):
    n = x2d.shape[0]
    tpb = seq // tm
    full = lambda a: pl.BlockSpec(a.shape, lambda i: (0,) * a.ndim)
    consts_a = [w_oa, w_ob, g_x, w_xq, w_xo, g_f, wr_h, wr_l, b_r, tri]
    return pl.pallas_call(
        functools.partial(_mid_kernel, tm=tm),
        grid=(n // tm,),
        in_specs=[pl.BlockSpec((tm, D_MODEL), lambda i: (i, 0)),
                  pl.BlockSpec((tm, MLA_WIDTH), lambda i: (i, 0)),
                  pl.BlockSpec((tm, CONV_CH), lambda i: (i, 0)),
                  pl.BlockSpec((mem_len, 2 * D_MODEL), lambda i: (i // tpb, 0))] + [full(a) for a in consts_a],
        out_specs=[pl.BlockSpec((tm, D_MODEL), lambda i: (i, 0)),
                   pl.BlockSpec((tm * PACK_TILES, LANES), lambda i: (i, 0)),
                   pl.BlockSpec((TOP_K, tm), lambda i: (0, i)),
                   pl.BlockSpec((TOP_K, tm), lambda i: (0, i)),
                   pl.BlockSpec((TOP_K, tm), lambda i: (0, i)),
                   pl.BlockSpec((N_EXPERTS, LANES), lambda i: (0, 0)),
                   pl.BlockSpec((tm // min(COMB_TM, tm), N_EXPERTS, LANES), lambda i: (i, 0, 0))],
        out_shape=[jax.ShapeDtypeStruct((n, D_MODEL), F32),
                   jax.ShapeDtypeStruct((n * PACK_TILES, LANES), jnp.uint32),
                   jax.ShapeDtypeStruct((TOP_K, n), I32),
                   jax.ShapeDtypeStruct((TOP_K, n), F32),
                   jax.ShapeDtypeStruct((TOP_K, n), I32),
                   jax.ShapeDtypeStruct((N_EXPERTS, LANES), I32),
                   jax.ShapeDtypeStruct((n // min(COMB_TM, tm), N_EXPERTS, LANES), I32)],
        scratch_shapes=[pltpu.VMEM((N_EXPERTS, LANES), F32)],
        compiler_params=_cparams(("arbitrary",)),
        name="mid",
    )(x2d, ya, yc, kv, *consts_a)


def _pos_kernel(idx_ref, rank_ref, adj_ref, lpos_ref, *, tm, group):
    i = pl.program_id(0)
    eid = lax.broadcasted_iota(I32, (N_EXPERTS, tm), 0)
    for j in range(group):
        half = ((i * group + j) % 2) * _comb_cap(tm)
        cols = slice(j * tm, (j + 1) * tm)
        adj = adj_ref[j][:, 0:1] + half
        for k in range(TOP_K):
            hot = eid == idx_ref[k:k + 1, cols]
            lpos_ref[k:k + 1, cols] = (jnp.sum(jnp.where(hot, adj, 0), axis=0, keepdims=True)
                                       + rank_ref[k:k + 1, cols]) * ROW_TILES


def _pos(idx, rank, adj_l, *, tm):
    n = idx.shape[1]
    n_tiles = n // tm
    group = 4 if n_tiles % 4 == 0 else 1
    return pl.pallas_call(
        functools.partial(_pos_kernel, tm=tm, group=group),
        grid=(n_tiles // group,),
        in_specs=[pl.BlockSpec((TOP_K, tm * group), lambda i: (0, i)),
                  pl.BlockSpec((TOP_K, tm * group), lambda i: (0, i)),
                  pl.BlockSpec((group, N_EXPERTS, LANES), lambda i: (i, 0, 0))],
        out_specs=pl.BlockSpec((TOP_K, tm * group), lambda i: (0, i)),
        out_shape=jax.ShapeDtypeStruct((TOP_K, n), I32),
        compiler_params=_cparams(("arbitrary",)),
        name="slot_pos",
    )(idx, rank, adj_l)


RUN_BITS = tuple(1 << b for b in range(9, -1, -1))


def _pow2_runs(length, fn):
    off = jnp.int32(0)
    for bit in RUN_BITS:
        take = (length & bit) != 0

        @pl.when(take)
        def _(off=off, bit=bit):
            fn(off, bit)

        off = off + jnp.where(take, bit, 0)


def _dispatch_kernel(pad_start_ref, pad_cnt_ref, cstart_ref, clen_ref, cloc_ref, lpos_ref, h3_ref, xs_hbm,
                     cbuf, sems, zsem, *, tm, n_tiles):
    i = pl.program_id(0)
    slot = i % 2
    cap = _comb_cap(tm)
    rows_per_tile = TOP_K * tm

    @pl.when(i == 0)
    def _():
        zrows = RUN_BITS[0]
        cbuf[pl.ds(cap, zrows)] = jnp.zeros((zrows, PACK_TILES, LANES), jnp.uint32)

        def pad_pass(start):
            def per_expert(e, c):
                def one(off, bit):
                    cp = pltpu.make_async_copy(cbuf.at[pl.ds(cap, bit)],
                                               xs_hbm.at[pl.ds(pad_start_ref[e] + off, bit)], zsem)
                    cp.start() if start else cp.wait()

                _pow2_runs(pad_cnt_ref[e], one)
                return c

            lax.fori_loop(0, N_EXPERTS, per_expert, 0)

        pad_pass(True)
        pad_pass(False)

        def tail_pass(start):
            def one(j, c):
                cp = pltpu.make_async_copy(cbuf.at[pl.ds(cap, zrows)],
                                           xs_hbm.at[pl.ds(pad_start_ref[N_EXPERTS] + j * zrows, zrows)], zsem)
                cp.start() if start else cp.wait()
                return c

            lax.fori_loop(0, pad_cnt_ref[N_EXPERTS], one, 0)

        tail_pass(True)
        tail_pass(False)

    def wait_tile(s):
        pltpu.make_async_copy(cbuf.at[pl.ds(s * cap, rows_per_tile)], xs_hbm.at[pl.ds(0, rows_per_tile)],
                              sems.at[s]).wait()

    @pl.when(i >= 2)
    def _():
        wait_tile(slot)

    def tok_group(tt, c):
        for u in range(COMB_UNROLL):
            t = tt * COMB_UNROLL + u
            row = h3_ref[t]
            for k in range(TOP_K):
                cbuf[lax.shift_right_logical(lpos_ref[k * tm + t], 3)] = row
        return c

    lax.fori_loop(0, tm // COMB_UNROLL, tok_group, 0)

    def per_run(e, c):
        src0 = slot * cap + cloc_ref[i * N_EXPERTS + e]
        dst0 = cstart_ref[i * N_EXPERTS + e]

        def one(off, bit):
            pltpu.make_async_copy(cbuf.at[pl.ds(src0 + off, bit)], xs_hbm.at[pl.ds(dst0 + off, bit)],
                                  sems.at[slot]).start()

        _pow2_runs(clen_ref[i * N_EXPERTS + e], one)
        return c

    lax.fori_loop(0, N_EXPERTS, per_run, 0)

    @pl.when(i == n_tiles - 1)
    def _():
        wait_tile(slot)
        if n_tiles >= 2:
            wait_tile(1 - slot)


def _dispatch(pad_start, pad_cnt, cstart, clen, cloc, lpos_tok, h3r, *, n_slots, tm):
    n = h3r.shape[0]
    n_tiles = n // tm
    gs = pltpu.PrefetchScalarGridSpec(
        num_scalar_prefetch=5,
        grid=(n_tiles,),
        in_specs=[pl.BlockSpec((TOP_K * tm,), lambda i, *_: (i,), memory_space=pltpu.SMEM),
                  pl.BlockSpec((tm, PACK_TILES, LANES), lambda i, *_: (i, 0, 0))],
        out_specs=pl.BlockSpec(memory_space=pl.ANY),
        scratch_shapes=[pltpu.VMEM((2 * _comb_cap(tm), PACK_TILES, LANES), jnp.uint32),
                        pltpu.SemaphoreType.DMA((2,)),
                        pltpu.SemaphoreType.DMA(())],
    )
    return pl.pallas_call(
        functools.partial(_dispatch_kernel, tm=tm, n_tiles=n_tiles),
        grid_spec=gs,
        out_shape=jax.ShapeDtypeStruct((n_slots, PACK_TILES, LANES), jnp.uint32),
        compiler_params=_cparams(("arbitrary",)),
        name="dispatch",
    )(pad_start, pad_cnt, cstart, clen, cloc, lpos_tok, h3r)


def _expert_kernel(blk_e_ref, nused_ref, next_e_ref, x_ref, bu_ref, bd_ref, wu_hbm, wd_hbm, y_ref,
                   wu_f32, wd_f32, wu_bf, wd_bf, slot_ref, wsem, *, tb):
    b = pl.program_id(0)

    def weight_copies(e, slot):
        return (pltpu.make_async_copy(wu_hbm.at[e], wu_f32.at[slot], wsem.at[slot, 0]),
                pltpu.make_async_copy(wd_hbm.at[e], wd_f32.at[slot], wsem.at[slot, 1]))

    @pl.when(b >= nused_ref[0])
    def _():
        y_ref[...] = jnp.zeros_like(y_ref)

    @pl.when(b < nused_ref[0])
    def _():
        e = blk_e_ref[b]
        prev = blk_e_ref[jnp.maximum(b - 1, 0)]

        @pl.when(b == 0)
        def _():
            slot_ref[0] = 0
            for cp in weight_copies(e, 0):
                cp.start()

        @pl.when(jnp.logical_and(b > 0, e != prev))
        def _():
            slot_ref[0] = 1 - slot_ref[0]

        @pl.when(jnp.logical_or(b == 0, e != prev))
        def _():
            slot = slot_ref[0]
            for cp in weight_copies(e, slot):
                cp.wait()
            nxt = next_e_ref[e]

            @pl.when(nxt >= 0)
            def _():
                for cp in weight_copies(nxt, 1 - slot):
                    cp.start()

            wu_bf[...] = wu_f32[slot].astype(BF16)
            wd_bf[...] = wd_f32[slot].astype(BF16)

        words = [x_ref[pl.ds(t, tb, stride=PACK_TILES), :] for t in range(PACK_TILES)]
        lo = [pltpu.unpack_elementwise(w, index=0, packed_dtype=BF16, unpacked_dtype=F32) for w in words]
        hi = [pltpu.unpack_elementwise(w, index=1, packed_dtype=BF16, unpacked_dtype=F32) for w in words]
        x = jnp.concatenate(lo + hi, axis=-1).astype(BF16)
        up = jnp.dot(x, wu_bf[...], preferred_element_type=F32) + bu_ref[...]
        glu = jnp.minimum(up[:, :D_FF], SWIGLU_LIMIT)
        lin = jnp.clip(up[:, D_FF:], -SWIGLU_LIMIT, SWIGLU_LIMIT)
        act = (glu * jax.nn.sigmoid(SWIGLU_ALPHA * glu) * (lin + 1.0)).astype(BF16)
        y = jnp.dot(act, wd_bf[...], preferred_element_type=F32) + bd_ref[...]
        for t in range(ROW_TILES):
            y_ref[pl.ds(t, tb, stride=ROW_TILES), :] = y[:, t * LANES:(t + 1) * LANES]


def _expert(blk_e, nused, next_e, xs, w_up, b_up, w_down, b_down, *, tb):
    n_slots = xs.shape[0] // PACK_TILES
    nb = n_slots // tb
    blk = lambda b, be, nu: jnp.minimum(b, nu[0] - 1)
    gs = pltpu.PrefetchScalarGridSpec(
        num_scalar_prefetch=3,
        grid=(nb + 1,),
        in_specs=[pl.BlockSpec((tb * PACK_TILES, LANES), lambda b, be, nu, ne: (blk(b, be, nu), 0)),
                  pl.BlockSpec((None, 1, 2 * D_FF), lambda b, be, nu, ne: (be[blk(b, be, nu)], 0, 0)),
                  pl.BlockSpec((None, 1, D_MODEL), lambda b, be, nu, ne: (be[blk(b, be, nu)], 0, 0)),
                  pl.BlockSpec(memory_space=pl.ANY),
                  pl.BlockSpec(memory_space=pl.ANY)],
        out_specs=pl.BlockSpec((tb * ROW_TILES, LANES), lambda b, be, nu, ne: (b, 0)),
        scratch_shapes=[pltpu.VMEM((2, D_MODEL, 2 * D_FF), F32),
                        pltpu.VMEM((2, D_FF, D_MODEL), F32),
                        pltpu.VMEM((D_MODEL, 2 * D_FF), BF16),
                        pltpu.VMEM((D_FF, D_MODEL), BF16),
                        pltpu.SMEM((1,), I32),
                        pltpu.SemaphoreType.DMA((2, 2))],
    )
    return pl.pallas_call(
        functools.partial(_expert_kernel, tb=tb),
        grid_spec=gs,
        out_shape=jax.ShapeDtypeStruct(((n_slots + tb) * ROW_TILES, LANES), F32),
        compiler_params=_cparams(("arbitrary",)),
        name="expert_ffn",
    )(blk_e, nused, next_e, xs, b_up, b_down, w_up, w_down)


COMB_PIECE = 32
COMB_UNROLL = 16


def _comb_cap(tm):
    return TOP_K * tm + N_EXPERTS * (COMB_PIECE - 1)


def _combine_kernel(cstart_ref, cpieces_ref, cloc_ref, lpos_ref, gate_ref, x2_ref, gfin_ref, y_hbm, out_ref,
                    chunks, rt, sems, *, tm, n_tiles):
    i = pl.program_id(0)
    slot = i % 2
    piece_rows = COMB_PIECE * ROW_TILES

    def piece_copy(tile, e, p, to_slot):
        src = (cstart_ref[tile * N_EXPERTS + e] + p * COMB_PIECE) * ROW_TILES
        dst = (to_slot * _comb_cap(tm) + cloc_ref[tile * N_EXPERTS + e] + p * COMB_PIECE) * ROW_TILES
        return pltpu.make_async_copy(
            y_hbm.at[pl.ds(pl.multiple_of(src, ROW_TILES), piece_rows), :],
            chunks.at[pl.ds(pl.multiple_of(dst, ROW_TILES), piece_rows), :],
            sems.at[to_slot])

    def for_pieces(tile, to_slot, start):
        def per_e(e, c):
            def per_p(p, c2):
                cp = piece_copy(tile, e, p, to_slot)
                if start:
                    cp.start()
                else:
                    cp.wait()
                return c2

            lax.fori_loop(0, cpieces_ref[tile * N_EXPERTS + e], per_p, 0)
            return c

        lax.fori_loop(0, N_EXPERTS, per_e, 0)

    @pl.when(i == 0)
    def _():
        for_pieces(0, 0, True)

    @pl.when(i + 1 < n_tiles)
    def _():
        for_pieces(i + 1, 1 - slot, True)

    for_pieces(i, slot, False)

    def tok_group(tt, c):
        for u in range(COMB_UNROLL):
            t = tt * COMB_UNROLL + u
            acc = None
            for k in range(TOP_K):
                r = pl.multiple_of(lpos_ref[k * tm + t], ROW_TILES)
                term = gate_ref[k * tm + t] * chunks[pl.ds(r, ROW_TILES), :]
                acc = term if acc is None else acc + term
            rt[pl.ds(pl.multiple_of(t * ROW_TILES, ROW_TILES), ROW_TILES), :] = acc
        return c

    lax.fori_loop(0, tm // COMB_UNROLL, tok_group, 0)

    moe = jnp.concatenate([rt[pl.ds(s, tm, stride=ROW_TILES), :] for s in range(ROW_TILES)], axis=-1)
    out_ref[...] = _rms(x2_ref[...] + moe, gfin_ref[...])


def _combine(cstart, cpieces, cloc, lpos, gate, x2, g_final, y, *, tm):
    n = x2.shape[0]
    n_tiles = n // tm
    gs = pltpu.PrefetchScalarGridSpec(
        num_scalar_prefetch=3,
        grid=(n_tiles,),
        in_specs=[pl.BlockSpec((TOP_K * tm,), lambda i, a, b, c: (i,), memory_space=pltpu.SMEM),
                  pl.BlockSpec((TOP_K * tm,), lambda i, a, b, c: (i,), memory_space=pltpu.SMEM),
                  pl.BlockSpec((tm, D_MODEL), lambda i, a, b, c: (i, 0)),
                  pl.BlockSpec((1, D_MODEL), lambda i, a, b, c: (0, 0)),
                  pl.BlockSpec(memory_space=pl.ANY)],
        out_specs=pl.BlockSpec((tm, D_MODEL), lambda i, a, b, c: (i, 0)),
        scratch_shapes=[pltpu.VMEM((2 * _comb_cap(tm) * ROW_TILES, LANES), F32),
                        pltpu.VMEM((tm * ROW_TILES, LANES), F32),
                        pltpu.SemaphoreType.DMA((2,))],
    )
    return pl.pallas_call(
        functools.partial(_combine_kernel, tm=tm, n_tiles=n_tiles),
        grid_spec=gs,
        out_shape=jax.ShapeDtypeStruct((n, D_MODEL), F32),
        compiler_params=_cparams(("arbitrary",)),
        name="combine",
    )(cstart, cpieces, cloc, lpos, gate, x2, g_final, y)


def _head_pad_cols(w, per_head_in, take, place):
    kdim = w.shape[0]
    w3 = w.reshape(kdim, MLA_HEADS, per_head_in)
    out = jnp.zeros((kdim, MLA_HEADS, HEAD_PAD), w.dtype)
    for (t0, t1), p0 in zip(take, place):
        out = out.at[:, :, p0:p0 + (t1 - t0)].set(w3[:, :, t0:t1])
    return out.reshape(kdim, MLA_HEADS * HEAD_PAD)


def kernel(x, mem, positions, g_mix, w_in, g_q, w_uq, g_kv, w_ukv, w_dw, b_dw, g_conv_ln, b_conv_ln, w_out,
           g_xattn, g_mem, w_xq, w_xkv, w_xo, g_ffn, w_router, b_router, w_up, b_up, w_down, b_down, g_final):
    batch, seq, _ = x.shape
    mem_len = mem.shape[1]
    n = batch * seq
    half = MLA_ROPE // 2
    r0 = MLA_NOPE
    assert w_in.shape[0] == 1, "one trunk layer"

    row = lambda v: v.reshape(1, -1).astype(F32)

    wi = w_in[0]
    o1, o2, o3 = Q_LORA, Q_LORA + KV_LORA, Q_LORA + KV_LORA + MLA_ROPE
    kr_blk = jnp.zeros((D_MODEL, HEAD_PAD), F32).at[:, r0:r0 + MLA_ROPE].set(wi[:, o2:o3])
    kr_swp = (jnp.zeros((D_MODEL, HEAD_PAD), F32)
              .at[:, r0:r0 + half].set(wi[:, o2 + half:o3])
              .at[:, r0 + half:r0 + MLA_ROPE].set(wi[:, o2:o2 + half]))
    w_in_p = jnp.concatenate([wi[:, :o2], kr_blk, kr_swp, wi[:, o3:]], axis=1).astype(BF16)
    per_q = MLA_NOPE + MLA_ROPE
    w_uq_p = _head_pad_cols(w_uq[0], per_q, [(0, per_q)], [0]).astype(BF16)
    w_uq_s = _head_pad_cols(w_uq[0], per_q, [(MLA_NOPE + half, per_q), (MLA_NOPE, MLA_NOPE + half)],
                            [r0, r0 + half]).astype(BF16)
    per_kv = MLA_NOPE + MLA_V
    w_uk_k = _head_pad_cols(w_ukv[0], per_kv, [(0, MLA_NOPE)], [0]).astype(BF16)
    w_uk_v = (w_ukv[0].reshape(KV_LORA, MLA_HEADS, per_kv)[:, :, MLA_NOPE:]
              .reshape(KV_LORA, MLA_WIDTH).astype(BF16))
    inv = ROPE_THETA ** (-jnp.arange(0, MLA_ROPE, 2, dtype=F32) / MLA_ROPE)
    inv_c = (jnp.zeros((LANES, 1), F32).at[r0:r0 + half, 0].set(inv).at[r0 + half:r0 + MLA_ROPE, 0].set(inv))
    sgn_c = (jnp.zeros((LANES, 1), F32).at[r0:r0 + half, 0].set(-1.0).at[r0 + half:r0 + MLA_ROPE, 0].set(1.0))

    x2d = x.reshape(n, D_MODEL)
    pos_row = positions.reshape(1, n).astype(I32)

    q_t, k, v_t, y_conv = _mix_in(x2d, pos_row, inv_c, sgn_c, row(g_mix[0]), w_in_p, row(g_q[0]),
                                  jnp.transpose(w_uq_p), jnp.transpose(w_uq_s), row(g_kv[0]), w_uk_k,
                                  jnp.transpose(w_uk_v), w_dw[0].astype(F32), row(b_dw[0]),
                                  row(g_conv_ln[0]), row(b_conv_ln[0]), seq=seq)
    y_mla = _attn(q_t, k, v_t, batch=batch, seq=seq)
    kv = _mem_kv(mem.reshape(batch * mem_len, D_MODEL), row(g_mem[0]), w_xkv[0].astype(BF16),
                 batch=batch, mem_len=mem_len)

    tm_mid = min(TM_MID, seq)
    wr = jnp.zeros((D_MODEL, LANES), F32).at[:, :N_EXPERTS].set(w_router[0])
    wr_h = wr.astype(BF16)
    wr_l = (wr - wr_h.astype(F32)).astype(BF16)
    b_r = b_router[0].reshape(N_EXPERTS, 1).astype(F32)
    tri = (lax.broadcasted_iota(I32, (tm_mid, tm_mid), 0)
           < lax.broadcasted_iota(I32, (tm_mid, tm_mid), 1)).astype(BF16)
    wo = w_out[0].astype(BF16)
    x2, h3r, idx, gate, rank, cnt, runs_l = _mid(
        x2d, y_mla, y_conv, kv, wo[:MLA_WIDTH], wo[MLA_WIDTH:], row(g_xattn[0]), w_xq[0].astype(BF16),
        w_xo[0].astype(BF16), row(g_ffn[0]), wr_h, wr_l, b_r, tri, seq=seq, mem_len=mem_len, tm=tm_mid)

    tb = TB_EXPERT
    counts = cnt[:, 0]
    padded = (counts + tb - 1) // tb * tb
    pend = jnp.cumsum(padded)
    pstart = pend - padded
    n_slots = n * TOP_K + N_EXPERTS * tb
    nb = n_slots // tb
    blk_first = jnp.arange(nb, dtype=I32) * tb
    blk_e = jnp.minimum(jnp.sum((pend[None, :] <= blk_first[:, None]).astype(I32), axis=1),
                        N_EXPERTS - 1).astype(I32)
    nused = (pend[-1:] // tb).astype(I32)
    eids = jnp.arange(N_EXPERTS, dtype=I32)
    later_used = jnp.logical_and(eids[None, :] > eids[:, None], (padded > 0)[None, :])
    next_e = jnp.min(jnp.where(later_used, eids[None, :], N_EXPERTS), axis=1)
    next_e = jnp.where(next_e == N_EXPERTS, -1, next_e).astype(I32)

    runs = runs_l[:, :, 0]
    run_len = jnp.concatenate([runs[1:], counts[None, :]], axis=0) - runs
    cstart = pstart[None, :] + runs
    cpieces = (run_len + COMB_PIECE - 1) // COMB_PIECE
    cloc = jnp.cumsum(cpieces * COMB_PIECE, axis=1) - cpieces * COMB_PIECE
    adj_l = jnp.broadcast_to((cloc - runs)[:, :, None], runs_l.shape).astype(I32)

    tm_comb = min(COMB_TM, tm_mid)
    lpos = _pos(idx, rank, adj_l, tm=tm_comb)
    tiled = lambda a: a.reshape(TOP_K, n // tm_comb, tm_comb).transpose(1, 0, 2).reshape(-1)
    lpos_tok = tiled(lpos)
    flat = lambda a: a.reshape(-1).astype(I32)
    tail_blocks = (n_slots - pend[-1:]) // RUN_BITS[0]
    fill_start = jnp.concatenate([pstart + counts, pend[-1:]]).astype(I32)
    fill_count = jnp.concatenate([padded - counts, tail_blocks]).astype(I32)
    xs = _dispatch(fill_start, fill_count, flat(cstart), flat(run_len),
                   flat(cloc), lpos_tok, h3r.reshape(n, PACK_TILES, LANES), n_slots=n_slots, tm=tm_comb)
    y = _expert(blk_e, nused, next_e, xs.reshape(n_slots * PACK_TILES, LANES), w_up[0],
                b_up[0].reshape(N_EXPERTS, 1, 2 * D_FF), w_down[0], b_down[0].reshape(N_EXPERTS, 1, D_MODEL), tb=tb)
    out = _combine(flat(cstart), flat(cpieces), flat(cloc), lpos_tok, tiled(gate),
                   x2, row(g_final), y, tm=tm_comb)
    return out.reshape(batch, seq, D_MODEL)
```

```python
import functools

import jax
import jax.numpy as jnp
from jax import lax
from jax.experimental import pallas as pl
from jax.experimental.pallas import tpu as pltpu

F32 = jnp.float32
BF16 = jnp.bfloat16
I32 = jnp.int32

D_MODEL = 1024
MLA_HEADS = 8
MLA_NOPE = 64
MLA_ROPE = 32
MLA_V = 64
MLA_WIDTH = MLA_HEADS * MLA_V
Q_LORA = 256
KV_LORA = 128
ROPE_THETA = 10000.0
CHUNK = 64
CONV_CH = 512
CONV_WIDTH = 31
X_HEADS = 4
X_HEAD_DIM = D_MODEL // X_HEADS
N_EXPERTS = 32
TOP_K = 4
D_FF = D_MODEL
SWIGLU_LIMIT = 7.0
SWIGLU_ALPHA = 1.702
NORM_EPS = 1e-5

LANES = 128
SUBLANES = 8
ROW_TILES = D_MODEL // LANES
PACK_TILES = ROW_TILES // 2
VMEM_LIMIT_BYTES = 56 * 1024 * 1024

HEAD_PAD = LANES
Q_PRESCALE = (MLA_NOPE + MLA_ROPE) ** -0.5 * 1.4426950408889634
CONV_HALO = 32

TM_MIX = 512
TQ_ATTN = 1024
TM_MID = 512
TB_EXPERT = 512
MID_SPLIT = 2
COMB_TM = 512


def _rms(x, g):
    return x * lax.rsqrt(jnp.mean(x * x, axis=-1, keepdims=True) + NORM_EPS) * g


def _cparams(sem):
    return pltpu.CompilerParams(dimension_semantics=sem, vmem_limit_bytes=VMEM_LIMIT_BYTES)


def _mix_in_kernel(x_ref, pos_ref, inv_ref, sgn_ref, gmix_ref, win_ref, gq_ref, wuq_ref, wuqs_ref,
                   gkv_ref, wukk_ref, wukv_ref, wdw_ref, bdw_ref, gln_ref, bln_ref,
                   q_ref, k_ref, v_ref, yc_ref, zbuf, zsh, *, tiles_per_batch, tm):
    i = pl.program_id(0)
    x = x_ref[...]
    h = _rms(x, gmix_ref[...]).astype(BF16)
    u = jnp.dot(h, win_ref[...], preferred_element_type=F32)
    c0 = Q_LORA + KV_LORA
    c1 = c0 + 2 * HEAD_PAD
    cq = u[:, 0:Q_LORA]
    ckv = u[:, Q_LORA:c0]
    kr = u[:, c0:c0 + HEAD_PAD]
    krs = u[:, c0 + HEAD_PAD:c1]
    a = u[:, c1:c1 + CONV_CH]
    gate = u[:, c1 + CONV_CH:c1 + 2 * CONV_CH]

    r0, r1 = MLA_NOPE, MLA_NOPE + MLA_ROPE
    ang_r = inv_ref[r0:r1, :] * pos_ref[...].astype(F32)
    cos_t = jnp.concatenate([jnp.ones((r0, tm), F32), jnp.cos(ang_r), jnp.ones((HEAD_PAD - r1, tm), F32)], axis=0)
    sin_t = jnp.concatenate([jnp.zeros((r0, tm), F32), jnp.sin(ang_r) * sgn_ref[r0:r1, :],
                             jnp.zeros((HEAD_PAD - r1, tm), F32)], axis=0)
    cosb = jnp.transpose(cos_t)
    sinb = jnp.transpose(sin_t)
    cos_q = cos_t * Q_PRESCALE
    sin_q = sin_t * Q_PRESCALE

    nt = (((1,), (1,)), ((), ()))
    cqn = _rms(cq, gq_ref[...]).astype(BF16)
    qm = lax.dot_general(wuq_ref[...], cqn, nt, preferred_element_type=F32)
    qs = lax.dot_general(wuqs_ref[...], cqn, nt, preferred_element_type=F32)
    for hd in range(MLA_HEADS):
        sl = slice(hd * HEAD_PAD, (hd + 1) * HEAD_PAD)
        q_ref[sl, :] = (qm[sl, :] * cos_q + qs[sl, :] * sin_q).astype(BF16)

    ckvn = _rms(ckv, gkv_ref[...]).astype(BF16)
    kk = jnp.dot(ckvn, wukk_ref[...], preferred_element_type=F32)
    v_ref[...] = lax.dot_general(wukv_ref[...], ckvn, nt, preferred_element_type=F32).astype(BF16)
    krot = kr * cosb + krs * sinb
    for hd in range(MLA_HEADS):
        sl = slice(hd * HEAD_PAD, (hd + 1) * HEAD_PAD)
        k_ref[:, sl] = (kk[:, sl] + krot).astype(BF16)

    z = a * jax.nn.sigmoid(gate)

    @pl.when(i % tiles_per_batch == 0)
    def _():
        zbuf[0:CONV_HALO, :] = jnp.zeros((CONV_HALO, CONV_CH), F32)

    zbuf[CONV_HALO:CONV_HALO + tm, :] = z
    off = CONV_HALO - (CONV_WIDTH - 1)
    rows = 32
    span = tm + CONV_HALO - SUBLANES
    step = 64
    for r in range(1, SUBLANES):
        for c0 in range(0, span, step):
            cl = min(step, span - c0)
            zsh[r - 1, c0:c0 + cl, :] = zbuf[c0 + r:c0 + r + cl, :]
    for r0 in range(0, tm, rows):
        acc = jnp.zeros((rows, CONV_CH), F32) + bdw_ref[...]
        for j in range(CONV_WIDTH):
            a8, ph = divmod(j + off, SUBLANES)
            lo_row = r0 + a8 * SUBLANES
            tap = zbuf[lo_row:lo_row + rows, :] if ph == 0 else zsh[ph - 1, lo_row:lo_row + rows, :]
            acc = acc + wdw_ref[j:j + 1, :] * tap
        mu = jnp.mean(acc, axis=-1, keepdims=True)
        cen = acc - mu
        var = jnp.mean(cen * cen, axis=-1, keepdims=True)
        y = cen * lax.rsqrt(var + NORM_EPS) * gln_ref[...] + bln_ref[...]
        yc_ref[r0:r0 + rows, :] = (y * jax.nn.sigmoid(y)).astype(BF16)
    zbuf[0:CONV_HALO, :] = zbuf[tm:tm + CONV_HALO, :]


def _mix_in(x2d, pos_row, inv_c, sgn_c, g_mix, w_in_p, g_q, w_uq_t, w_uq_st, g_kv, w_uk_k, w_uk_vt,
            w_dw, b_dw, g_ln, b_ln, *, seq):
    n = x2d.shape[0]
    tm = min(TM_MIX, seq)
    full = lambda a: pl.BlockSpec(a.shape, lambda i: (0,) * a.ndim)
    consts = [inv_c, sgn_c, g_mix, w_in_p, g_q, w_uq_t, w_uq_st, g_kv, w_uk_k, w_uk_vt, w_dw, b_dw, g_ln, b_ln]
    return pl.pallas_call(
        functools.partial(_mix_in_kernel, tiles_per_batch=seq // tm, tm=tm),
        grid=(n // tm,),
        in_specs=[pl.BlockSpec((tm, D_MODEL), lambda i: (i, 0)),
                  pl.BlockSpec((1, tm), lambda i: (0, i))] + [full(a) for a in consts],
        out_specs=[pl.BlockSpec((MLA_HEADS * HEAD_PAD, tm), lambda i: (0, i)),
                   pl.BlockSpec((tm, MLA_HEADS * HEAD_PAD), lambda i: (i, 0)),
                   pl.BlockSpec((MLA_WIDTH, tm), lambda i: (0, i)),
                   pl.BlockSpec((tm, CONV_CH), lambda i: (i, 0))],
        out_shape=[jax.ShapeDtypeStruct((MLA_HEADS * HEAD_PAD, n), BF16),
                   jax.ShapeDtypeStruct((n, MLA_HEADS * HEAD_PAD), BF16),
                   jax.ShapeDtypeStruct((MLA_WIDTH, n), BF16),
                   jax.ShapeDtypeStruct((n, CONV_CH), BF16)],
        scratch_shapes=[pltpu.VMEM((CONV_HALO + tm + SUBLANES, CONV_CH), F32),
                        pltpu.VMEM((SUBLANES - 1, CONV_HALO + tm, CONV_CH), F32)],
        compiler_params=_cparams(("arbitrary",)),
        name="mix_in",
    )(x2d, pos_row, *consts)


ATTN_GROUP = 4


TK_ATTN = 1024
DENOM_ROWS = 16


def _attn_kernel(q_ref, k_ref, v_ref, o_ref, *s_scr, tq):
    i = pl.program_id(1)
    tk = min(TK_ATTN, tq)
    per_q = tq // tk
    key_c = lax.broadcasted_iota(I32, (tk, tq), 0) // CHUNK
    qry_c = lax.broadcasted_iota(I32, (tk, tq), 1) // CHUNK

    for g in range(MLA_HEADS // ATTN_GROUP):
        heads = list(range(g * ATTN_GROUP, (g + 1) * ATTN_GROUP))
        qts = [q_ref[hd * HEAD_PAD:(hd + 1) * HEAD_PAD, :] for hd in heads]

        def step(j, carry, diag):
            start = pl.multiple_of(j * tk, tk)
            new = []
            for n_, hd in enumerate(heads):
                m, acc = carry[n_]
                kj = k_ref[pl.ds(start, tk), hd * HEAD_PAD:(hd + 1) * HEAD_PAD]
                vj = jnp.concatenate([v_ref[hd * MLA_V:(hd + 1) * MLA_V, pl.ds(start, tk)],
                                      jnp.ones((DENOM_ROWS, tk), BF16)], axis=0)
                s = jnp.dot(kj, qts[n_], preferred_element_type=F32)
                if diag is not None:
                    s = jnp.where(key_c + diag * (tk // CHUNK) <= qry_c, s, -jnp.inf)
                m_new = jnp.maximum(m, jnp.max(s, axis=0, keepdims=True))
                s_scr[n_][...] = s
                s = s_scr[n_][...]
                alpha = jnp.exp2(m - m_new)
                p = jnp.exp2(s - m_new)
                acc = alpha * acc + jnp.dot(vj, p.astype(BF16), preferred_element_type=F32)
                new.append((m_new, acc))
            return tuple(new)

        init = tuple((jnp.full((1, tq), -jnp.inf, F32), jnp.zeros((MLA_V + DENOM_ROWS, tq), F32)) for _ in heads)
        carry = lax.fori_loop(0, i * per_q, functools.partial(step, diag=None), init)
        for d in range(per_q):
            carry = step(i * per_q + d, carry, d)
        for n_, hd in enumerate(heads):
            _, acc = carry[n_]
            out = acc[0:MLA_V, :] / acc[MLA_V:MLA_V + 1, :]
            o_ref[:, hd * MLA_V:(hd + 1) * MLA_V] = jnp.transpose(out).astype(BF16)


def _attn(q_t, k, v_t, *, batch, seq):
    tq = min(TQ_ATTN, seq)
    nq = seq // tq
    return pl.pallas_call(
        functools.partial(_attn_kernel, tq=tq),
        grid=(batch, nq),
        in_specs=[pl.BlockSpec((MLA_HEADS * HEAD_PAD, tq), lambda b, i: (0, b * nq + i)),
                  pl.BlockSpec((seq, MLA_HEADS * HEAD_PAD), lambda b, i: (b, 0)),
                  pl.BlockSpec((MLA_WIDTH, seq), lambda b, i: (0, b))],
        out_specs=pl.BlockSpec((tq, MLA_WIDTH), lambda b, i: (b * nq + i, 0)),
        out_shape=jax.ShapeDtypeStruct((batch * seq, MLA_WIDTH), BF16),
        scratch_shapes=[pltpu.VMEM((min(TK_ATTN, tq), tq), F32) for _ in range(ATTN_GROUP)],
        compiler_params=_cparams(("arbitrary", "arbitrary")),
        name="mla_attn",
    )(q_t, k, v_t)


def _mem_kv_kernel(mem_ref, g_ref, w_ref, kv_ref):
    mn = _rms(mem_ref[...], g_ref[...]).astype(BF16)
    kv_ref[...] = jnp.dot(mn, w_ref[...], preferred_element_type=F32).astype(BF16)


def _mem_kv(mem2d, g_mem, w_xkv, *, batch, mem_len):
    return pl.pallas_call(
        _mem_kv_kernel,
        grid=(batch,),
        in_specs=[pl.BlockSpec((mem_len, D_MODEL), lambda b: (b, 0)),
                  pl.BlockSpec((1, D_MODEL), lambda b: (0, 0)),
                  pl.BlockSpec((D_MODEL, 2 * D_MODEL), lambda b: (0, 0))],
        out_specs=pl.BlockSpec((mem_len, 2 * D_MODEL), lambda b: (b, 0)),
        out_shape=jax.ShapeDtypeStruct((batch * mem_len, 2 * D_MODEL), BF16),
        compiler_params=_cparams(("arbitrary",)),
        name="mem_kv",
    )(mem2d, g_mem, w_xkv)


def _mid_kernel(x_ref, ya_ref, yc_ref, kv_ref, woa_ref, wob_ref, gx_ref, wxq_ref, wxo_ref, gf_ref,
                wrh_ref, wrl_ref, br_ref, tri_ref,
                x2_ref, h3_ref, idx_ref, gate_ref, rank_ref, cnt_ref, runs_ref, run_ref, *, tm):
    i = pl.program_id(0)

    @pl.when(i == 0)
    def _():
        run_ref[...] = jnp.zeros_like(run_ref)

    nt = (((1,), (1,)), ((), ()))
    half_d = D_MODEL // 2
    rows = tm // MID_SPLIT
    lgs = []
    for part in range(MID_SPLIT):
        rs = slice(part * rows, (part + 1) * rows)
        x1 = (x_ref[rs, :]
              + jnp.dot(ya_ref[rs, :], woa_ref[...], preferred_element_type=F32)
              + jnp.dot(yc_ref[rs, :], wob_ref[...], preferred_element_type=F32))
        h = _rms(x1, gx_ref[...]).astype(BF16)
        q = jnp.dot(h, wxq_ref[...], preferred_element_type=F32).astype(BF16)
        os = []
        for hd in range(X_HEADS):
            sl = slice(hd * X_HEAD_DIM, (hd + 1) * X_HEAD_DIM)
            vsl = slice(D_MODEL + hd * X_HEAD_DIM, D_MODEL + (hd + 1) * X_HEAD_DIM)
            s = lax.dot_general(q[:, sl], kv_ref[:, sl], nt, preferred_element_type=F32) * (X_HEAD_DIM ** -0.5)
            e = jnp.exp(s - jnp.max(s, axis=-1, keepdims=True))
            p = (e / jnp.sum(e, axis=-1, keepdims=True)).astype(BF16)
            os.append(jnp.dot(p, kv_ref[:, vsl], preferred_element_type=F32).astype(BF16))
        o = jnp.concatenate(os, axis=-1)
        x2 = x1 + jnp.dot(o, wxo_ref[...], preferred_element_type=F32)
        x2_ref[rs, :] = x2
        h3 = _rms(x2, gf_ref[...])
        hh = h3.astype(BF16)
        hf = hh.astype(F32)
        word = pltpu.pack_elementwise([h3[:, :half_d], h3[:, half_d:]], packed_dtype=BF16)
        for t in range(PACK_TILES):
            h3_ref[pl.ds(part * rows * PACK_TILES + t, rows, stride=PACK_TILES), :] = word[:, t * LANES:(t + 1) * LANES]

        hl = (h3 - hf).astype(BF16)
        lgs.append(jnp.dot(hh, wrh_ref[...], preferred_element_type=F32)
                   + (jnp.dot(hh, wrl_ref[...], preferred_element_type=F32)
                      + jnp.dot(hl, wrh_ref[...], preferred_element_type=F32)))
    lg = jnp.concatenate(lgs, axis=0)
    lgt = jnp.transpose(lg)[0:N_EXPERTS, :] + br_ref[...]
    eid = lax.broadcasted_iota(I32, (N_EXPERTS, tm), 0)
    vals, hots = [], []
    cur = lgt
    for k in range(TOP_K):
        mx = jnp.max(cur, axis=0, keepdims=True)
        ik = jnp.min(jnp.where(cur == mx, eid, N_EXPERTS), axis=0, keepdims=True)
        hot = eid == ik
        cur = jnp.where(hot, -jnp.inf, cur)
        idx_ref[k:k + 1, :] = ik
        vals.append(mx)
        hots.append(hot)
    es = [jnp.exp(v - vals[0]) for v in vals]
    den = es[0] + es[1] + es[2] + es[3]
    for k in range(TOP_K):
        gate_ref[k:k + 1, :] = es[k] / den
    cnt = jnp.zeros((N_EXPERTS, tm), F32)
    for k in range(TOP_K):
        cnt = cnt + hots[k].astype(F32)
    run = run_ref[:, 0:1]
    ctm = min(COMB_TM, tm)
    for j in range(tm // ctm):
        before = run if j == 0 else run + jnp.sum(cnt[:, :j * ctm], axis=1, keepdims=True)
        runs_ref[j] = jnp.broadcast_to(before, (N_EXPERTS, LANES)).astype(I32)
    tot = jnp.dot(cnt.astype(BF16), tri_ref[...], preferred_element_type=F32) + run
    for k in range(TOP_K):
        rank_ref[k:k + 1, :] = jnp.sum(jnp.where(hots[k], tot, 0.0), axis=0, keepdims=True).astype(I32)
    new_run = run + jnp.sum(cnt, axis=1, keepdims=True)
    run_ref[...] = jnp.broadcast_to(new_run, run_ref.shape)
    cnt_ref[...] = jnp.broadcast_to(new_run, cnt_ref.shape).astype(I32)


def _mid(x2d, ya, yc, kv, w_oa, w_ob, g_x, w_xq, w_xo, g_f, wr_h, wr_l, b_r, tri, *, seq, mem_len, tm):
    n = x2d.shape[0]
    tpb = seq // tm
    full = lambda a: pl.BlockSpec(a.shape, lambda i: (0,) * a.ndim)
    consts_a = [w_oa, w_ob, g_x, w_xq, w_xo, g_f, wr_h, wr_l, b_r, tri]
    return pl.pallas_call(
        functools.partial(_mid_kernel, tm=tm),
        grid=(n // tm,),
        in_specs=[pl.BlockSpec((tm, D_MODEL), lambda i: (i, 0)),
                  pl.BlockSpec((tm, MLA_WIDTH), lambda i: (i, 0)),
                  pl.BlockSpec((tm, CONV_CH), lambda i: (i, 0)),
                  pl.BlockSpec((mem_len, 2 * D_MODEL), lambda i: (i // tpb, 0))] + [full(a) for a in consts_a],
        out_specs=[pl.BlockSpec((tm, D_MODEL), lambda i: (i, 0)),
                   pl.BlockSpec((tm * PACK_TILES, LANES), lambda i: (i, 0)),
                   pl.BlockSpec((TOP_K, tm), lambda i: (0, i)),
                   pl.BlockSpec((TOP_K, tm), lambda i: (0, i)),
                   pl.BlockSpec((TOP_K, tm), lambda i: (0, i)),
                   pl.BlockSpec((N_EXPERTS, LANES), lambda i: (0, 0)),
                   pl.BlockSpec((tm // min(COMB_TM, tm), N_EXPERTS, LANES), lambda i: (i, 0, 0))],
        out_shape=[jax.ShapeDtypeStruct((n, D_MODEL), F32),
                   jax.ShapeDtypeStruct((n * PACK_TILES, LANES), jnp.uint32),
                   jax.ShapeDtypeStruct((TOP_K, n), I32),
                   jax.ShapeDtypeStruct((TOP_K, n), F32),
                   jax.ShapeDtypeStruct((TOP_K, n), I32),
                   jax.ShapeDtypeStruct((N_EXPERTS, LANES), I32),
                   jax.ShapeDtypeStruct((n // min(COMB_TM, tm), N_EXPERTS, LANES), I32)],
        scratch_shapes=[pltpu.VMEM((N_EXPERTS, LANES), F32)],
        compiler_params=_cparams(("arbitrary",)),
        name="mid",
    )(x2d, ya, yc, kv, *consts_a)


def _pos_kernel(idx_ref, rank_ref, adj_ref, lpos_ref, *, tm, group):
    i = pl.program_id(0)
    eid = lax.broadcasted_iota(I32, (N_EXPERTS, tm), 0)
    for j in range(group):
        half = ((i * group + j) % 2) * _comb_cap(tm)
        cols = slice(j * tm, (j + 1) * tm)
        adj = adj_ref[j][:, 0:1] + half
        for k in range(TOP_K):
            hot = eid == idx_ref[k:k + 1, cols]
            lpos_ref[k:k + 1, cols] = (jnp.sum(jnp.where(hot, adj, 0), axis=0, keepdims=True)
                                       + rank_ref[k:k + 1, cols]) * ROW_TILES


def _pos(idx, rank, adj_l, *, tm):
    n = idx.shape[1]
    n_tiles = n // tm
    group = 4 if n_tiles % 4 == 0 else 1
    return pl.pallas_call(
        functools.partial(_pos_kernel, tm=tm, group=group),
        grid=(n_tiles // group,),
        in_specs=[pl.BlockSpec((TOP_K, tm * group), lambda i: (0, i)),
                  pl.BlockSpec((TOP_K, tm * group), lambda i: (0, i)),
                  pl.BlockSpec((group, N_EXPERTS, LANES), lambda i: (i, 0, 0))],
        out_specs=pl.BlockSpec((TOP_K, tm * group), lambda i: (0, i)),
        out_shape=jax.ShapeDtypeStruct((TOP_K, n), I32),
        compiler_params=_cparams(("arbitrary",)),
        name="slot_pos",
    )(idx, rank, adj_l)


RUN_BITS = tuple(1 << b for b in range(9, -1, -1))


def _pow2_runs(length, fn):
    off = jnp.int32(0)
    for bit in RUN_BITS:
        take = (length & bit) != 0

        @pl.when(take)
        def _(off=off, bit=bit):
            fn(off, bit)

        off = off + jnp.where(take, bit, 0)


def _dispatch_kernel(pad_start_ref, pad_cnt_ref, cstart_ref, clen_ref, cloc_ref, lpos_ref, h3_ref, xs_hbm,
                     cbuf, sems, zsem, *, tm, n_tiles):
    i = pl.program_id(0)
    slot = i % 2
    cap = _comb_cap(tm)
    rows_per_tile = TOP_K * tm

    @pl.when(i == 0)
    def _():
        zrows = RUN_BITS[0]
        cbuf[pl.ds(cap, zrows)] = jnp.zeros((zrows, PACK_TILES, LANES), jnp.uint32)

        def pad_pass(start):
            def per_expert(e, c):
                def one(off, bit):
                    cp = pltpu.make_async_copy(cbuf.at[pl.ds(cap, bit)],
                                               xs_hbm.at[pl.ds(pad_start_ref[e] + off, bit)], zsem)
                    cp.start() if start else cp.wait()

                _pow2_runs(pad_cnt_ref[e], one)
                return c

            lax.fori_loop(0, N_EXPERTS, per_expert, 0)

        pad_pass(True)
        pad_pass(False)

        def tail_pass(start):
            def one(j, c):
                cp = pltpu.make_async_copy(cbuf.at[pl.ds(cap, zrows)],
                                           xs_hbm.at[pl.ds(pad_start_ref[N_EXPERTS] + j * zrows, zrows)], zsem)
                cp.start() if start else cp.wait()
                return c

            lax.fori_loop(0, pad_cnt_ref[N_EXPERTS], one, 0)

        tail_pass(True)
        tail_pass(False)

    def wait_tile(s):
        pltpu.make_async_copy(cbuf.at[pl.ds(s * cap, rows_per_tile)], xs_hbm.at[pl.ds(0, rows_per_tile)],
                              sems.at[s]).wait()

    @pl.when(i >= 2)
    def _():
        wait_tile(slot)

    def tok_group(tt, c):
        for u in range(COMB_UNROLL):
            t = tt * COMB_UNROLL + u
            row = h3_ref[t]
            for k in range(TOP_K):
                cbuf[lax.shift_right_logical(lpos_ref[k * tm + t], 3)] = row
        return c

    lax.fori_loop(0, tm // COMB_UNROLL, tok_group, 0)

    def per_run(e, c):
        src0 = slot * cap + cloc_ref[i * N_EXPERTS + e]
        dst0 = cstart_ref[i * N_EXPERTS + e]

        def one(off, bit):
            pltpu.make_async_copy(cbuf.at[pl.ds(src0 + off, bit)], xs_hbm.at[pl.ds(dst0 + off, bit)],
                                  sems.at[slot]).start()

        _pow2_runs(clen_ref[i * N_EXPERTS + e], one)
        return c

    lax.fori_loop(0, N_EXPERTS, per_run, 0)

    @pl.when(i == n_tiles - 1)
    def _():
        wait_tile(slot)
        if n_tiles >= 2:
            wait_tile(1 - slot)


def _dispatch(pad_start, pad_cnt, cstart, clen, cloc, lpos_tok, h3r, *, n_slots, tm):
    n = h3r.shape[0]
    n_tiles = n // tm
    gs = pltpu.PrefetchScalarGridSpec(
        num_scalar_prefetch=5,
        grid=(n_tiles,),
        in_specs=[pl.BlockSpec((TOP_K * tm,), lambda i, *_: (i,), memory_space=pltpu.SMEM),
                  pl.BlockSpec((tm, PACK_TILES, LANES), lambda i, *_: (i, 0, 0))],
        out_specs=pl.BlockSpec(memory_space=pl.ANY),
        scratch_shapes=[pltpu.VMEM((2 * _comb_cap(tm), PACK_TILES, LANES), jnp.uint32),
                        pltpu.SemaphoreType.DMA((2,)),
                        pltpu.SemaphoreType.DMA(())],
    )
    return pl.pallas_call(
        functools.partial(_dispatch_kernel, tm=tm, n_tiles=n_tiles),
        grid_spec=gs,
        out_shape=jax.ShapeDtypeStruct((n_slots, PACK_TILES, LANES), jnp.uint32),
        compiler_params=_cparams(("arbitrary",)),
        name="dispatch",
    )(pad_start, pad_cnt, cstart, clen, cloc, lpos_tok, h3r)


def _expert_kernel(blk_e_ref, nused_ref, next_e_ref, x_ref, bu_ref, bd_ref, wu_hbm, wd_hbm, y_ref,
                   wu_f32, wd_f32, wu_bf, wd_bf, slot_ref, wsem, *, tb):
    b = pl.program_id(0)

    def weight_copies(e, slot):
        return (pltpu.make_async_copy(wu_hbm.at[e], wu_f32.at[slot], wsem.at[slot, 0]),
                pltpu.make_async_copy(wd_hbm.at[e], wd_f32.at[slot], wsem.at[slot, 1]))

    @pl.when(b >= nused_ref[0])
    def _():
        y_ref[...] = jnp.zeros_like(y_ref)

    @pl.when(b < nused_ref[0])
    def _():
        e = blk_e_ref[b]
        prev = blk_e_ref[jnp.maximum(b - 1, 0)]

        @pl.when(b == 0)
        def _():
            slot_ref[0] = 0
            for cp in weight_copies(e, 0):
                cp.start()

        @pl.when(jnp.logical_and(b > 0, e != prev))
        def _():
            slot_ref[0] = 1 - slot_ref[0]

        @pl.when(jnp.logical_or(b == 0, e != prev))
        def _():
            slot = slot_ref[0]
            for cp in weight_copies(e, slot):
                cp.wait()
            nxt = next_e_ref[e]

            @pl.when(nxt >= 0)
            def _():
                for cp in weight_copies(nxt, 1 - slot):
                    cp.start()

            wu_bf[...] = wu_f32[slot].astype(BF16)
            wd_bf[...] = wd_f32[slot].astype(BF16)

        words = [x_ref[pl.ds(t, tb, stride=PACK_TILES), :] for t in range(PACK_TILES)]
        lo = [pltpu.unpack_elementwise(w, index=0, packed_dtype=BF16, unpacked_dtype=F32) for w in words]
        hi = [pltpu.unpack_elementwise(w, index=1, packed_dtype=BF16, unpacked_dtype=F32) for w in words]
        x = jnp.concatenate(lo + hi, axis=-1).astype(BF16)
        up = jnp.dot(x, wu_bf[...], preferred_element_type=F32) + bu_ref[...]
        glu = jnp.minimum(up[:, :D_FF], SWIGLU_LIMIT)
        lin = jnp.clip(up[:, D_FF:], -SWIGLU_LIMIT, SWIGLU_LIMIT)
        act = (glu * jax.nn.sigmoid(SWIGLU_ALPHA * glu) * (lin + 1.0)).astype(BF16)
        y = jnp.dot(act, wd_bf[...], preferred_element_type=F32) + bd_ref[...]
        for t in range(ROW_TILES):
            y_ref[pl.ds(t, tb, stride=ROW_TILES), :] = y[:, t * LANES:(t + 1) * LANES]


def _expert(blk_e, nused, next_e, xs, w_up, b_up, w_down, b_down, *, tb):
    n_slots = xs.shape[0] // PACK_TILES
    nb = n_slots // tb
    blk = lambda b, be, nu: jnp.minimum(b, nu[0] - 1)
    gs = pltpu.PrefetchScalarGridSpec(
        num_scalar_prefetch=3,
        grid=(nb + 1,),
        in_specs=[pl.BlockSpec((tb * PACK_TILES, LANES), lambda b, be, nu, ne: (blk(b, be, nu), 0)),
                  pl.BlockSpec((None, 1, 2 * D_FF), lambda b, be, nu, ne: (be[blk(b, be, nu)], 0, 0)),
                  pl.BlockSpec((None, 1, D_MODEL), lambda b, be, nu, ne: (be[blk(b, be, nu)], 0, 0)),
                  pl.BlockSpec(memory_space=pl.ANY),
                  pl.BlockSpec(memory_space=pl.ANY)],
        out_specs=pl.BlockSpec((tb * ROW_TILES, LANES), lambda b, be, nu, ne: (b, 0)),
        scratch_shapes=[pltpu.VMEM((2, D_MODEL, 2 * D_FF), F32),
                        pltpu.VMEM((2, D_FF, D_MODEL), F32),
                        pltpu.VMEM((D_MODEL, 2 * D_FF), BF16),
                        pltpu.VMEM((D_FF, D_MODEL), BF16),
                        pltpu.SMEM((1,), I32),
                        pltpu.SemaphoreType.DMA((2, 2))],
    )
    return pl.pallas_call(
        functools.partial(_expert_kernel, tb=tb),
        grid_spec=gs,
        out_shape=jax.ShapeDtypeStruct(((n_slots + tb) * ROW_TILES, LANES), F32),
        compiler_params=_cparams(("arbitrary",)),
        name="expert_ffn",
    )(blk_e, nused, next_e, xs, b_up, b_down, w_up, w_down)


COMB_PIECE = 32
TAIL_BITS = tuple(1 << b for b in range(COMB_PIECE.bit_length() - 2, -1, -1))
COMB_UNROLL = 16


def _comb_cap(tm):
    return TOP_K * tm + N_EXPERTS * (COMB_PIECE - 1)


def _combine_kernel(cstart_ref, clen_ref, cloc_ref, lpos_ref, gate_ref, x2_ref, gfin_ref, y_hbm, out_ref,
                    chunks, rt, sems, *, tm, n_tiles):
    i = pl.program_id(0)
    slot = i % 2
    tile_rows = TOP_K * tm * ROW_TILES

    def fetch_runs(tile, to_slot):
        def per_e(e, c):
            length = clen_ref[tile * N_EXPERTS + e]
            src0 = cstart_ref[tile * N_EXPERTS + e]
            dst0 = to_slot * _comb_cap(tm) + cloc_ref[tile * N_EXPERTS + e]

            def rows_copy(off, rows):
                src = pl.multiple_of((src0 + off) * ROW_TILES, ROW_TILES)
                dst = pl.multiple_of((dst0 + off) * ROW_TILES, ROW_TILES)
                pltpu.make_async_copy(y_hbm.at[pl.ds(src, rows * ROW_TILES), :],
                                      chunks.at[pl.ds(dst, rows * ROW_TILES), :], sems.at[to_slot]).start()

            whole = lax.shift_right_logical(length, COMB_PIECE.bit_length() - 1)

            def per_p(p, c2):
                rows_copy(p * COMB_PIECE, COMB_PIECE)
                return c2

            lax.fori_loop(0, whole, per_p, 0)
            off = whole * COMB_PIECE
            for bit in TAIL_BITS:
                take = (length & bit) != 0

                @pl.when(take)
                def _(off=off, bit=bit):
                    rows_copy(off, bit)

                off = off + jnp.where(take, bit, 0)
            return c

        lax.fori_loop(0, N_EXPERTS, per_e, 0)

    def wait_tile(s):
        pltpu.make_async_copy(y_hbm.at[pl.ds(0, tile_rows), :],
                              chunks.at[pl.ds(s * _comb_cap(tm) * ROW_TILES, tile_rows), :], sems.at[s]).wait()

    @pl.when(i == 0)
    def _():
        fetch_runs(0, 0)

    @pl.when(i + 1 < n_tiles)
    def _():
        fetch_runs(i + 1, 1 - slot)

    wait_tile(slot)

    def tok_group(tt, c):
        for u in range(COMB_UNROLL):
            t = tt * COMB_UNROLL + u
            acc = None
            for k in range(TOP_K):
                r = pl.multiple_of(lpos_ref[k * tm + t], ROW_TILES)
                term = gate_ref[k * tm + t] * chunks[pl.ds(r, ROW_TILES), :]
                acc = term if acc is None else acc + term
            rt[pl.ds(pl.multiple_of(t * ROW_TILES, ROW_TILES), ROW_TILES), :] = acc
        return c

    lax.fori_loop(0, tm // COMB_UNROLL, tok_group, 0)

    moe = jnp.concatenate([rt[pl.ds(s, tm, stride=ROW_TILES), :] for s in range(ROW_TILES)], axis=-1)
    out_ref[...] = _rms(x2_ref[...] + moe, gfin_ref[...])


def _combine(cstart, clen, cloc, lpos, gate, x2, g_final, y, *, tm):
    n = x2.shape[0]
    n_tiles = n // tm
    gs = pltpu.PrefetchScalarGridSpec(
        num_scalar_prefetch=3,
        grid=(n_tiles,),
        in_specs=[pl.BlockSpec((TOP_K * tm,), lambda i, a, b, c: (i,), memory_space=pltpu.SMEM),
                  pl.BlockSpec((TOP_K * tm,), lambda i, a, b, c: (i,), memory_space=pltpu.SMEM),
                  pl.BlockSpec((tm, D_MODEL), lambda i, a, b, c: (i, 0)),
                  pl.BlockSpec((1, D_MODEL), lambda i, a, b, c: (0, 0)),
                  pl.BlockSpec(memory_space=pl.ANY)],
        out_specs=pl.BlockSpec((tm, D_MODEL), lambda i, a, b, c: (i, 0)),
        scratch_shapes=[pltpu.VMEM((2 * _comb_cap(tm) * ROW_TILES, LANES), F32),
                        pltpu.VMEM((tm * ROW_TILES, LANES), F32),
                        pltpu.SemaphoreType.DMA((2,))],
    )
    return pl.pallas_call(
        functools.partial(_combine_kernel, tm=tm, n_tiles=n_tiles),
        grid_spec=gs,
        out_shape=jax.ShapeDtypeStruct((n, D_MODEL), F32),
        compiler_params=_cparams(("arbitrary",)),
        name="combine",
    )(cstart, clen, cloc, lpos, gate, x2, g_final, y)


def _head_pad_cols(w, per_head_in, take, place):
    kdim = w.shape[0]
    w3 = w.reshape(kdim, MLA_HEADS, per_head_in)
    out = jnp.zeros((kdim, MLA_HEADS, HEAD_PAD), w.dtype)
    for (t0, t1), p0 in zip(take, place):
        out = out.at[:, :, p0:p0 + (t1 - t0)].set(w3[:, :, t0:t1])
    return out.reshape(kdim, MLA_HEADS * HEAD_PAD)


def kernel(x, mem, positions, g_mix, w_in, g_q, w_uq, g_kv, w_ukv, w_dw, b_dw, g_conv_ln, b_conv_ln, w_out,
           g_xattn, g_mem, w_xq, w_xkv, w_xo, g_ffn, w_router, b_router, w_up, b_up, w_down, b_down, g_final):
    batch, seq, _ = x.shape
    mem_len = mem.shape[1]
    n = batch * seq
    half = MLA_ROPE // 2
    r0 = MLA_NOPE
    assert w_in.shape[0] == 1, "one trunk layer"

    row = lambda v: v.reshape(1, -1).astype(F32)

    wi = w_in[0]
    o1, o2, o3 = Q_LORA, Q_LORA + KV_LORA, Q_LORA + KV_LORA + MLA_ROPE
    kr_blk = jnp.zeros((D_MODEL, HEAD_PAD), F32).at[:, r0:r0 + MLA_ROPE].set(wi[:, o2:o3])
    kr_swp = (jnp.zeros((D_MODEL, HEAD_PAD), F32)
              .at[:, r0:r0 + half].set(wi[:, o2 + half:o3])
              .at[:, r0 + half:r0 + MLA_ROPE].set(wi[:, o2:o2 + half]))
    w_in_p = jnp.concatenate([wi[:, :o2], kr_blk, kr_swp, wi[:, o3:]], axis=1).astype(BF16)
    per_q = MLA_NOPE + MLA_ROPE
    w_uq_p = _head_pad_cols(w_uq[0], per_q, [(0, per_q)], [0]).astype(BF16)
    w_uq_s = _head_pad_cols(w_uq[0], per_q, [(MLA_NOPE + half, per_q), (MLA_NOPE, MLA_NOPE + half)],
                            [r0, r0 + half]).astype(BF16)
    per_kv = MLA_NOPE + MLA_V
    w_uk_k = _head_pad_cols(w_ukv[0], per_kv, [(0, MLA_NOPE)], [0]).astype(BF16)
    w_uk_v = (w_ukv[0].reshape(KV_LORA, MLA_HEADS, per_kv)[:, :, MLA_NOPE:]
              .reshape(KV_LORA, MLA_WIDTH).astype(BF16))
    inv = ROPE_THETA ** (-jnp.arange(0, MLA_ROPE, 2, dtype=F32) / MLA_ROPE)
    inv_c = (jnp.zeros((LANES, 1), F32).at[r0:r0 + half, 0].set(inv).at[r0 + half:r0 + MLA_ROPE, 0].set(inv))
    sgn_c = (jnp.zeros((LANES, 1), F32).at[r0:r0 + half, 0].set(-1.0).at[r0 + half:r0 + MLA_ROPE, 0].set(1.0))

    x2d = x.reshape(n, D_MODEL)
    pos_row = positions.reshape(1, n).astype(I32)

    q_t, k, v_t, y_conv = _mix_in(x2d, pos_row, inv_c, sgn_c, row(g_mix[0]), w_in_p, row(g_q[0]),
                                  jnp.transpose(w_uq_p), jnp.transpose(w_uq_s), row(g_kv[0]), w_uk_k,
                                  jnp.transpose(w_uk_v), w_dw[0].astype(F32), row(b_dw[0]),
                                  row(g_conv_ln[0]), row(b_conv_ln[0]), seq=seq)
    y_mla = _attn(q_t, k, v_t, batch=batch, seq=seq)
    kv = _mem_kv(mem.reshape(batch * mem_len, D_MODEL), row(g_mem[0]), w_xkv[0].astype(BF16),
                 batch=batch, mem_len=mem_len)

    tm_mid = min(TM_MID, seq)
    wr = jnp.zeros((D_MODEL, LANES), F32).at[:, :N_EXPERTS].set(w_router[0])
    wr_h = wr.astype(BF16)
    wr_l = (wr - wr_h.astype(F32)).astype(BF16)
    b_r = b_router[0].reshape(N_EXPERTS, 1).astype(F32)
    tri = (lax.broadcasted_iota(I32, (tm_mid, tm_mid), 0)
           < lax.broadcasted_iota(I32, (tm_mid, tm_mid), 1)).astype(BF16)
    wo = w_out[0].astype(BF16)
    x2, h3r, idx, gate, rank, cnt, runs_l = _mid(
        x2d, y_mla, y_conv, kv, wo[:MLA_WIDTH], wo[MLA_WIDTH:], row(g_xattn[0]), w_xq[0].astype(BF16),
        w_xo[0].astype(BF16), row(g_ffn[0]), wr_h, wr_l, b_r, tri, seq=seq, mem_len=mem_len, tm=tm_mid)

    tb = TB_EXPERT
    counts = cnt[:, 0]
    padded = (counts + tb - 1) // tb * tb
    pend = jnp.cumsum(padded)
    pstart = pend - padded
    n_slots = n * TOP_K + N_EXPERTS * tb
    nb = n_slots // tb
    blk_first = jnp.arange(nb, dtype=I32) * tb
    blk_e = jnp.minimum(jnp.sum((pend[None, :] <= blk_first[:, None]).astype(I32), axis=1),
                        N_EXPERTS - 1).astype(I32)
    nused = (pend[-1:] // tb).astype(I32)
    eids = jnp.arange(N_EXPERTS, dtype=I32)
    later_used = jnp.logical_and(eids[None, :] > eids[:, None], (padded > 0)[None, :])
    next_e = jnp.min(jnp.where(later_used, eids[None, :], N_EXPERTS), axis=1)
    next_e = jnp.where(next_e == N_EXPERTS, -1, next_e).astype(I32)

    runs = runs_l[:, :, 0]
    run_len = jnp.concatenate([runs[1:], counts[None, :]], axis=0) - runs
    cstart = pstart[None, :] + runs
    cpieces = (run_len + COMB_PIECE - 1) // COMB_PIECE
    cloc = jnp.cumsum(cpieces * COMB_PIECE, axis=1) - cpieces * COMB_PIECE
    adj_l = jnp.broadcast_to((cloc - runs)[:, :, None], runs_l.shape).astype(I32)

    tm_comb = min(COMB_TM, tm_mid)
    lpos = _pos(idx, rank, adj_l, tm=tm_comb)
    tiled = lambda a: a.reshape(TOP_K, n // tm_comb, tm_comb).transpose(1, 0, 2).reshape(-1)
    lpos_tok = tiled(lpos)
    flat = lambda a: a.reshape(-1).astype(I32)
    tail_blocks = (n_slots - pend[-1:]) // RUN_BITS[0]
    fill_start = jnp.concatenate([pstart + counts, pend[-1:]]).astype(I32)
    fill_count = jnp.concatenate([padded - counts, tail_blocks]).astype(I32)
    xs = _dispatch(fill_start, fill_count, flat(cstart), flat(run_len),
                   flat(cloc), lpos_tok, h3r.reshape(n, PACK_TILES, LANES), n_slots=n_slots, tm=tm_comb)
    y = _expert(blk_e, nused, next_e, xs.reshape(n_slots * PACK_TILES, LANES), w_up[0],
                b_up[0].reshape(N_EXPERTS, 1, 2 * D_FF), w_down[0], b_down[0].reshape(N_EXPERTS, 1, D_MODEL), tb=tb)
    out = _combine(flat(cstart), flat(run_len), flat(cloc), lpos_tok, tiled(gate),
                   x2, row(g_final), y, tm=tm_comb)
    return out.reshape(batch, seq, D_MODEL)
```

```python
import functools

import jax
import jax.numpy as jnp
from jax import lax
from jax.experimental import pallas as pl
from jax.experimental.pallas import tpu as pltpu

F32 = jnp.float32
BF16 = jnp.bfloat16
I32 = jnp.int32

D_MODEL = 1024
MLA_HEADS = 8
MLA_NOPE = 64
MLA_ROPE = 32
MLA_V = 64
MLA_WIDTH = MLA_HEADS * MLA_V
Q_LORA = 256
KV_LORA = 128
ROPE_THETA = 10000.0
CHUNK = 64
CONV_CH = 512
CONV_WIDTH = 31
X_HEADS = 4
X_HEAD_DIM = D_MODEL // X_HEADS
N_EXPERTS = 32
TOP_K = 4
D_FF = D_MODEL
SWIGLU_LIMIT = 7.0
SWIGLU_ALPHA = 1.702
NORM_EPS = 1e-5

LANES = 128
SUBLANES = 8
ROW_TILES = D_MODEL // LANES
PACK_TILES = ROW_TILES // 2
VMEM_LIMIT_BYTES = 56 * 1024 * 1024

HEAD_PAD = LANES
Q_PRESCALE = (MLA_NOPE + MLA_ROPE) ** -0.5 * 1.4426950408889634
CONV_HALO = 32

TM_MIX = 512
TQ_ATTN = 1024
TM_MID = 512
TB_EXPERT = 512
MID_SPLIT = 2
COMB_TM = 512


def _rms(x, g):
    return x * lax.rsqrt(jnp.mean(x * x, axis=-1, keepdims=True) + NORM_EPS) * g


def _cparams(sem):
    return pltpu.CompilerParams(dimension_semantics=sem, vmem_limit_bytes=VMEM_LIMIT_BYTES)


def _mix_in_kernel(x_ref, pos_ref, inv_ref, sgn_ref, gmix_ref, win_ref, gq_ref, wuq_ref, wuqs_ref,
                   gkv_ref, wukk_ref, wukv_ref, wdw_ref, bdw_ref, gln_ref, bln_ref,
                   q_ref, k_ref, v_ref, yc_ref, zbuf, zsh, *, tiles_per_batch, tm):
    i = pl.program_id(0)
    x = x_ref[...]
    h = _rms(x, gmix_ref[...]).astype(BF16)
    u = jnp.dot(h, win_ref[...], preferred_element_type=F32)
    c0 = Q_LORA + KV_LORA
    c1 = c0 + 2 * HEAD_PAD
    cq = u[:, 0:Q_LORA]
    ckv = u[:, Q_LORA:c0]
    kr = u[:, c0:c0 + HEAD_PAD]
    krs = u[:, c0 + HEAD_PAD:c1]
    a = u[:, c1:c1 + CONV_CH]
    gate = u[:, c1 + CONV_CH:c1 + 2 * CONV_CH]

    r0, r1 = MLA_NOPE, MLA_NOPE + MLA_ROPE
    ang_r = inv_ref[r0:r1, :] * pos_ref[...].astype(F32)
    cos_t = jnp.concatenate([jnp.ones((r0, tm), F32), jnp.cos(ang_r), jnp.ones((HEAD_PAD - r1, tm), F32)], axis=0)
    sin_t = jnp.concatenate([jnp.zeros((r0, tm), F32), jnp.sin(ang_r) * sgn_ref[r0:r1, :],
                             jnp.zeros((HEAD_PAD - r1, tm), F32)], axis=0)
    cosb = jnp.transpose(cos_t)
    sinb = jnp.transpose(sin_t)
    cos_q = cos_t * Q_PRESCALE
    sin_q = sin_t * Q_PRESCALE

    nt = (((1,), (1,)), ((), ()))
    cqn = _rms(cq, gq_ref[...]).astype(BF16)
    qm = lax.dot_general(wuq_ref[...], cqn, nt, preferred_element_type=F32)
    qs = lax.dot_general(wuqs_ref[...], cqn, nt, preferred_element_type=F32)
    for hd in range(MLA_HEADS):
        sl = slice(hd * HEAD_PAD, (hd + 1) * HEAD_PAD)
        q_ref[sl, :] = (qm[sl, :] * cos_q + qs[sl, :] * sin_q).astype(BF16)

    ckvn = _rms(ckv, gkv_ref[...]).astype(BF16)
    kk = jnp.dot(ckvn, wukk_ref[...], preferred_element_type=F32)
    v_ref[...] = lax.dot_general(wukv_ref[...], ckvn, nt, preferred_element_type=F32).astype(BF16)
    krot = kr * cosb + krs * sinb
    for hd in range(MLA_HEADS):
        sl = slice(hd * HEAD_PAD, (hd + 1) * HEAD_PAD)
        k_ref[:, sl] = (kk[:, sl] + krot).astype(BF16)

    z = a * jax.nn.sigmoid(gate)

    @pl.when(i % tiles_per_batch == 0)
    def _():
        zbuf[0:CONV_HALO, :] = jnp.zeros((CONV_HALO, CONV_CH), F32)

    zbuf[CONV_HALO:CONV_HALO + tm, :] = z
    off = CONV_HALO - (CONV_WIDTH - 1)
    rows = 32
    span = tm + CONV_HALO - SUBLANES
    step = 64
    for r in range(1, SUBLANES):
        for c0 in range(0, span, step):
            cl = min(step, span - c0)
            zsh[r - 1, c0:c0 + cl, :] = zbuf[c0 + r:c0 + r + cl, :]
    for r0 in range(0, tm, rows):
        acc = jnp.zeros((rows, CONV_CH), F32) + bdw_ref[...]
        for j in range(CONV_WIDTH):
            a8, ph = divmod(j + off, SUBLANES)
            lo_row = r0 + a8 * SUBLANES
            tap = zbuf[lo_row:lo_row + rows, :] if ph == 0 else zsh[ph - 1, lo_row:lo_row + rows, :]
            acc = acc + wdw_ref[j:j + 1, :] * tap
        mu = jnp.mean(acc, axis=-1, keepdims=True)
        cen = acc - mu
        var = jnp.mean(cen * cen, axis=-1, keepdims=True)
        y = cen * lax.rsqrt(var + NORM_EPS) * gln_ref[...] + bln_ref[...]
        yc_ref[r0:r0 + rows, :] = (y * jax.nn.sigmoid(y)).astype(BF16)
    zbuf[0:CONV_HALO, :] = zbuf[tm:tm + CONV_HALO, :]


def _mix_in(x2d, pos_row, inv_c, sgn_c, g_mix, w_in_p, g_q, w_uq_t, w_uq_st, g_kv, w_uk_k, w_uk_vt,
            w_dw, b_dw, g_ln, b_ln, *, seq):
    n = x2d.shape[0]
    tm = min(TM_MIX, seq)
    full = lambda a: pl.BlockSpec(a.shape, lambda i: (0,) * a.ndim)
    consts = [inv_c, sgn_c, g_mix, w_in_p, g_q, w_uq_t, w_uq_st, g_kv, w_uk_k, w_uk_vt, w_dw, b_dw, g_ln, b_ln]
    return pl.pallas_call(
        functools.partial(_mix_in_kernel, tiles_per_batch=seq // tm, tm=tm),
        grid=(n // tm,),
        in_specs=[pl.BlockSpec((tm, D_MODEL), lambda i: (i, 0)),
                  pl.BlockSpec((1, tm), lambda i: (0, i))] + [full(a) for a in consts],
        out_specs=[pl.BlockSpec((MLA_HEADS * HEAD_PAD, tm), lambda i: (0, i)),
                   pl.BlockSpec((tm, MLA_HEADS * HEAD_PAD), lambda i: (i, 0)),
                   pl.BlockSpec((MLA_WIDTH, tm), lambda i: (0, i)),
                   pl.BlockSpec((tm, CONV_CH), lambda i: (i, 0))],
        out_shape=[jax.ShapeDtypeStruct((MLA_HEADS * HEAD_PAD, n), BF16),
                   jax.ShapeDtypeStruct((n, MLA_HEADS * HEAD_PAD), BF16),
                   jax.ShapeDtypeStruct((MLA_WIDTH, n), BF16),
                   jax.ShapeDtypeStruct((n, CONV_CH), BF16)],
        scratch_shapes=[pltpu.VMEM((CONV_HALO + tm + SUBLANES, CONV_CH), F32),
                        pltpu.VMEM((SUBLANES - 1, CONV_HALO + tm, CONV_CH), F32)],
        compiler_params=_cparams(("arbitrary",)),
        name="mix_in",
    )(x2d, pos_row, *consts)


ATTN_GROUP = 4


TK_ATTN = 1024
DENOM_ROWS = 16


def _attn_kernel(q_ref, k_ref, v_ref, o_ref, *s_scr, tq):
    i = pl.program_id(1)
    tk = min(TK_ATTN, tq)
    per_q = tq // tk
    key_c = lax.broadcasted_iota(I32, (tk, tq), 0) // CHUNK
    qry_c = lax.broadcasted_iota(I32, (tk, tq), 1) // CHUNK

    for g in range(MLA_HEADS // ATTN_GROUP):
        heads = list(range(g * ATTN_GROUP, (g + 1) * ATTN_GROUP))
        qts = [q_ref[hd * HEAD_PAD:(hd + 1) * HEAD_PAD, :] for hd in heads]

        def step(j, carry, diag):
            start = pl.multiple_of(j * tk, tk)
            new = []
            for n_, hd in enumerate(heads):
                m, acc = carry[n_]
                kj = k_ref[pl.ds(start, tk), hd * HEAD_PAD:(hd + 1) * HEAD_PAD]
                vj = jnp.concatenate([v_ref[hd * MLA_V:(hd + 1) * MLA_V, pl.ds(start, tk)],
                                      jnp.ones((DENOM_ROWS, tk), BF16)], axis=0)
                s = jnp.dot(kj, qts[n_], preferred_element_type=F32)
                if diag is not None:
                    s = jnp.where(key_c + diag * (tk // CHUNK) <= qry_c, s, -jnp.inf)
                m_new = jnp.maximum(m, jnp.max(s, axis=0, keepdims=True))
                s_scr[n_][...] = s
                s = s_scr[n_][...]
                alpha = jnp.exp2(m - m_new)
                p = jnp.exp2(s - m_new)
                acc = alpha * acc + jnp.dot(vj, p.astype(BF16), preferred_element_type=F32)
                new.append((m_new, acc))
            return tuple(new)

        init = tuple((jnp.full((1, tq), -jnp.inf, F32), jnp.zeros((MLA_V + DENOM_ROWS, tq), F32)) for _ in heads)
        carry = lax.fori_loop(0, i * per_q, functools.partial(step, diag=None), init)
        for d in range(per_q):
            carry = step(i * per_q + d, carry, d)
        for n_, hd in enumerate(heads):
            _, acc = carry[n_]
            out = acc[0:MLA_V, :] / acc[MLA_V:MLA_V + 1, :]
            o_ref[:, hd * MLA_V:(hd + 1) * MLA_V] = jnp.transpose(out).astype(BF16)


def _attn(q_t, k, v_t, *, batch, seq):
    tq = min(TQ_ATTN, seq)
    nq = seq // tq
    return pl.pallas_call(
        functools.partial(_attn_kernel, tq=tq),
        grid=(batch, nq),
        in_specs=[pl.BlockSpec((MLA_HEADS * HEAD_PAD, tq), lambda b, i: (0, b * nq + i)),
                  pl.BlockSpec((seq, MLA_HEADS * HEAD_PAD), lambda b, i: (b, 0)),
                  pl.BlockSpec((MLA_WIDTH, seq), lambda b, i: (0, b))],
        out_specs=pl.BlockSpec((tq, MLA_WIDTH), lambda b, i: (b * nq + i, 0)),
        out_shape=jax.ShapeDtypeStruct((batch * seq, MLA_WIDTH), BF16),
        scratch_shapes=[pltpu.VMEM((min(TK_ATTN, tq), tq), F32) for _ in range(ATTN_GROUP)],
        compiler_params=_cparams(("arbitrary", "arbitrary")),
        name="mla_attn",
    )(q_t, k, v_t)


def _mem_kv_kernel(mem_ref, g_ref, w_ref, kv_ref):
    mn = _rms(mem_ref[...], g_ref[...]).astype(BF16)
    kv_ref[...] = jnp.dot(mn, w_ref[...], preferred_element_type=F32).astype(BF16)


def _mem_kv(mem2d, g_mem, w_xkv, *, batch, mem_len):
    return pl.pallas_call(
        _mem_kv_kernel,
        grid=(batch,),
        in_specs=[pl.BlockSpec((mem_len, D_MODEL), lambda b: (b, 0)),
                  pl.BlockSpec((1, D_MODEL), lambda b: (0, 0)),
                  pl.BlockSpec((D_MODEL, 2 * D_MODEL), lambda b: (0, 0))],
        out_specs=pl.BlockSpec((mem_len, 2 * D_MODEL), lambda b: (b, 0)),
        out_shape=jax.ShapeDtypeStruct((batch * mem_len, 2 * D_MODEL), BF16),
        compiler_params=_cparams(("arbitrary",)),
        name="mem_kv",
    )(mem2d, g_mem, w_xkv)


def _mid_kernel(x_ref, ya_ref, yc_ref, kv_ref, woa_ref, wob_ref, gx_ref, wxq_ref, wxo_ref, gf_ref,
                wrh_ref, wrl_ref, br_ref, tri_ref,
                x2_ref, h3_ref, idx_ref, gate_ref, rank_ref, cnt_ref, runs_ref, run_ref, *, tm):
    i = pl.program_id(0)

    @pl.when(i == 0)
    def _():
        run_ref[...] = jnp.zeros_like(run_ref)

    nt = (((1,), (1,)), ((), ()))
    half_d = D_MODEL // 2
    rows = tm // MID_SPLIT
    lgs = []
    for part in range(MID_SPLIT):
        rs = slice(part * rows, (part + 1) * rows)
        x1 = (x_ref[rs, :]
              + jnp.dot(ya_ref[rs, :], woa_ref[...], preferred_element_type=F32)
              + jnp.dot(yc_ref[rs, :], wob_ref[...], preferred_element_type=F32))
        h = _rms(x1, gx_ref[...]).astype(BF16)
        q = jnp.dot(h, wxq_ref[...], preferred_element_type=F32).astype(BF16)
        os = []
        for hd in range(X_HEADS):
            sl = slice(hd * X_HEAD_DIM, (hd + 1) * X_HEAD_DIM)
            vsl = slice(D_MODEL + hd * X_HEAD_DIM, D_MODEL + (hd + 1) * X_HEAD_DIM)
            s = lax.dot_general(q[:, sl], kv_ref[:, sl], nt, preferred_element_type=F32) * (X_HEAD_DIM ** -0.5)
            e = jnp.exp(s - jnp.max(s, axis=-1, keepdims=True))
            p = (e / jnp.sum(e, axis=-1, keepdims=True)).astype(BF16)
            os.append(jnp.dot(p, kv_ref[:, vsl], preferred_element_type=F32).astype(BF16))
        o = jnp.concatenate(os, axis=-1)
        x2 = x1 + jnp.dot(o, wxo_ref[...], preferred_element_type=F32)
        x2_ref[rs, :] = x2
        h3 = _rms(x2, gf_ref[...])
        hh = h3.astype(BF16)
        hf = hh.astype(F32)
        word = pltpu.pack_elementwise([h3[:, :half_d], h3[:, half_d:]], packed_dtype=BF16)
        for t in range(PACK_TILES):
            h3_ref[pl.ds(part * rows * PACK_TILES + t, rows, stride=PACK_TILES), :] = word[:, t * LANES:(t + 1) * LANES]

        hl = (h3 - hf).astype(BF16)
        lgs.append(jnp.dot(hh, wrh_ref[...], preferred_element_type=F32)
                   + (jnp.dot(hh, wrl_ref[...], preferred_element_type=F32)
                      + jnp.dot(hl, wrh_ref[...], preferred_element_type=F32)))
    lg = jnp.concatenate(lgs, axis=0)
    lgt = jnp.transpose(lg)[0:N_EXPERTS, :] + br_ref[...]
    eid = lax.broadcasted_iota(I32, (N_EXPERTS, tm), 0)
    vals, hots = [], []
    cur = lgt
    for k in range(TOP_K):
        mx = jnp.max(cur, axis=0, keepdims=True)
        ik = jnp.min(jnp.where(cur == mx, eid, N_EXPERTS), axis=0, keepdims=True)
        hot = eid == ik
        cur = jnp.where(hot, -jnp.inf, cur)
        idx_ref[k:k + 1, :] = ik
        vals.append(mx)
        hots.append(hot)
    es = [jnp.exp(v - vals[0]) for v in vals]
    den = es[0] + es[1] + es[2] + es[3]
    for k in range(TOP_K):
        gate_ref[k:k + 1, :] = es[k] / den
    cnt = jnp.zeros((N_EXPERTS, tm), F32)
    for k in range(TOP_K):
        cnt = cnt + hots[k].astype(F32)
    run = run_ref[:, 0:1]
    ctm = min(COMB_TM, tm)
    for j in range(tm // ctm):
        before = run if j == 0 else run + jnp.sum(cnt[:, :j * ctm], axis=1, keepdims=True)
        runs_ref[j] = jnp.broadcast_to(before, (N_EXPERTS, LANES)).astype(I32)
    tot = jnp.dot(cnt.astype(BF16), tri_ref[...], preferred_element_type=F32) + run
    for k in range(TOP_K):
        rank_ref[k:k + 1, :] = jnp.sum(jnp.where(hots[k], tot, 0.0), axis=0, keepdims=True).astype(I32)
    new_run = run + jnp.sum(cnt, axis=1, keepdims=True)
    run_ref[...] = jnp.broadcast_to(new_run, run_ref.shape)
    cnt_ref[...] = jnp.broadcast_to(new_run, cnt_ref.shape).astype(I32)


def _mid(x2d, ya, yc, kv, w_oa, w_ob, g_x, w_xq, w_xo, g_f, wr_h, wr_l, b_r, tri, *, seq, mem_len, tm):
    n = x2d.shape[0]
    tpb = seq // tm
    full = lambda a: pl.BlockSpec(a.shape, lambda i: (0,) * a.ndim)
    consts_a = [w_oa, w_ob, g_x, w_xq, w_xo, g_f, wr_h, wr_l, b_r, tri]
    return pl.pallas_call(
        functools.partial(_mid_kernel, tm=tm),
        grid=(n // tm,),
        in_specs=[pl.BlockSpec((tm, D_MODEL), lambda i: (i, 0)),
                  pl.BlockSpec((tm, MLA_WIDTH), lambda i: (i, 0)),
                  pl.BlockSpec((tm, CONV_CH), lambda i: (i, 0)),
                  pl.BlockSpec((mem_len, 2 * D_MODEL), lambda i: (i // tpb, 0))] + [full(a) for a in consts_a],
        out_specs=[pl.BlockSpec((tm, D_MODEL), lambda i: (i, 0)),
                   pl.BlockSpec((tm * PACK_TILES, LANES), lambda i: (i, 0)),
                   pl.BlockSpec((TOP_K, tm), lambda i: (0, i)),
                   pl.BlockSpec((TOP_K, tm), lambda i: (0, i)),
                   pl.BlockSpec((TOP_K, tm), lambda i: (0, i)),
                   pl.BlockSpec((N_EXPERTS, LANES), lambda i: (0, 0)),
                   pl.BlockSpec((tm // min(COMB_TM, tm), N_EXPERTS, LANES), lambda i: (i, 0, 0))],
        out_shape=[jax.ShapeDtypeStruct((n, D_MODEL), F32),
                   jax.ShapeDtypeStruct((n * PACK_TILES, LANES), jnp.uint32),
                   jax.ShapeDtypeStruct((TOP_K, n), I32),
                   jax.ShapeDtypeStruct((TOP_K, n), F32),
                   jax.ShapeDtypeStruct((TOP_K, n), I32),
                   jax.ShapeDtypeStruct((N_EXPERTS, LANES), I32),
                   jax.ShapeDtypeStruct((n // min(COMB_TM, tm), N_EXPERTS, LANES), I32)],
        scratch_shapes=[pltpu.VMEM((N_EXPERTS, LANES), F32)],
        compiler_params=_cparams(("arbitrary",)),
        name="mid",
    )(x2d, ya, yc, kv, *consts_a)


def _pos_kernel(idx_ref, rank_ref, adj_ref, lpos_ref, *, tm, group):
    i = pl.program_id(0)
    eid = lax.broadcasted_iota(I32, (N_EXPERTS, tm), 0)
    for j in range(group):
        half = ((i * group + j) % 2) * _comb_cap(tm)
        cols = slice(j * tm, (j + 1) * tm)
        adj = adj_ref[j][:, 0:1] + half
        for k in range(TOP_K):
            hot = eid == idx_ref[k:k + 1, cols]
            lpos_ref[k:k + 1, cols] = (jnp.sum(jnp.where(hot, adj, 0), axis=0, keepdims=True)
                                       + rank_ref[k:k + 1, cols]) * ROW_TILES


def _pos(idx, rank, adj_l, *, tm):
    n = idx.shape[1]
    n_tiles = n // tm
    group = 4 if n_tiles % 4 == 0 else 1
    return pl.pallas_call(
        functools.partial(_pos_kernel, tm=tm, group=group),
        grid=(n_tiles // group,),
        in_specs=[pl.BlockSpec((TOP_K, tm * group), lambda i: (0, i)),
                  pl.BlockSpec((TOP_K, tm * group), lambda i: (0, i)),
                  pl.BlockSpec((group, N_EXPERTS, LANES), lambda i: (i, 0, 0))],
        out_specs=pl.BlockSpec((TOP_K, tm * group), lambda i: (0, i)),
        out_shape=jax.ShapeDtypeStruct((TOP_K, n), I32),
        compiler_params=_cparams(("arbitrary",)),
        name="slot_pos",
    )(idx, rank, adj_l)


RUN_BITS = tuple(1 << b for b in range(9, -1, -1))


def _pow2_runs(length, fn):
    off = jnp.int32(0)
    for bit in RUN_BITS:
        take = (length & bit) != 0

        @pl.when(take)
        def _(off=off, bit=bit):
            fn(off, bit)

        off = off + jnp.where(take, bit, 0)


def _dispatch_kernel(pad_start_ref, pad_cnt_ref, cstart_ref, clen_ref, cloc_ref, lpos_ref, h3_ref, xs_hbm,
                     cbuf, sems, zsem, *, tm, n_tiles):
    i = pl.program_id(0)
    slot = i % 2
    cap = _comb_cap(tm)
    rows_per_tile = TOP_K * tm

    @pl.when(i == 0)
    def _():
        zrows = RUN_BITS[0]
        cbuf[pl.ds(cap, zrows)] = jnp.zeros((zrows, PACK_TILES, LANES), jnp.uint32)

        def pad_pass(start):
            def per_expert(e, c):
                def one(off, bit):
                    cp = pltpu.make_async_copy(cbuf.at[pl.ds(cap, bit)],
                                               xs_hbm.at[pl.ds(pad_start_ref[e] + off, bit)], zsem)
                    cp.start() if start else cp.wait()

                _pow2_runs(pad_cnt_ref[e], one)
                return c

            lax.fori_loop(0, N_EXPERTS, per_expert, 0)

        pad_pass(True)
        pad_pass(False)

        def tail_pass(start):
            def one(j, c):
                cp = pltpu.make_async_copy(cbuf.at[pl.ds(cap, zrows)],
                                           xs_hbm.at[pl.ds(pad_start_ref[N_EXPERTS] + j * zrows, zrows)], zsem)
                cp.start() if start else cp.wait()
                return c

            lax.fori_loop(0, pad_cnt_ref[N_EXPERTS], one, 0)

        tail_pass(True)
        tail_pass(False)

    def wait_tile(s):
        pltpu.make_async_copy(cbuf.at[pl.ds(s * cap, rows_per_tile)], xs_hbm.at[pl.ds(0, rows_per_tile)],
                              sems.at[s]).wait()

    @pl.when(i >= 2)
    def _():
        wait_tile(slot)

    def tok_group(tt, c):
        for u in range(COMB_UNROLL):
            t = tt * COMB_UNROLL + u
            row = h3_ref[t]
            for k in range(TOP_K):
                cbuf[lax.shift_right_logical(lpos_ref[k * tm + t], 3)] = row
        return c

    lax.fori_loop(0, tm // COMB_UNROLL, tok_group, 0)

    def per_run(e, c):
        src0 = slot * cap + cloc_ref[i * N_EXPERTS + e]
        dst0 = cstart_ref[i * N_EXPERTS + e]

        def one(off, bit):
            pltpu.make_async_copy(cbuf.at[pl.ds(src0 + off, bit)], xs_hbm.at[pl.ds(dst0 + off, bit)],
                                  sems.at[slot]).start()

        _pow2_runs(clen_ref[i * N_EXPERTS + e], one)
        return c

    lax.fori_loop(0, N_EXPERTS, per_run, 0)

    @pl.when(i == n_tiles - 1)
    def _():
        wait_tile(slot)
        if n_tiles >= 2:
            wait_tile(1 - slot)


def _dispatch(pad_start, pad_cnt, cstart, clen, cloc, lpos_tok, h3r, *, n_slots, tm):
    n = h3r.shape[0]
    n_tiles = n // tm
    gs = pltpu.PrefetchScalarGridSpec(
        num_scalar_prefetch=5,
        grid=(n_tiles,),
        in_specs=[pl.BlockSpec((TOP_K * tm,), lambda i, *_: (i,), memory_space=pltpu.SMEM),
                  pl.BlockSpec((tm, PACK_TILES, LANES), lambda i, *_: (i, 0, 0))],
        out_specs=pl.BlockSpec(memory_space=pl.ANY),
        scratch_shapes=[pltpu.VMEM((2 * _comb_cap(tm), PACK_TILES, LANES), jnp.uint32),
                        pltpu.SemaphoreType.DMA((2,)),
                        pltpu.SemaphoreType.DMA(())],
    )
    return pl.pallas_call(
        functools.partial(_dispatch_kernel, tm=tm, n_tiles=n_tiles),
        grid_spec=gs,
        out_shape=jax.ShapeDtypeStruct((n_slots, PACK_TILES, LANES), jnp.uint32),
        compiler_params=_cparams(("arbitrary",)),
        name="dispatch",
    )(pad_start, pad_cnt, cstart, clen, cloc, lpos_tok, h3r)


def _expert_kernel(blk_e_ref, nused_ref, next_e_ref, x_ref, bu_ref, bd_ref, wu_hbm, wd_hbm, y_ref,
                   wu_f32, wd_f32, wu_bf, wd_bf, slot_ref, wsem, *, tb):
    b = pl.program_id(0)

    def weight_copies(e, slot):
        return (pltpu.make_async_copy(wu_hbm.at[e], wu_f32.at[slot], wsem.at[slot, 0]),
                pltpu.make_async_copy(wd_hbm.at[e], wd_f32.at[slot], wsem.at[slot, 1]))

    @pl.when(b >= nused_ref[0])
    def _():
        y_ref[...] = jnp.zeros_like(y_ref)

    @pl.when(b < nused_ref[0])
    def _():
        e = blk_e_ref[b]
        prev = blk_e_ref[jnp.maximum(b - 1, 0)]

        @pl.when(b == 0)
        def _():
            slot_ref[0] = 0
            for cp in weight_copies(e, 0):
                cp.start()

        @pl.when(jnp.logical_and(b > 0, e != prev))
        def _():
            slot_ref[0] = 1 - slot_ref[0]

        @pl.when(jnp.logical_or(b == 0, e != prev))
        def _():
            slot = slot_ref[0]
            for cp in weight_copies(e, slot):
                cp.wait()
            nxt = next_e_ref[e]

            @pl.when(nxt >= 0)
            def _():
                for cp in weight_copies(nxt, 1 - slot):
                    cp.start()

            wu_bf[...] = wu_f32[slot].astype(BF16)
            wd_bf[...] = wd_f32[slot].astype(BF16)

        words = [x_ref[pl.ds(t, tb, stride=PACK_TILES), :] for t in range(PACK_TILES)]
        lo = [pltpu.unpack_elementwise(w, index=0, packed_dtype=BF16, unpacked_dtype=F32) for w in words]
        hi = [pltpu.unpack_elementwise(w, index=1, packed_dtype=BF16, unpacked_dtype=F32) for w in words]
        x = jnp.concatenate(lo + hi, axis=-1).astype(BF16)
        up = jnp.dot(x, wu_bf[...], preferred_element_type=F32) + bu_ref[...]
        glu = jnp.minimum(up[:, :D_FF], SWIGLU_LIMIT)
        lin = jnp.clip(up[:, D_FF:], -SWIGLU_LIMIT, SWIGLU_LIMIT)
        act = (glu * jax.nn.sigmoid(SWIGLU_ALPHA * glu) * (lin + 1.0)).astype(BF16)
        y = jnp.dot(act, wd_bf[...], preferred_element_type=F32) + bd_ref[...]
        for t in range(ROW_TILES):
            y_ref[pl.ds(t, tb, stride=ROW_TILES), :] = y[:, t * LANES:(t + 1) * LANES]


def _expert(blk_e, nused, next_e, xs, w_up, b_up, w_down, b_down, *, tb):
    n_slots = xs.shape[0] // PACK_TILES
    nb = n_slots // tb
    blk = lambda b, be, nu: jnp.minimum(b, nu[0] - 1)
    gs = pltpu.PrefetchScalarGridSpec(
        num_scalar_prefetch=3,
        grid=(nb,),
        in_specs=[pl.BlockSpec((tb * PACK_TILES, LANES), lambda b, be, nu, ne: (blk(b, be, nu), 0)),
                  pl.BlockSpec((None, 1, 2 * D_FF), lambda b, be, nu, ne: (be[blk(b, be, nu)], 0, 0)),
                  pl.BlockSpec((None, 1, D_MODEL), lambda b, be, nu, ne: (be[blk(b, be, nu)], 0, 0)),
                  pl.BlockSpec(memory_space=pl.ANY),
                  pl.BlockSpec(memory_space=pl.ANY)],
        out_specs=pl.BlockSpec((tb * ROW_TILES, LANES), lambda b, be, nu, ne: (b, 0)),
        scratch_shapes=[pltpu.VMEM((2, D_MODEL, 2 * D_FF), F32),
                        pltpu.VMEM((2, D_FF, D_MODEL), F32),
                        pltpu.VMEM((D_MODEL, 2 * D_FF), BF16),
                        pltpu.VMEM((D_FF, D_MODEL), BF16),
                        pltpu.SMEM((1,), I32),
                        pltpu.SemaphoreType.DMA((2, 2))],
    )
    return pl.pallas_call(
        functools.partial(_expert_kernel, tb=tb),
        grid_spec=gs,
        out_shape=jax.ShapeDtypeStruct((n_slots * ROW_TILES, LANES), F32),
        compiler_params=_cparams(("arbitrary",)),
        name="expert_ffn",
    )(blk_e, nused, next_e, xs, b_up, b_down, w_up, w_down)


COMB_PIECE = 32
TAIL_BITS = tuple(1 << b for b in range(COMB_PIECE.bit_length() - 2, -1, -1))
COMB_UNROLL = 16


def _comb_cap(tm):
    return TOP_K * tm + N_EXPERTS * (COMB_PIECE - 1)


def _combine_kernel(cstart_ref, clen_ref, cloc_ref, lpos_ref, gate_ref, x2_ref, gfin_ref, y_hbm, out_ref,
                    chunks, rt, sems, *, tm, n_tiles):
    i = pl.program_id(0)
    slot = i % 2
    tile_rows = TOP_K * tm * ROW_TILES

    def fetch_runs(tile, to_slot):
        def per_e(e, c):
            length = clen_ref[tile * N_EXPERTS + e]
            src0 = cstart_ref[tile * N_EXPERTS + e]
            dst0 = to_slot * _comb_cap(tm) + cloc_ref[tile * N_EXPERTS + e]

            def rows_copy(off, rows):
                src = pl.multiple_of((src0 + off) * ROW_TILES, ROW_TILES)
                dst = pl.multiple_of((dst0 + off) * ROW_TILES, ROW_TILES)
                pltpu.make_async_copy(y_hbm.at[pl.ds(src, rows * ROW_TILES), :],
                                      chunks.at[pl.ds(dst, rows * ROW_TILES), :], sems.at[to_slot]).start()

            whole = lax.shift_right_logical(length, COMB_PIECE.bit_length() - 1)

            def per_p(p, c2):
                rows_copy(p * COMB_PIECE, COMB_PIECE)
                return c2

            lax.fori_loop(0, whole, per_p, 0)
            off = whole * COMB_PIECE
            for bit in TAIL_BITS:
                take = (length & bit) != 0

                @pl.when(take)
                def _(off=off, bit=bit):
                    rows_copy(off, bit)

                off = off + jnp.where(take, bit, 0)
            return c

        lax.fori_loop(0, N_EXPERTS, per_e, 0)

    def wait_tile(s):
        pltpu.make_async_copy(y_hbm.at[pl.ds(0, tile_rows), :],
                              chunks.at[pl.ds(s * _comb_cap(tm) * ROW_TILES, tile_rows), :], sems.at[s]).wait()

    @pl.when(i == 0)
    def _():
        fetch_runs(0, 0)

    @pl.when(i + 1 < n_tiles)
    def _():
        fetch_runs(i + 1, 1 - slot)

    wait_tile(slot)

    def tok_group(tt, c):
        for u in range(COMB_UNROLL):
            t = tt * COMB_UNROLL + u
            acc = None
            for k in range(TOP_K):
                r = pl.multiple_of(lpos_ref[k * tm + t], ROW_TILES)
                term = gate_ref[k * tm + t] * chunks[pl.ds(r, ROW_TILES), :]
                acc = term if acc is None else acc + term
            rt[pl.ds(pl.multiple_of(t * ROW_TILES, ROW_TILES), ROW_TILES), :] = acc
        return c

    lax.fori_loop(0, tm // COMB_UNROLL, tok_group, 0)

    moe = jnp.concatenate([rt[pl.ds(s, tm, stride=ROW_TILES), :] for s in range(ROW_TILES)], axis=-1)
    out_ref[...] = _rms(x2_ref[...] + moe, gfin_ref[...])


def _combine(cstart, clen, cloc, lpos, gate, x2, g_final, y, *, tm):
    n = x2.shape[0]
    n_tiles = n // tm
    gs = pltpu.PrefetchScalarGridSpec(
        num_scalar_prefetch=3,
        grid=(n_tiles,),
        in_specs=[pl.BlockSpec((TOP_K * tm,), lambda i, a, b, c: (i,), memory_space=pltpu.SMEM),
                  pl.BlockSpec((TOP_K * tm,), lambda i, a, b, c: (i,), memory_space=pltpu.SMEM),
                  pl.BlockSpec((tm, D_MODEL), lambda i, a, b, c: (i, 0)),
                  pl.BlockSpec((1, D_MODEL), lambda i, a, b, c: (0, 0)),
                  pl.BlockSpec(memory_space=pl.ANY)],
        out_specs=pl.BlockSpec((tm, D_MODEL), lambda i, a, b, c: (i, 0)),
        scratch_shapes=[pltpu.VMEM((2 * _comb_cap(tm) * ROW_TILES, LANES), F32),
                        pltpu.VMEM((tm * ROW_TILES, LANES), F32),
                        pltpu.SemaphoreType.DMA((2,))],
    )
    return pl.pallas_call(
        functools.partial(_combine_kernel, tm=tm, n_tiles=n_tiles),
        grid_spec=gs,
        out_shape=jax.ShapeDtypeStruct((n, D_MODEL), F32),
        compiler_params=_cparams(("arbitrary",)),
        name="combine",
    )(cstart, clen, cloc, lpos, gate, x2, g_final, y)


def _head_pad_cols(w, per_head_in, take, place):
    kdim = w.shape[0]
    w3 = w.reshape(kdim, MLA_HEADS, per_head_in)
    out = jnp.zeros((kdim, MLA_HEADS, HEAD_PAD), w.dtype)
    for (t0, t1), p0 in zip(take, place):
        out = out.at[:, :, p0:p0 + (t1 - t0)].set(w3[:, :, t0:t1])
    return out.reshape(kdim, MLA_HEADS * HEAD_PAD)


def kernel(x, mem, positions, g_mix, w_in, g_q, w_uq, g_kv, w_ukv, w_dw, b_dw, g_conv_ln, b_conv_ln, w_out,
           g_xattn, g_mem, w_xq, w_xkv, w_xo, g_ffn, w_router, b_router, w_up, b_up, w_down, b_down, g_final):
    batch, seq, _ = x.shape
    mem_len = mem.shape[1]
    n = batch * seq
    half = MLA_ROPE // 2
    r0 = MLA_NOPE
    assert w_in.shape[0] == 1, "one trunk layer"

    row = lambda v: v.reshape(1, -1).astype(F32)

    wi = w_in[0]
    o1, o2, o3 = Q_LORA, Q_LORA + KV_LORA, Q_LORA + KV_LORA + MLA_ROPE
    kr_blk = jnp.zeros((D_MODEL, HEAD_PAD), F32).at[:, r0:r0 + MLA_ROPE].set(wi[:, o2:o3])
    kr_swp = (jnp.zeros((D_MODEL, HEAD_PAD), F32)
              .at[:, r0:r0 + half].set(wi[:, o2 + half:o3])
              .at[:, r0 + half:r0 + MLA_ROPE].set(wi[:, o2:o2 + half]))
    w_in_p = jnp.concatenate([wi[:, :o2], kr_blk, kr_swp, wi[:, o3:]], axis=1).astype(BF16)
    per_q = MLA_NOPE + MLA_ROPE
    w_uq_p = _head_pad_cols(w_uq[0], per_q, [(0, per_q)], [0]).astype(BF16)
    w_uq_s = _head_pad_cols(w_uq[0], per_q, [(MLA_NOPE + half, per_q), (MLA_NOPE, MLA_NOPE + half)],
                            [r0, r0 + half]).astype(BF16)
    per_kv = MLA_NOPE + MLA_V
    w_uk_k = _head_pad_cols(w_ukv[0], per_kv, [(0, MLA_NOPE)], [0]).astype(BF16)
    w_uk_v = (w_ukv[0].reshape(KV_LORA, MLA_HEADS, per_kv)[:, :, MLA_NOPE:]
              .reshape(KV_LORA, MLA_WIDTH).astype(BF16))
    inv = ROPE_THETA ** (-jnp.arange(0, MLA_ROPE, 2, dtype=F32) / MLA_ROPE)
    inv_c = (jnp.zeros((LANES, 1), F32).at[r0:r0 + half, 0].set(inv).at[r0 + half:r0 + MLA_ROPE, 0].set(inv))
    sgn_c = (jnp.zeros((LANES, 1), F32).at[r0:r0 + half, 0].set(-1.0).at[r0 + half:r0 + MLA_ROPE, 0].set(1.0))

    x2d = x.reshape(n, D_MODEL)
    pos_row = positions.reshape(1, n).astype(I32)

    q_t, k, v_t, y_conv = _mix_in(x2d, pos_row, inv_c, sgn_c, row(g_mix[0]), w_in_p, row(g_q[0]),
                                  jnp.transpose(w_uq_p), jnp.transpose(w_uq_s), row(g_kv[0]), w_uk_k,
                                  jnp.transpose(w_uk_v), w_dw[0].astype(F32), row(b_dw[0]),
                                  row(g_conv_ln[0]), row(b_conv_ln[0]), seq=seq)
    y_mla = _attn(q_t, k, v_t, batch=batch, seq=seq)
    kv = _mem_kv(mem.reshape(batch * mem_len, D_MODEL), row(g_mem[0]), w_xkv[0].astype(BF16),
                 batch=batch, mem_len=mem_len)

    tm_mid = min(TM_MID, seq)
    wr = jnp.zeros((D_MODEL, LANES), F32).at[:, :N_EXPERTS].set(w_router[0])
    wr_h = wr.astype(BF16)
    wr_l = (wr - wr_h.astype(F32)).astype(BF16)
    b_r = b_router[0].reshape(N_EXPERTS, 1).astype(F32)
    tri = (lax.broadcasted_iota(I32, (tm_mid, tm_mid), 0)
           < lax.broadcasted_iota(I32, (tm_mid, tm_mid), 1)).astype(BF16)
    wo = w_out[0].astype(BF16)
    x2, h3r, idx, gate, rank, cnt, runs_l = _mid(
        x2d, y_mla, y_conv, kv, wo[:MLA_WIDTH], wo[MLA_WIDTH:], row(g_xattn[0]), w_xq[0].astype(BF16),
        w_xo[0].astype(BF16), row(g_ffn[0]), wr_h, wr_l, b_r, tri, seq=seq, mem_len=mem_len, tm=tm_mid)

    tb = TB_EXPERT
    counts = cnt[:, 0]
    padded = (counts + tb - 1) // tb * tb
    pend = jnp.cumsum(padded)
    pstart = pend - padded
    n_slots = n * TOP_K + N_EXPERTS * tb
    nb = n_slots // tb
    blk_first = jnp.arange(nb, dtype=I32) * tb
    blk_e = jnp.minimum(jnp.sum((pend[None, :] <= blk_first[:, None]).astype(I32), axis=1),
                        N_EXPERTS - 1).astype(I32)
    nused = (pend[-1:] // tb).astype(I32)
    eids = jnp.arange(N_EXPERTS, dtype=I32)
    later_used = jnp.logical_and(eids[None, :] > eids[:, None], (padded > 0)[None, :])
    next_e = jnp.min(jnp.where(later_used, eids[None, :], N_EXPERTS), axis=1)
    next_e = jnp.where(next_e == N_EXPERTS, -1, next_e).astype(I32)

    runs = runs_l[:, :, 0]
    run_len = jnp.concatenate([runs[1:], counts[None, :]], axis=0) - runs
    cstart = pstart[None, :] + runs
    cpieces = (run_len + COMB_PIECE - 1) // COMB_PIECE
    cloc = jnp.cumsum(cpieces * COMB_PIECE, axis=1) - cpieces * COMB_PIECE
    adj_l = jnp.broadcast_to((cloc - runs)[:, :, None], runs_l.shape).astype(I32)

    tm_comb = min(COMB_TM, tm_mid)
    lpos = _pos(idx, rank, adj_l, tm=tm_comb)
    tiled = lambda a: a.reshape(TOP_K, n // tm_comb, tm_comb).transpose(1, 0, 2).reshape(-1)
    lpos_tok = tiled(lpos)
    flat = lambda a: a.reshape(-1).astype(I32)
    tail_blocks = (n_slots - pend[-1:]) // RUN_BITS[0]
    fill_start = jnp.concatenate([pstart + counts, pend[-1:]]).astype(I32)
    fill_count = jnp.concatenate([padded - counts, tail_blocks]).astype(I32)
    xs = _dispatch(fill_start, fill_count, flat(cstart), flat(run_len),
                   flat(cloc), lpos_tok, h3r.reshape(n, PACK_TILES, LANES), n_slots=n_slots, tm=tm_comb)
    y = _expert(blk_e, nused, next_e, xs.reshape(n_slots * PACK_TILES, LANES), w_up[0],
                b_up[0].reshape(N_EXPERTS, 1, 2 * D_FF), w_down[0], b_down[0].reshape(N_EXPERTS, 1, D_MODEL), tb=tb)
    out = _combine(flat(cstart), flat(run_len), flat(cloc), lpos_tok, tiled(gate),
                   x2, row(g_final), y, tm=tm_comb)
    return out.reshape(batch, seq, D_MODEL)
```
